```python
import math
import jax, jax.numpy as jnp
from jax import lax
import numpy as np

D_MODEL = 1024
BATCH = 8
SEQ = 2048
DEPTH = 2
DEC_BATCH = 128
DEC_SEQ = 8
PAST_LEN = 16384
PAGE_SIZE = 128

CHUNK = 128
D_A = D_MODEL // 2
CH_A = 128
G_A = D_A // CH_A
D_B = D_MODEL // 2
EXPAND_B = 128
H_B = D_B // EXPAND_B
DK_B = EXPAND_B
DV_B = D_B // H_B
HGRN_CHUNK = 64
SPLITS = [D_A, D_A, D_B, D_B, D_B, D_B, D_MODEL, D_MODEL]
IN_COLS = sum(SPLITS)
D_FF = ((8 * D_MODEL // 3 + 127) // 128) * 128
N_EXPERTS = 8
TOP_K = 2
N_DENSE = (DEPTH + 1) // 2
N_MOE = DEPTH // 2
EPS = 1e-6

kernel_name = "gmlp_hgrn2_gated_hybrid_step"


def rms_norm(x, g):
    xf = x.astype(jnp.float32)
    y = xf * lax.rsqrt(jnp.mean(xf * xf, axis=-1, keepdims=True) + EPS)
    return (y * g.astype(jnp.float32)).astype(x.dtype)


def layer_norm(x, g, b):
    xf = x.astype(jnp.float32)
    mu = jnp.mean(xf, axis=-1, keepdims=True)
    xc = xf - mu
    y = xc * lax.rsqrt(jnp.mean(xc * xc, axis=-1, keepdims=True) + EPS)
    return (y * g.astype(jnp.float32) + b.astype(jnp.float32)).astype(x.dtype)


def chunk_spatial_gate(u, v, w_s, b_s):
    bsz, L, _ = v.shape
    n = -(-L // CHUNK)
    pad = n * CHUNK - L
    vp = jnp.pad(v, ((0, 0), (0, pad), (0, 0))).reshape(bsz, n, CHUNK, G_A, CH_A)
    w = w_s * jnp.tril(jnp.ones((CHUNK, CHUNK), w_s.dtype))
    mixed = jnp.einsum('gts,bnsgc->bntgc', w, vp) + b_s.T[None, None, :, :, None]
    mixed = mixed.reshape(bsz, n * CHUNK, D_A)[:, :L]
    return u * mixed


def hgrn2_recurrence(q, logf, k, v, s0):
    bsz, L, H, _ = q.shape
    DV = v.shape[-1]
    C = math.gcd(L, HGRN_CHUNK)
    n = L // C

    def to_chunks(t):
        return t.reshape(bsz, n, C, H, t.shape[-1]).swapaxes(0, 1)

    causal = jnp.tril(jnp.ones((C, C), bool))[None, :, :, None, None]

    def step(S, inp):
        qc, gc, kc, vc = inp
        G = jnp.cumsum(gc, axis=1)
        o_inter = jnp.einsum('bthd,bhdv->bthv', qc * jnp.exp(G), S)
        decay = jnp.exp(jnp.where(causal, G[:, :, None] - G[:, None, :], -jnp.inf))
        scores = jnp.einsum('bthd,bshd,btshd->bhts', qc, kc, decay)
        o_intra = jnp.einsum('bhts,bshv->bthv', scores, vc)
        G_last = G[:, -1]
        k_dec = kc * jnp.exp(G_last[:, None] - G)
        S = jnp.exp(G_last)[..., None] * S + jnp.einsum('bshd,bshv->bhdv', k_dec, vc)
        return S, o_inter + o_intra

    S, o = lax.scan(step, s0, (to_chunks(q), to_chunks(logf), to_chunks(k), to_chunks(v)))
    return o.swapaxes(0, 1).reshape(bsz, L, H, DV), S


def token_mixer(xn, s0, lb, w_in, ln_v_g, ln_v_b, w_spatial, b_spatial, hgrn_norm_g,
                w_branch_a, w_branch_b, w_out):
    bsz, L, _ = xn.shape
    z = xn @ w_in
    cuts = list(np.cumsum(SPLITS)[:-1])
    u, v, q, fz, iz, gz, gate_a, gate_b = jnp.split(z, cuts, axis=-1)
    u = jax.nn.gelu(u, approximate=False)
    v = layer_norm(jax.nn.gelu(v, approximate=False), ln_v_g, ln_v_b)
    a = chunk_spatial_gate(u, v, w_spatial, b_spatial)
    logf = jnp.logaddexp(jnp.log(lb), jnp.log1p(-lb) + jax.nn.log_sigmoid(fz.astype(jnp.float32)))
    kf = -jnp.expm1(logf)
    qf = jax.nn.silu(q.astype(jnp.float32))
    shp = (bsz, L, H_B, DK_B)
    o, s_new = hgrn2_recurrence(qf.reshape(shp), logf.reshape(shp), kf.reshape(shp),
                                iz.astype(jnp.float32).reshape(bsz, L, H_B, DV_B), s0)
    o = o * lax.rsqrt(jnp.mean(o * o, axis=-1, keepdims=True) + EPS)
    o = o * hgrn_norm_g.astype(jnp.float32).reshape(H_B, DV_B)
    b = o.reshape(bsz, L, D_B).astype(xn.dtype) * jax.nn.silu(gz)
    merged = jax.nn.sigmoid(gate_a) * (a @ w_branch_a) + jax.nn.sigmoid(gate_b) * (b @ w_branch_b)
    return merged @ w_out, s_new, v


def swiglu(h, w_gate, w_up, w_down):
    return (jax.nn.silu(h @ w_gate) * (h @ w_up)) @ w_down


def moe_swiglu(h, router_w, w_gate, w_up, w_down):
    logits = (h @ router_w).astype(jnp.float32)
    top_v, top_i = lax.top_k(logits, TOP_K)
    gates = jax.nn.softmax(top_v, axis=-1)
    combine = jnp.sum(jax.nn.one_hot(top_i, N_EXPERTS, dtype=jnp.float32) * gates[..., None], axis=-2)
    combine = combine.astype(h.dtype)
    out = jnp.zeros_like(h)
    for e in range(N_EXPERTS):
        out = out + combine[..., e:e + 1] * swiglu(h, w_gate[e], w_up[e], w_down[e])
    return out


def trunk(x, s_in, collect_rows, params):
    (norm_mix_g, w_in, ln_v_g, ln_v_b, w_spatial, b_spatial, lower_bounds, hgrn_norm_g,
     w_branch_a, w_branch_b, w_out, norm_ffn_g, dense_w_gate, dense_w_up, dense_w_down,
     router_w, moe_w_gate, moe_w_up, moe_w_down, final_norm_g) = params
    lb_cum = jnp.cumsum(jax.nn.softmax(lower_bounds.astype(jnp.float32), axis=0), axis=0)
    lb_all = lb_cum - lb_cum[0:1]
    h = x
    states, rows = [], []
    for l in range(DEPTH):
        mix, s_new, v = token_mixer(rms_norm(h, norm_mix_g[l]), s_in[l], lb_all[l], w_in[l],
                                    ln_v_g[l], ln_v_b[l], w_spatial[l], b_spatial[l],
                                    hgrn_norm_g[l], w_branch_a[l], w_branch_b[l], w_out[l])
        h = h + mix
        hn = rms_norm(h, norm_ffn_g[l])
        if l % 2 == 0:
            i = l // 2
            h = h + swiglu(hn, dense_w_gate[i], dense_w_up[i], dense_w_down[i])
        else:
            i = l // 2
            h = h + moe_swiglu(hn, router_w[i], moe_w_gate[i], moe_w_up[i], moe_w_down[i])
        states.append(s_new)
        if collect_rows:
            rows.append(v)
    y = rms_norm(h, final_norm_g)
    new_rows = jnp.stack(rows) if collect_rows else None
    return y, jnp.stack(states), new_rows


def setup_inputs(seed: int = 0) -> dict:
    key = jax.random.key(seed)
    ks = jax.random.split(key, 24)
    f32 = jnp.float32

    def nrm(k, shape, scale):
        return jax.random.normal(k, shape, f32) * scale

    def gain(k, shape):
        return 1.0 + 0.05 * jax.random.normal(k, shape, f32)

    return {
        "x_prompt": nrm(ks[0], (BATCH, SEQ, D_MODEL), 1.0),
        "x_sample": nrm(ks[1], (DEC_BATCH, DEC_SEQ, D_MODEL), 1.0),
        "state_hgrn": nrm(ks[2], (DEPTH, DEC_BATCH, H_B, DK_B, DV_B), 0.5),
        "norm_mix_g": gain(ks[3], (DEPTH, D_MODEL)),
        "w_in": nrm(ks[4], (DEPTH, D_MODEL, IN_COLS), D_MODEL ** -0.5),
        "ln_v_g": gain(ks[5], (DEPTH, D_A)),
        "ln_v_b": nrm(ks[6], (DEPTH, D_A), 0.02),
        "w_spatial": nrm(ks[7], (DEPTH, G_A, CHUNK, CHUNK), CHUNK ** -0.5),
        "b_spatial": gain(ks[8], (DEPTH, G_A, CHUNK)),
        "lower_bounds": nrm(ks[9], (DEPTH, D_B), 0.5),
        "hgrn_norm_g": gain(ks[10], (DEPTH, D_B)),
        "w_branch_a": nrm(ks[11], (DEPTH, D_A, D_MODEL), D_A ** -0.5),
        "w_branch_b": nrm(ks[12], (DEPTH, D_B, D_MODEL), D_B ** -0.5),
        "w_out": nrm(ks[13], (DEPTH, D_MODEL, D_MODEL), D_MODEL ** -0.5),
        "norm_ffn_g": gain(ks[14], (DEPTH, D_MODEL)),
        "dense_w_gate": nrm(ks[15], (N_DENSE, D_MODEL, D_FF), D_MODEL ** -0.5),
        "dense_w_up": nrm(ks[16], (N_DENSE, D_MODEL, D_FF), D_MODEL ** -0.5),
        "dense_w_down": nrm(ks[17], (N_DENSE, D_FF, D_MODEL), D_FF ** -0.5),
        "router_w": nrm(ks[18], (N_MOE, D_MODEL, N_EXPERTS), D_MODEL ** -0.5),
        "moe_w_gate": nrm(ks[19], (N_MOE, N_EXPERTS, D_MODEL, D_FF), D_MODEL ** -0.5),
        "moe_w_up": nrm(ks[20], (N_MOE, N_EXPERTS, D_MODEL, D_FF), D_MODEL ** -0.5),
        "moe_w_down": nrm(ks[21], (N_MOE, N_EXPERTS, D_FF, D_MODEL), D_FF ** -0.5),
        "final_norm_g": gain(ks[22], (D_MODEL,)),
    }


def reference(x_prompt, x_sample, state_hgrn, norm_mix_g, w_in, ln_v_g, ln_v_b, w_spatial,
              b_spatial, lower_bounds, hgrn_norm_g, w_branch_a, w_branch_b, w_out, norm_ffn_g,
              dense_w_gate, dense_w_up, dense_w_down, router_w, moe_w_gate, moe_w_up,
              moe_w_down, final_norm_g):
    params = (norm_mix_g, w_in, ln_v_g, ln_v_b, w_spatial, b_spatial, lower_bounds, hgrn_norm_g,
              w_branch_a, w_branch_b, w_out, norm_ffn_g, dense_w_gate, dense_w_up, dense_w_down,
              router_w, moe_w_gate, moe_w_up, moe_w_down, final_norm_g)
    s_zero = jnp.zeros((DEPTH, x_prompt.shape[0], H_B, DK_B, DV_B), jnp.float32)
    y_prompt, s_prompt, _ = trunk(x_prompt, s_zero, False, params)
    y_sample, s_sample, v_rows = trunk(x_sample, state_hgrn.astype(jnp.float32), True, params)
    new_state_hgrn_prompt = s_prompt.astype(x_prompt.dtype)
    new_state_hgrn_sample = s_sample.astype(state_hgrn.dtype)
    new_state_chunk_v_sample = v_rows
    return (y_prompt, y_sample, new_state_hgrn_prompt, new_state_hgrn_sample, new_state_chunk_v_sample)
```

```python
import functools

import numpy as np
import jax
import jax.numpy as jnp
from jax import lax
from jax.experimental import pallas as pl
from jax.experimental.pallas import tpu as pltpu

F32 = jnp.float32
BF16 = jnp.bfloat16

D_MODEL = 1024
DEPTH = 2
D_A = 512
G_A = 4
CH_A = 128
CHUNK = 128
D_B = 512
H_B = 4
DK = 128
IN_COLS = 5120
D_FF = 2816
N_EXPERTS = 8
EPS = 1e-6

LANES = 128
VMEM_LIMIT = 52 * 1024 * 1024

TM_PROJ = 256
TM_FFN = 256
R_HGRN = 256
SUB = 16
N_LEVELS = 4
SEQ_PER_STEP = 16


def _cparams(sem):
    return pltpu.CompilerParams(dimension_semantics=sem, vmem_limit_bytes=VMEM_LIMIT)


def _dot(a, b):
    return jnp.dot(a, b, preferred_element_type=F32)


def _dot_nt(a, b):
    return lax.dot_general(a, b, (((1,), (1,)), ((), ())), preferred_element_type=F32)


def _split3(x):
    hi = x.astype(BF16)
    r1 = x - hi.astype(F32)
    mid = r1.astype(BF16)
    lo = (r1 - mid.astype(F32)).astype(BF16)
    return hi, mid, lo


def _rms(x, g):
    return x * lax.rsqrt(jnp.mean(x * x, axis=-1, keepdims=True) + EPS) * g


def _gelu(x):
    return 0.5 * x * (1.0 + lax.erf(x * np.float32(2.0 ** -0.5)))


def _sigmoid(x):
    return jax.nn.sigmoid(x)


def _in_proj_body(x_ref, g_ref, w_ref, lng_ref, lnb_ref, la_ref, l1m_ref, oml_ref, wmix_ref,
                  bmix_ref, a_ref, vln_ref, q_ref, lf_ref, k_ref, iv_ref, gs_ref, sa_ref, sb_ref):
    tm = x_ref.shape[0]
    xn = _rms(x_ref[...], g_ref[...]).astype(BF16)

    def seg(lo, n):
        return _dot(xn, w_ref[:, lo:lo + n])

    u = _gelu(seg(0, D_A))
    v = _gelu(seg(D_A, D_A))
    vc = v - jnp.mean(v, axis=-1, keepdims=True)
    v = vc * lax.rsqrt(jnp.mean(vc * vc, axis=-1, keepdims=True) + EPS) * lng_ref[...] + lnb_ref[...]
    vln_ref[...] = v
    vb = v.astype(BF16)
    for c in range(tm // CHUNK):
        rows = slice(c * CHUNK, (c + 1) * CHUNK)
        for g in range(G_A):
            cols = slice(g * CH_A, (g + 1) * CH_A)
            mixed = _dot(wmix_ref[0, g], vb[rows, cols]) + bmix_ref[0, :, cols]
            a_ref[rows, cols] = u[rows, cols] * mixed

    zq = seg(2 * D_A, D_B)
    q_ref[...] = zq * _sigmoid(zq)
    fz = seg(2 * D_A + D_B, D_B)
    log_sig = jnp.minimum(fz, 0.0) - jnp.log1p(jnp.exp(-jnp.abs(fz)))
    b = l1m_ref[...] + log_sig
    la = la_ref[...]
    lf_ref[...] = jnp.maximum(la, b) + jnp.log1p(jnp.exp(-jnp.abs(la - b)))
    k_ref[...] = oml_ref[...] * _sigmoid(-fz)
    iv_ref[...] = seg(2 * D_A + 2 * D_B, D_B)
    gz = seg(2 * D_A + 3 * D_B, D_B)
    gs_ref[...] = gz * _sigmoid(gz)
    sa_ref[...] = _sigmoid(seg(2 * D_A + 4 * D_B, D_MODEL))
    sb_ref[...] = _sigmoid(seg(2 * D_A + 4 * D_B + D_MODEL, D_MODEL))


def _in_proj(x, g, w, lng, lnb, la, l1m, oml, wmix, bmix, n_prompt_tiles):
    t = x.shape[0]
    tm = TM_PROJ
    row = lambda n: pl.BlockSpec((1, n), lambda i: (0, 0))
    tile = lambda n: pl.BlockSpec((tm, n), lambda i: (i, 0))
    sel = lambda i: jnp.minimum(i // n_prompt_tiles, 1)
    out_shapes = [jax.ShapeDtypeStruct((t, D_A), F32)] * 7 + [jax.ShapeDtypeStruct((t, D_MODEL), F32)] * 2
    return pl.pallas_call(
        _in_proj_body,
        grid=(t // tm,),
        in_specs=[tile(D_MODEL), row(D_MODEL),
                  pl.BlockSpec((D_MODEL, IN_COLS), lambda i: (0, 0)),
                  row(D_A), row(D_A), row(D_B), row(D_B), row(D_B),
                  pl.BlockSpec((1, G_A, CHUNK, CHUNK), lambda i: (sel(i), 0, 0, 0)),
                  pl.BlockSpec((1, CHUNK, D_A), lambda i: (sel(i), 0, 0))],
        out_specs=[tile(D_A)] * 7 + [tile(D_MODEL)] * 2,
        out_shape=out_shapes,
        compiler_params=_cparams(("arbitrary",)),
        name="in_proj",
    )(x, g, w, lng, lnb, la, l1m, oml, wmix, bmix)


def _sub_chunk_exact(q_ref, k_ref, v_ref, g_ref, r0, cols, sub):
    qg = q_ref[pl.ds(r0, sub), cols]
    gg = g_ref[pl.ds(r0, sub), cols]
    kg = k_ref[pl.ds(r0, sub), cols]
    vg = v_ref[pl.ds(r0, sub), cols]
    row = lax.broadcasted_iota(jnp.int32, (sub, 1), 0)
    acc = jnp.zeros((sub, DK), F32)
    for s in range(sub):
        kb = kg[s:s + 1, :]
        gb = gg[s:s + 1, :]
        vb = vg[s:s + 1, :]
        p = qg * jnp.exp(gg - gb) * kb
        rs = jnp.sum(p, axis=-1, keepdims=True)
        acc = acc + jnp.where(row >= s, rs, 0.0) * vb
    return acc


def _hgrn_prompt_body(q_ref, lf_ref, k_ref, v_ref, cm_ref, lm_ref, gn_ref, o_ref, sfin_ref,
                      st_ref, g_scr, o_scr):
    j = pl.program_id(1)
    r = q_ref.shape[0]

    @pl.when(j == 0)
    def _():
        st_ref[...] = jnp.zeros_like(st_ref)

    parts = _split3(lf_ref[...])

    def cum(idx):
        m = cm_ref[idx]
        return _dot(m, parts[0]) + _dot(m, parts[1]) + _dot(m, parts[2])

    g_all = cum(0)
    g_scr[...] = g_all
    g_rev = cum(1)
    g_tot = g_all[r - 1:r, :]
    x_lv = [cum(2 + l) for l in range(N_LEVELS)]

    for h in range(H_B):
        cols = slice(h * DK, (h + 1) * DK)
        qh = q_ref[:, cols]
        kh = k_ref[:, cols]
        vh = v_ref[:, cols]
        st = st_ref[h]
        o = _dot_nt((qh * jnp.exp(g_all[:, cols])).astype(BF16), st.astype(BF16))
        sc = jnp.zeros((r, r), F32)
        for l in range(N_LEVELS):
            e = jnp.exp(x_lv[l][:, cols])
            sc = sc + _dot_nt((qh * e).astype(BF16), (kh * e).astype(BF16)) * lm_ref[l]
        o_scr[:, cols] = o + _dot(sc.astype(BF16), vh.astype(BF16))
        kt = (kh * jnp.exp(g_rev[:, cols])).astype(BF16)
        u = _dot(vh.T.astype(BF16), kt)
        st_ref[h] = st * jnp.exp(g_tot[:, cols]) + u

    def group(gi, carry):
        r0 = pl.multiple_of(gi * SUB, SUB)
        for h in range(H_B):
            cols = slice(h * DK, (h + 1) * DK)
            o_scr[pl.ds(r0, SUB), cols] += _sub_chunk_exact(q_ref, k_ref, v_ref, g_scr, r0, cols, SUB)
        return carry

    lax.fori_loop(0, r // SUB, group, 0)

    for h in range(H_B):
        cols = slice(h * DK, (h + 1) * DK)
        o_ref[:, cols] = _rms(o_scr[:, cols], gn_ref[:, cols])

    @pl.when(j == pl.num_programs(1) - 1)
    def _():
        for h in range(H_B):
            sfin_ref[0, h] = st_ref[h].T


def _hgrn_consts():
    r = R_HGRN
    i = np.arange(r)[:, None]
    j = np.arange(r)[None, :]
    mats = [(j <= i), (j > i)]
    masks = []
    for l in range(N_LEVELS):
        m = SUB << l
        blk_i, blk_j = i // (2 * m), j // (2 * m)
        right_i = (i % (2 * m)) >= m
        ref_i = blk_i * 2 * m + m - 1
        same = blk_i == blk_j
        w = same & np.where(right_i, (j > ref_i) & (j <= i), (j > i) & (j <= ref_i))
        mats.append(w)
        right_j = (j % (2 * m)) >= m
        masks.append(same & right_i & ~right_j)
    cm = jnp.asarray(np.stack(mats).astype(np.float32), dtype=BF16)
    lm = jnp.asarray(np.stack(masks).astype(np.float32))
    return cm, lm


def _hgrn_prompt(q, lf, k, iv, gn, n_seq, seq_len):
    r = R_HGRN
    nblk = seq_len // r
    cm, lm = _hgrn_consts()
    blk = pl.BlockSpec((r, D_B), lambda b, j: (b * nblk + j, 0))
    return pl.pallas_call(
        _hgrn_prompt_body,
        grid=(n_seq, nblk),
        in_specs=[blk, blk, blk, blk,
                  pl.BlockSpec(cm.shape, lambda b, j: (0, 0, 0)),
                  pl.BlockSpec(lm.shape, lambda b, j: (0, 0, 0)),
                  pl.BlockSpec((1, D_B), lambda b, j: (0, 0))],
        out_specs=[blk, pl.BlockSpec((1, H_B, DK, DK), lambda b, j: (b, 0, 0, 0))],
        out_shape=[jax.ShapeDtypeStruct((n_seq * seq_len, D_B), F32),
                   jax.ShapeDtypeStruct((n_seq, H_B, DK, DK), F32)],
        scratch_shapes=[pltpu.VMEM((H_B, DK, DK), F32),
                        pltpu.VMEM((r, D_B), F32),
                        pltpu.VMEM((r, D_B), F32)],
        compiler_params=_cparams(("arbitrary", "arbitrary")),
        name="hgrn_prompt",
    )(q, lf, k, iv, cm, lm, gn)


def _hgrn_sample_body(q_ref, lf_ref, k_ref, v_ref, s0_ref, cm_ref, gn_ref, o_ref, s1_ref,
                      g_scr, o_scr, *, dec_seq):
    rows = q_ref.shape[0]
    n_seq = rows // dec_seq
    parts = _split3(lf_ref[...])

    def cum(idx):
        m = cm_ref[idx]
        return _dot(m, parts[0]) + _dot(m, parts[1]) + _dot(m, parts[2])

    g_all = cum(0)
    g_scr[...] = g_all
    g_rev = cum(1)
    g_tot = cum(2)
    lane = lax.broadcasted_iota(jnp.int32, (DK, rows), 1)

    for h in range(H_B):
        cols = slice(h * DK, (h + 1) * DK)
        qt = (q_ref[:, cols] * jnp.exp(g_all[:, cols])).astype(BF16)
        kt_t = (k_ref[:, cols] * jnp.exp(g_rev[:, cols])).T
        dec_t = jnp.exp(g_tot[:, cols]).T
        vb = v_ref[:, cols].astype(BF16)
        for n in range(n_seq):
            r0 = n * dec_seq
            s0 = s0_ref[n, h]
            o_scr[r0:r0 + dec_seq, cols] = _dot(qt[r0:r0 + dec_seq, :], s0.astype(BF16))
            in_seq = (lane >= r0) & (lane < r0 + dec_seq)
            u = _dot(jnp.where(in_seq, kt_t, 0.0).astype(BF16), vb)
            s1_ref[n, h] = s0 * dec_t[:, r0:r0 + 1] + u

    def group(n, carry):
        r0 = pl.multiple_of(n * dec_seq, dec_seq)
        for h in range(H_B):
            cols = slice(h * DK, (h + 1) * DK)
            o_scr[pl.ds(r0, dec_seq), cols] += _sub_chunk_exact(q_ref, k_ref, v_ref, g_scr, r0, cols,
                                                                 dec_seq)
        return carry

    lax.fori_loop(0, n_seq, group, 0)

    for h in range(H_B):
        cols = slice(h * DK, (h + 1) * DK)
        o_ref[:, cols] = _rms(o_scr[:, cols], gn_ref[:, cols])


def _hgrn_sample(q, lf, k, iv, s0, gn, row0, n_seq, dec_seq):
    rows = SEQ_PER_STEP * dec_seq
    i = np.arange(rows)[:, None]
    j = np.arange(rows)[None, :]
    same = (i // dec_seq) == (j // dec_seq)
    cm = jnp.asarray(np.stack([same & (j <= i), same & (j > i), same]).astype(np.float32), dtype=BF16)
    blk0 = row0 // rows
    blk = pl.BlockSpec((rows, D_B), lambda n: (blk0 + n, 0))
    oblk = pl.BlockSpec((rows, D_B), lambda n: (n, 0))
    sblk = pl.BlockSpec((SEQ_PER_STEP, H_B, DK, DK), lambda n: (n, 0, 0, 0))
    return pl.pallas_call(
        functools.partial(_hgrn_sample_body, dec_seq=dec_seq),
        grid=(n_seq // SEQ_PER_STEP,),
        in_specs=[blk, blk, blk, blk, sblk,
                  pl.BlockSpec(cm.shape, lambda n: (0, 0, 0)),
                  pl.BlockSpec((1, D_B), lambda n: (0, 0))],
        out_specs=[oblk, sblk],
        out_shape=[jax.ShapeDtypeStruct((n_seq * dec_seq, D_B), F32),
                   jax.ShapeDtypeStruct((n_seq, H_B, DK, DK), F32)],
        scratch_shapes=[pltpu.VMEM((rows, D_B), F32), pltpu.VMEM((rows, D_B), F32)],
        compiler_params=_cparams(("arbitrary",)),
        name="hgrn_sample",
    )(q, lf, k, iv, s0, cm, gn)


def _post_mix_body(a_ref, op_ref, os_ref, gs_ref, sa_ref, sb_ref, h_ref, wa_ref, wb_ref, wo_ref,
                   gf_ref, h1_ref, hn_ref, *, n_prompt_tiles):
    i = pl.program_id(0)
    o = jnp.where(i < n_prompt_tiles, op_ref[...], os_ref[...])
    pa = _dot(a_ref[...].astype(BF16), wa_ref[...])
    pb = _dot((o * gs_ref[...]).astype(BF16), wb_ref[...])
    merged = sa_ref[...] * pa + sb_ref[...] * pb
    h1 = h_ref[...] + _dot(merged.astype(BF16), wo_ref[...])
    h1_ref[...] = h1
    hn_ref[...] = _rms(h1, gf_ref[...])


def _post_mix(a, o_p, o_s, gs, sa, sb, h, wa, wb, wo, gf, n_prompt_tiles):
    t = h.shape[0]
    tm = TM_PROJ
    tile = lambda n: pl.BlockSpec((tm, n), lambda i: (i, 0))
    full = lambda s: pl.BlockSpec(s, lambda i: (0, 0))
    return pl.pallas_call(
        functools.partial(_post_mix_body, n_prompt_tiles=n_prompt_tiles),
        grid=(t // tm,),
        in_specs=[tile(D_A),
                  pl.BlockSpec((tm, D_B), lambda i: (jnp.minimum(i, n_prompt_tiles - 1), 0)),
                  pl.BlockSpec((tm, D_B), lambda i: (jnp.maximum(i - n_prompt_tiles, 0), 0)),
                  tile(D_B), tile(D_MODEL), tile(D_MODEL), tile(D_MODEL),
                  full((D_A, D_MODEL)), full((D_B, D_MODEL)), full((D_MODEL, D_MODEL)),
                  full((1, D_MODEL))],
        out_specs=[tile(D_MODEL), tile(D_MODEL)],
        out_shape=[jax.ShapeDtypeStruct((t, D_MODEL), F32)] * 2,
        compiler_params=_cparams(("arbitrary",)),
        name="post_mix",
    )(a, o_p, o_s, gs, sa, sb, h, wa, wb, wo, gf)


FF_SPLITS = ((0, 1536), (1536, 1280))


def _swiglu(hb, wg_ref, wu_ref, wd_ref, idx):
    acc = None
    for lo, n in FF_SPLITS:
        g = _dot(hb, wg_ref[idx + (slice(None), slice(lo, lo + n))])
        u = _dot(hb, wu_ref[idx + (slice(None), slice(lo, lo + n))])
        act = (g * _sigmoid(g) * u).astype(BF16)
        part = _dot(act, wd_ref[idx + (slice(lo, lo + n), slice(None))])
        acc = part if acc is None else acc + part
    return acc


def _ffn_body(h1_ref, hn_ref, wg_ref, wu_ref, wd_ref, out_ref):
    hb = hn_ref[...].astype(BF16)
    out_ref[...] = h1_ref[...] + _swiglu(hb, wg_ref, wu_ref, wd_ref, ())


def _ffn_dense(h1, hn, wg, wu, wd):
    t = h1.shape[0]
    tm = TM_FFN
    tile = pl.BlockSpec((tm, D_MODEL), lambda i: (i, 0))
    return pl.pallas_call(
        _ffn_body,
        grid=(t // tm,),
        in_specs=[tile, tile,
                  pl.BlockSpec((D_MODEL, D_FF), lambda i: (0, 0)),
                  pl.BlockSpec((D_MODEL, D_FF), lambda i: (0, 0)),
                  pl.BlockSpec((D_FF, D_MODEL), lambda i: (0, 0))],
        out_specs=tile,
        out_shape=jax.ShapeDtypeStruct((t, D_MODEL), F32),
        compiler_params=_cparams(("arbitrary",)),
        name="ffn_dense",
    )(h1, hn, wg, wu, wd)


def _router_body(hn_ref, wh_ref, wl_ref, comb_ref):
    hn = hn_ref[...]
    hh = hn.astype(BF16)
    hl = (hn - hh.astype(F32)).astype(BF16)
    logits = _dot(hh, wh_ref[...]) + _dot(hh, wl_ref[...]) + _dot(hl, wh_ref[...])
    lane = lax.broadcasted_iota(jnp.int32, logits.shape, 1).astype(F32)
    neg = np.float32(-np.inf)
    logits = jnp.where(lane < N_EXPERTS, logits, neg)
    m1 = jnp.max(logits, axis=-1, keepdims=True)
    i1 = jnp.min(jnp.where(logits == m1, lane, float(LANES)), axis=-1, keepdims=True)
    rest = jnp.where(lane == i1, neg, logits)
    m2 = jnp.max(rest, axis=-1, keepdims=True)
    i2 = jnp.min(jnp.where(rest == m2, lane, float(LANES)), axis=-1, keepdims=True)
    e2 = jnp.exp(m2 - m1)
    den = 1.0 + e2
    comb_ref[...] = jnp.where(lane == i1, 1.0 / den, 0.0) + jnp.where(lane == i2, e2 / den, 0.0)


def _router(hn, wh, wl):
    t = hn.shape[0]
    tm = TM_FFN
    return pl.pallas_call(
        _router_body,
        grid=(t // tm,),
        in_specs=[pl.BlockSpec((tm, D_MODEL), lambda i: (i, 0)),
                  pl.BlockSpec((D_MODEL, LANES), lambda i: (0, 0)),
                  pl.BlockSpec((D_MODEL, LANES), lambda i: (0, 0))],
        out_specs=pl.BlockSpec((tm, LANES), lambda i: (i, 0)),
        out_shape=jax.ShapeDtypeStruct((t, LANES), F32),
        compiler_params=_cparams(("arbitrary",)),
        name="router",
    )(hn, wh, wl)


def _moe_body(acc_ref, hn_ref, comb_ref, wg_ref, wu_ref, wd_ref, gfin_ref, out_ref):
    e = pl.program_id(0)
    hb = hn_ref[...].astype(BF16)
    comb = comb_ref[...]
    lane = lax.broadcasted_iota(jnp.int32, comb.shape, 1)
    w = jnp.sum(jnp.where(lane == e, comb, 0.0), axis=-1, keepdims=True)
    val = acc_ref[...] + w * _swiglu(hb, wg_ref, wu_ref, wd_ref, (0,))
    last = pl.num_programs(0) - 1

    @pl.when(e < last)
    def _():
        out_ref[...] = val

    @pl.when(e == last)
    def _():
        out_ref[...] = _rms(val, gfin_ref[...])


def _moe_dense(h1, hn, comb, wg, wu, wd, gfin):
    t = h1.shape[0]
    tm = TM_FFN
    tile = lambda n: pl.BlockSpec((tm, n), lambda e, i: (i, 0))
    return pl.pallas_call(
        _moe_body,
        grid=(N_EXPERTS, t // tm),
        in_specs=[tile(D_MODEL), tile(D_MODEL), tile(LANES),
                  pl.BlockSpec((1, D_MODEL, D_FF), lambda e, i: (e, 0, 0)),
                  pl.BlockSpec((1, D_MODEL, D_FF), lambda e, i: (e, 0, 0)),
                  pl.BlockSpec((1, D_FF, D_MODEL), lambda e, i: (e, 0, 0)),
                  pl.BlockSpec((1, D_MODEL), lambda e, i: (0, 0))],
        out_specs=tile(D_MODEL),
        out_shape=jax.ShapeDtypeStruct((t, D_MODEL), F32),
        input_output_aliases={0: 0},
        compiler_params=_cparams(("arbitrary", "arbitrary")),
        name="moe_dense",
    )(h1, hn, comb, wg, wu, wd, gfin)


def _mix_consts(w_spatial, b_spatial, dec_seq):
    tril = jnp.tril(jnp.ones((CHUNK, CHUNK), F32))
    w_p = w_spatial * tril
    reps = CHUNK // dec_seq
    w_s = jnp.stack([jnp.kron(jnp.eye(reps, dtype=F32), w_p[g, :dec_seq, :dec_seq]) for g in range(G_A)])
    wmix = jnp.stack([w_p, w_s]).astype(BF16)
    b_p = jnp.repeat(b_spatial.T, CH_A, axis=1)
    b_s = jnp.tile(b_p[:dec_seq], (reps, 1))
    return wmix, jnp.stack([b_p, b_s])


def kernel(x_prompt, x_sample, state_hgrn, norm_mix_g, w_in, ln_v_g, ln_v_b, w_spatial, b_spatial,
           lower_bounds, hgrn_norm_g, w_branch_a, w_branch_b, w_out, norm_ffn_g, dense_w_gate,
           dense_w_up, dense_w_down, router_w, moe_w_gate, moe_w_up, moe_w_down, final_norm_g):
    n_seq, seq_len, _ = x_prompt.shape
    dec_batch, dec_seq, _ = x_sample.shape
    t_prompt = n_seq * seq_len
    t_sample = dec_batch * dec_seq
    n_prompt_tiles = t_prompt // TM_PROJ

    h = jnp.concatenate([x_prompt.reshape(t_prompt, D_MODEL), x_sample.reshape(t_sample, D_MODEL)])

    lb_cum = jnp.cumsum(jax.nn.softmax(lower_bounds.astype(F32), axis=0), axis=0)
    lb_all = lb_cum - lb_cum[0:1]
    log_lb, log1m_lb, one_m_lb = jnp.log(lb_all), jnp.log1p(-lb_all), 1.0 - lb_all

    row = lambda p: p.reshape(1, -1)
    states_p, states_s, v_rows = [], [], []
    for l in range(DEPTH):
        wmix, bmix = _mix_consts(w_spatial[l], b_spatial[l], dec_seq)
        a, vln, q, lf, k, iv, gs, sa, sb = _in_proj(
            h, row(norm_mix_g[l]), w_in[l].astype(BF16), row(ln_v_g[l]), row(ln_v_b[l]),
            row(log_lb[l]), row(log1m_lb[l]), row(one_m_lb[l]), wmix, bmix, n_prompt_tiles)
        gn = row(hgrn_norm_g[l])
        o_p, s_p = _hgrn_prompt(q, lf, k, iv, gn, n_seq, seq_len)
        o_s, s_s = _hgrn_sample(q, lf, k, iv, state_hgrn[l].astype(F32), gn, t_prompt, dec_batch, dec_seq)
        h1, hn = _post_mix(a, o_p, o_s, gs, sa, sb, h, w_branch_a[l].astype(BF16),
                           w_branch_b[l].astype(BF16), w_out[l].astype(BF16), row(norm_ffn_g[l]),
                           n_prompt_tiles)
        i = l // 2
        if l % 2 == 0:
            h = _ffn_dense(h1, hn, dense_w_gate[i].astype(BF16), dense_w_up[i].astype(BF16),
                           dense_w_down[i].astype(BF16))
        else:
            rw = jnp.pad(router_w[i], ((0, 0), (0, LANES - N_EXPERTS)))
            rw_hi = rw.astype(BF16)
            rw_lo = (rw - rw_hi.astype(F32)).astype(BF16)
            comb = _router(hn, rw_hi, rw_lo)
            h = _moe_dense(h1, hn, comb, moe_w_gate[i].astype(BF16), moe_w_up[i].astype(BF16),
                           moe_w_down[i].astype(BF16), row(final_norm_g))
        states_p.append(s_p)
        states_s.append(s_s)
        v_rows.append(vln[t_prompt:].reshape(dec_batch, dec_seq, D_A))

    y_prompt = h[:t_prompt].reshape(n_seq, seq_len, D_MODEL)
    y_sample = h[t_prompt:].reshape(dec_batch, dec_seq, D_MODEL)
    return (y_prompt, y_sample, jnp.stack(states_p).astype(x_prompt.dtype),
            jnp.stack(states_s).astype(state_hgrn.dtype), jnp.stack(v_rows))
```

```python
import functools

import numpy as np
import jax
import jax.numpy as jnp
from jax import lax
from jax.experimental import pallas as pl
from jax.experimental.pallas import tpu as pltpu

F32 = jnp.float32
BF16 = jnp.bfloat16

D_MODEL = 1024
DEPTH = 2
D_A = 512
G_A = 4
CH_A = 128
CHUNK = 128
D_B = 512
H_B = 4
DK = 128
IN_COLS = 5120
D_FF = 2816
N_EXPERTS = 8
EPS = 1e-6

LANES = 128
VMEM_LIMIT = 52 * 1024 * 1024

TM_PROJ = 256
TM_FFN = 256
R_HGRN = 256
SUB = 16
N_LEVELS = 4
SEQ_PER_STEP = 16


def _cparams(sem):
    return pltpu.CompilerParams(dimension_semantics=sem, vmem_limit_bytes=VMEM_LIMIT)


def _dot(a, b):
    return jnp.dot(a, b, preferred_element_type=F32)


def _dot_nt(a, b):
    return lax.dot_general(a, b, (((1,), (1,)), ((), ())), preferred_element_type=F32)


def _split3(x):
    hi = x.astype(BF16)
    r1 = x - hi.astype(F32)
    mid = r1.astype(BF16)
    lo = (r1 - mid.astype(F32)).astype(BF16)
    return hi, mid, lo


def _rms(x, g):
    return x * lax.rsqrt(jnp.mean(x * x, axis=-1, keepdims=True) + EPS) * g


def _gelu(x):
    return 0.5 * x * (1.0 + lax.erf(x * np.float32(2.0 ** -0.5)))


def _sigmoid(x):
    return jax.nn.sigmoid(x)


def _in_proj_body(x_ref, g_ref, w_ref, lng_ref, lnb_ref, la_ref, l1m_ref, oml_ref, wmix_ref,
                  bmix_ref, a_ref, vln_ref, q_ref, lf_ref, k_ref, iv_ref, gs_ref, sa_ref, sb_ref):
    tm = x_ref.shape[0]
    xn = _rms(x_ref[...], g_ref[...]).astype(BF16)

    def seg(lo, n):
        return _dot(xn, w_ref[:, lo:lo + n])

    u = _gelu(seg(0, D_A))
    v = _gelu(seg(D_A, D_A))
    vc = v - jnp.mean(v, axis=-1, keepdims=True)
    v = vc * lax.rsqrt(jnp.mean(vc * vc, axis=-1, keepdims=True) + EPS) * lng_ref[...] + lnb_ref[...]
    vln_ref[...] = v
    vb = v.astype(BF16)
    for c in range(tm // CHUNK):
        rows = slice(c * CHUNK, (c + 1) * CHUNK)
        for g in range(G_A):
            cols = slice(g * CH_A, (g + 1) * CH_A)
            mixed = _dot(wmix_ref[0, g], vb[rows, cols]) + bmix_ref[0, :, cols]
            a_ref[rows, cols] = u[rows, cols] * mixed

    zq = seg(2 * D_A, D_B)
    q_ref[...] = zq * _sigmoid(zq)
    fz = seg(2 * D_A + D_B, D_B)
    log_sig = jnp.minimum(fz, 0.0) - jnp.log1p(jnp.exp(-jnp.abs(fz)))
    b = l1m_ref[...] + log_sig
    la = la_ref[...]
    lf_ref[...] = jnp.maximum(la, b) + jnp.log1p(jnp.exp(-jnp.abs(la - b)))
    k_ref[...] = oml_ref[...] * _sigmoid(-fz)
    iv_ref[...] = seg(2 * D_A + 2 * D_B, D_B)
    gz = seg(2 * D_A + 3 * D_B, D_B)
    gs_ref[...] = gz * _sigmoid(gz)
    sa_ref[...] = _sigmoid(seg(2 * D_A + 4 * D_B, D_MODEL))
    sb_ref[...] = _sigmoid(seg(2 * D_A + 4 * D_B + D_MODEL, D_MODEL))


def _in_proj(x, g, w, lng, lnb, la, l1m, oml, wmix, bmix, n_prompt_tiles):
    t = x.shape[0]
    tm = TM_PROJ
    row = lambda n: pl.BlockSpec((1, n), lambda i: (0, 0))
    tile = lambda n: pl.BlockSpec((tm, n), lambda i: (i, 0))
    sel = lambda i: jnp.minimum(i // n_prompt_tiles, 1)
    out_shapes = [jax.ShapeDtypeStruct((t, D_A), F32)] * 7 + [jax.ShapeDtypeStruct((t, D_MODEL), F32)] * 2
    return pl.pallas_call(
        _in_proj_body,
        grid=(t // tm,),
        in_specs=[tile(D_MODEL), row(D_MODEL),
                  pl.BlockSpec((D_MODEL, IN_COLS), lambda i: (0, 0)),
                  row(D_A), row(D_A), row(D_B), row(D_B), row(D_B),
                  pl.BlockSpec((1, G_A, CHUNK, CHUNK), lambda i: (sel(i), 0, 0, 0)),
                  pl.BlockSpec((1, CHUNK, D_A), lambda i: (sel(i), 0, 0))],
        out_specs=[tile(D_A)] * 7 + [tile(D_MODEL)] * 2,
        out_shape=out_shapes,
        compiler_params=_cparams(("arbitrary",)),
        name="in_proj",
    )(x, g, w, lng, lnb, la, l1m, oml, wmix, bmix)


def _sub_chunk_exact(q_ref, k_ref, v_ref, g_ref, r0, cols, sub):
    qg = q_ref[pl.ds(r0, sub), cols]
    gg = g_ref[pl.ds(r0, sub), cols]
    kg = k_ref[pl.ds(r0, sub), cols]
    vg = v_ref[pl.ds(r0, sub), cols]
    row = lax.broadcasted_iota(jnp.int32, (sub, 1), 0)
    acc = jnp.zeros((sub, DK), F32)
    for s in range(sub):
        kb = kg[s:s + 1, :]
        gb = gg[s:s + 1, :]
        vb = vg[s:s + 1, :]
        p = qg * jnp.exp(gg - gb) * kb
        rs = jnp.sum(p, axis=-1, keepdims=True)
        acc = acc + jnp.where(row >= s, rs, 0.0) * vb
    return acc


def _hgrn_prompt_body(q_ref, lf_ref, k_ref, v_ref, cm_ref, lm_ref, gn_ref, o_ref, sfin_ref,
                      st_ref, g_scr, o_scr):
    j = pl.program_id(1)
    r = q_ref.shape[0]

    @pl.when(j == 0)
    def _():
        st_ref[...] = jnp.zeros_like(st_ref)

    parts = _split3(lf_ref[...])

    def cum(idx):
        m = cm_ref[idx]
        return _dot(m, parts[0]) + _dot(m, parts[1]) + _dot(m, parts[2])

    g_all = cum(0)
    g_scr[...] = g_all
    g_rev = cum(1)
    g_tot = g_all[r - 1:r, :]
    x_lv = [cum(2 + l) for l in range(N_LEVELS)]

    for h in range(H_B):
        cols = slice(h * DK, (h + 1) * DK)
        qh = q_ref[:, cols]
        kh = k_ref[:, cols]
        vh = v_ref[:, cols]
        st = st_ref[h]
        o = _dot_nt((qh * jnp.exp(g_all[:, cols])).astype(BF16), st.astype(BF16))
        sc = jnp.zeros((r, r), F32)
        for l in range(N_LEVELS):
            e = jnp.exp(x_lv[l][:, cols])
            sc = sc + _dot_nt((qh * e).astype(BF16), (kh * e).astype(BF16)) * lm_ref[l]
        o_scr[:, cols] = o + _dot(sc.astype(BF16), vh.astype(BF16))
        kt = (kh * jnp.exp(g_rev[:, cols])).astype(BF16)
        u = _dot(vh.T.astype(BF16), kt)
        st_ref[h] = st * jnp.exp(g_tot[:, cols]) + u

    def group(gi, carry):
        r0 = pl.multiple_of(gi * SUB, SUB)
        for h in range(H_B):
            cols = slice(h * DK, (h + 1) * DK)
            o_scr[pl.ds(r0, SUB), cols] += _sub_chunk_exact(q_ref, k_ref, v_ref, g_scr, r0, cols, SUB)
        return carry

    lax.fori_loop(0, r // SUB, group, 0)

    for h in range(H_B):
        cols = slice(h * DK, (h + 1) * DK)
        o_ref[:, cols] = _rms(o_scr[:, cols], gn_ref[:, cols])

    @pl.when(j == pl.num_programs(1) - 1)
    def _():
        for h in range(H_B):
            sfin_ref[0, h] = st_ref[h].T


def _hgrn_consts():
    r = R_HGRN
    i = np.arange(r)[:, None]
    j = np.arange(r)[None, :]
    mats = [(j <= i), (j > i)]
    masks = []
    for l in range(N_LEVELS):
        m = SUB << l
        blk_i, blk_j = i // (2 * m), j // (2 * m)
        right_i = (i % (2 * m)) >= m
        ref_i = blk_i * 2 * m + m - 1
        same = blk_i == blk_j
        w = same & np.where(right_i, (j > ref_i) & (j <= i), (j > i) & (j <= ref_i))
        mats.append(w)
        right_j = (j % (2 * m)) >= m
        masks.append(same & right_i & ~right_j)
    cm = jnp.asarray(np.stack(mats).astype(np.float32), dtype=BF16)
    lm = jnp.asarray(np.stack(masks).astype(np.float32))
    return cm, lm


def _hgrn_prompt(q, lf, k, iv, gn, n_seq, seq_len):
    r = R_HGRN
    nblk = seq_len // r
    cm, lm = _hgrn_consts()
    blk = pl.BlockSpec((r, D_B), lambda b, j: (b * nblk + j, 0))
    return pl.pallas_call(
        _hgrn_prompt_body,
        grid=(n_seq, nblk),
        in_specs=[blk, blk, blk, blk,
                  pl.BlockSpec(cm.shape, lambda b, j: (0, 0, 0)),
                  pl.BlockSpec(lm.shape, lambda b, j: (0, 0, 0)),
                  pl.BlockSpec((1, D_B), lambda b, j: (0, 0))],
        out_specs=[blk, pl.BlockSpec((1, H_B, DK, DK), lambda b, j: (b, 0, 0, 0))],
        out_shape=[jax.ShapeDtypeStruct((n_seq * seq_len, D_B), F32),
                   jax.ShapeDtypeStruct((n_seq, H_B, DK, DK), F32)],
        scratch_shapes=[pltpu.VMEM((H_B, DK, DK), F32),
                        pltpu.VMEM((r, D_B), F32),
                        pltpu.VMEM((r, D_B), F32)],
        compiler_params=_cparams(("arbitrary", "arbitrary")),
        name="hgrn_prompt",
    )(q, lf, k, iv, cm, lm, gn)


def _hgrn_sample_body(q_ref, lf_ref, k_ref, v_ref, s0_ref, cm_ref, gn_ref, o_ref, s1_ref,
                      g_scr, o_scr, *, dec_seq):
    rows = q_ref.shape[0]
    n_seq = rows // dec_seq
    parts = _split3(lf_ref[...])

    def cum(idx):
        m = cm_ref[idx]
        return _dot(m, parts[0]) + _dot(m, parts[1]) + _dot(m, parts[2])

    g_all = cum(0)
    g_scr[...] = g_all
    g_rev = cum(1)
    g_tot = cum(2)
    lane = lax.broadcasted_iota(jnp.int32, (DK, rows), 1)

    for h in range(H_B):
        cols = slice(h * DK, (h + 1) * DK)
        qt = (q_ref[:, cols] * jnp.exp(g_all[:, cols])).astype(BF16)
        kt_t = (k_ref[:, cols] * jnp.exp(g_rev[:, cols])).T
        dec_t = jnp.exp(g_tot[:, cols]).T
        vb = v_ref[:, cols].astype(BF16)
        for n in range(n_seq):
            r0 = n * dec_seq
            s0 = s0_ref[n, h]
            o_scr[r0:r0 + dec_seq, cols] = _dot(qt[r0:r0 + dec_seq, :], s0.astype(BF16))
            in_seq = (lane >= r0) & (lane < r0 + dec_seq)
            u = _dot(jnp.where(in_seq, kt_t, 0.0).astype(BF16), vb)
            s1_ref[n, h] = s0 * dec_t[:, r0:r0 + 1] + u

    def group(n, carry):
        r0 = pl.multiple_of(n * dec_seq, dec_seq)
        for h in range(H_B):
            cols = slice(h * DK, (h + 1) * DK)
            o_scr[pl.ds(r0, dec_seq), cols] += _sub_chunk_exact(q_ref, k_ref, v_ref, g_scr, r0, cols,
                                                                 dec_seq)
        return carry

    lax.fori_loop(0, n_seq, group, 0)

    for h in range(H_B):
        cols = slice(h * DK, (h + 1) * DK)
        o_ref[:, cols] = _rms(o_scr[:, cols], gn_ref[:, cols])


def _hgrn_sample(q, lf, k, iv, s0, gn, row0, n_seq, dec_seq):
    rows = SEQ_PER_STEP * dec_seq
    i = np.arange(rows)[:, None]
    j = np.arange(rows)[None, :]
    same = (i // dec_seq) == (j // dec_seq)
    cm = jnp.asarray(np.stack([same & (j <= i), same & (j > i), same]).astype(np.float32), dtype=BF16)
    blk0 = row0 // rows
    blk = pl.BlockSpec((rows, D_B), lambda n: (blk0 + n, 0))
    oblk = pl.BlockSpec((rows, D_B), lambda n: (n, 0))
    sblk = pl.BlockSpec((SEQ_PER_STEP, H_B, DK, DK), lambda n: (n, 0, 0, 0))
    return pl.pallas_call(
        functools.partial(_hgrn_sample_body, dec_seq=dec_seq),
        grid=(n_seq // SEQ_PER_STEP,),
        in_specs=[blk, blk, blk, blk, sblk,
                  pl.BlockSpec(cm.shape, lambda n: (0, 0, 0)),
                  pl.BlockSpec((1, D_B), lambda n: (0, 0))],
        out_specs=[oblk, sblk],
        out_shape=[jax.ShapeDtypeStruct((n_seq * dec_seq, D_B), F32),
                   jax.ShapeDtypeStruct((n_seq, H_B, DK, DK), F32)],
        scratch_shapes=[pltpu.VMEM((rows, D_B), F32), pltpu.VMEM((rows, D_B), F32)],
        compiler_params=_cparams(("arbitrary",)),
        name="hgrn_sample",
    )(q, lf, k, iv, s0, cm, gn)


def _post_mix_body(a_ref, op_ref, os_ref, gs_ref, sa_ref, sb_ref, h_ref, wa_ref, wb_ref, wo_ref,
                   gf_ref, h1_ref, hn_ref, *, n_prompt_tiles):
    i = pl.program_id(0)
    o = jnp.where(i < n_prompt_tiles, op_ref[...], os_ref[...])
    pa = _dot(a_ref[...].astype(BF16), wa_ref[...])
    pb = _dot((o * gs_ref[...]).astype(BF16), wb_ref[...])
    merged = sa_ref[...] * pa + sb_ref[...] * pb
    h1 = h_ref[...] + _dot(merged.astype(BF16), wo_ref[...])
    h1_ref[...] = h1
    hn_ref[...] = _rms(h1, gf_ref[...])


def _post_mix(a, o_p, o_s, gs, sa, sb, h, wa, wb, wo, gf, n_prompt_tiles):
    t = h.shape[0]
    tm = TM_PROJ
    tile = lambda n: pl.BlockSpec((tm, n), lambda i: (i, 0))
    full = lambda s: pl.BlockSpec(s, lambda i: (0, 0))
    return pl.pallas_call(
        functools.partial(_post_mix_body, n_prompt_tiles=n_prompt_tiles),
        grid=(t // tm,),
        in_specs=[tile(D_A),
                  pl.BlockSpec((tm, D_B), lambda i: (jnp.minimum(i, n_prompt_tiles - 1), 0)),
                  pl.BlockSpec((tm, D_B), lambda i: (jnp.maximum(i - n_prompt_tiles, 0), 0)),
                  tile(D_B), tile(D_MODEL), tile(D_MODEL), tile(D_MODEL),
                  full((D_A, D_MODEL)), full((D_B, D_MODEL)), full((D_MODEL, D_MODEL)),
                  full((1, D_MODEL))],
        out_specs=[tile(D_MODEL), tile(D_MODEL)],
        out_shape=[jax.ShapeDtypeStruct((t, D_MODEL), F32)] * 2,
        compiler_params=_cparams(("arbitrary",)),
        name="post_mix",
    )(a, o_p, o_s, gs, sa, sb, h, wa, wb, wo, gf)


FF_SPLITS = ((0, 1536), (1536, 1280))


def _swiglu(hb, wg_ref, wu_ref, wd_ref, idx):
    acc = None
    for lo, n in FF_SPLITS:
        g = _dot(hb, wg_ref[idx + (slice(None), slice(lo, lo + n))])
        u = _dot(hb, wu_ref[idx + (slice(None), slice(lo, lo + n))])
        act = (g * _sigmoid(g) * u).astype(BF16)
        part = _dot(act, wd_ref[idx + (slice(lo, lo + n), slice(None))])
        acc = part if acc is None else acc + part
    return acc


def _ffn_body(h1_ref, hn_ref, wg_ref, wu_ref, wd_ref, out_ref):
    hb = hn_ref[...].astype(BF16)
    out_ref[...] = h1_ref[...] + _swiglu(hb, wg_ref, wu_ref, wd_ref, ())


def _ffn_dense(h1, hn, wg, wu, wd):
    t = h1.shape[0]
    tm = TM_FFN
    tile = pl.BlockSpec((tm, D_MODEL), lambda i: (i, 0))
    return pl.pallas_call(
        _ffn_body,
        grid=(t // tm,),
        in_specs=[tile, tile,
                  pl.BlockSpec((D_MODEL, D_FF), lambda i: (0, 0)),
                  pl.BlockSpec((D_MODEL, D_FF), lambda i: (0, 0)),
                  pl.BlockSpec((D_FF, D_MODEL), lambda i: (0, 0))],
        out_specs=tile,
        out_shape=jax.ShapeDtypeStruct((t, D_MODEL), F32),
        compiler_params=_cparams(("arbitrary",)),
        name="ffn_dense",
    )(h1, hn, wg, wu, wd)


def _router_body(hn_ref, wh_ref, wl_ref, route_ref, gate_ref, cnt_ref, carry_ref):
    @pl.when(pl.program_id(0) == 0)
    def _():
        carry_ref[...] = jnp.zeros_like(carry_ref)

    tm = hn_ref.shape[0]
    hn = hn_ref[...]
    hh = hn.astype(BF16)
    hl = (hn - hh.astype(F32)).astype(BF16)
    logits = _dot(hh, wh_ref[...]) + _dot(hh, wl_ref[...]) + _dot(hl, wh_ref[...])
    lane = lax.broadcasted_iota(jnp.int32, logits.shape, 1).astype(F32)
    neg = np.float32(-np.inf)
    logits = jnp.where(lane < N_EXPERTS, logits, neg)
    m1 = jnp.max(logits, axis=-1, keepdims=True)
    i1 = jnp.min(jnp.where(logits == m1, lane, float(LANES)), axis=-1, keepdims=True)
    rest = jnp.where(lane == i1, neg, logits)
    m2 = jnp.max(rest, axis=-1, keepdims=True)
    i2 = jnp.min(jnp.where(rest == m2, lane, float(LANES)), axis=-1, keepdims=True)
    e2 = jnp.exp(m2 - m1)
    den = 1.0 + e2
    gate_ref[...] = jnp.where(lane == 0.0, 1.0 / den, jnp.where(lane == 1.0, e2 / den, 0.0))

    sel = jnp.where((lane == i1) | (lane == i2), 1.0, 0.0)
    r = lax.broadcasted_iota(jnp.int32, (tm, tm), 0)
    c = lax.broadcasted_iota(jnp.int32, (tm, tm), 1)
    before = jnp.where(c < r, 1.0, 0.0).astype(BF16)
    rank = _dot(before, sel.astype(BF16)) + carry_ref[...]
    rank1 = jnp.sum(jnp.where(lane == i1, rank, 0.0), axis=-1, keepdims=True)
    rank2 = jnp.sum(jnp.where(lane == i2, rank, 0.0), axis=-1, keepdims=True)
    route = jnp.where(lane == 0.0, i1, jnp.where(lane == 1.0, i2,
                      jnp.where(lane == 2.0, rank1, jnp.where(lane == 3.0, rank2, 0.0))))
    route_ref[...] = route.astype(jnp.int32)
    carry_ref[...] += jnp.sum(sel, axis=0, keepdims=True)
    cnt_ref[...] = jnp.broadcast_to(carry_ref[...], cnt_ref.shape).astype(jnp.int32)


def _router(hn, wh, wl):
    t = hn.shape[0]
    tm = TM_FFN
    return pl.pallas_call(
        _router_body,
        grid=(t // tm,),
        in_specs=[pl.BlockSpec((tm, D_MODEL), lambda i: (i, 0)),
                  pl.BlockSpec((D_MODEL, LANES), lambda i: (0, 0)),
                  pl.BlockSpec((D_MODEL, LANES), lambda i: (0, 0))],
        out_specs=[pl.BlockSpec((tm, LANES), lambda i: (i, 0)),
                   pl.BlockSpec((tm, LANES), lambda i: (i, 0)),
                   pl.BlockSpec((8, LANES), lambda i: (0, 0))],
        out_shape=[jax.ShapeDtypeStruct((t, LANES), jnp.int32),
                   jax.ShapeDtypeStruct((t, LANES), F32),
                   jax.ShapeDtypeStruct((8, LANES), jnp.int32)],
        scratch_shapes=[pltpu.VMEM((1, LANES), F32)],
        compiler_params=_cparams(("arbitrary",)),
        name="router",
    )(hn, wh, wl)


TOP_K = 2
TMG = 256


def _row_copy(src_ref, src_row, dst_ref, dst_row, sem, n=1):
    return pltpu.make_async_copy(src_ref.at[pl.ds(src_row, n)], dst_ref.at[pl.ds(dst_row, n)], sem)


def _dispatch_body(pad_start_ref, pad_len_ref, pos_ref, hn_ref, xs_ref, zero_ref, sem, zsem):
    tm = hn_ref.shape[0]

    @pl.when(pl.program_id(0) == 0)
    def _():
        zero_ref[...] = jnp.zeros_like(zero_ref)
        for e in range(N_EXPERTS):
            s = pad_start_ref[e]

            def fill(r, carry):
                _row_copy(zero_ref, 0, xs_ref, s + r, zsem).start()
                return carry

            def fill_wait(r, carry):
                _row_copy(zero_ref, 0, xs_ref, 0, zsem).wait()
                return carry

            lax.fori_loop(0, pad_len_ref[e], fill, 0)
            lax.fori_loop(0, pad_len_ref[e], fill_wait, 0)

    def issue(t, carry):
        for k in range(TOP_K):
            _row_copy(hn_ref, t, xs_ref, pos_ref[0, 0, TOP_K * t + k], sem).start()
        return carry

    def drain(t, carry):
        for k in range(TOP_K):
            _row_copy(hn_ref, 0, xs_ref, 0, sem).wait()
        return carry

    lax.fori_loop(0, tm, issue, 0, unroll=8)
    lax.fori_loop(0, tm, drain, 0, unroll=8)


def _dispatch(hn, pos, pad_start, pad_len, n_rows):
    t = hn.shape[0]
    tm = TM_FFN
    grid_spec = pltpu.PrefetchScalarGridSpec(
        num_scalar_prefetch=2,
        grid=(t // tm,),
        in_specs=[pl.BlockSpec((1, 1, TOP_K * tm), lambda i, ps, pn: (i, 0, 0), memory_space=pltpu.SMEM),
                  pl.BlockSpec((tm, D_MODEL), lambda i, ps, pn: (i, 0))],
        out_specs=pl.BlockSpec(memory_space=pl.ANY),
        scratch_shapes=[pltpu.VMEM((8, D_MODEL), F32),
                        pltpu.SemaphoreType.DMA, pltpu.SemaphoreType.DMA],
    )
    return pl.pallas_call(
        _dispatch_body,
        grid_spec=grid_spec,
        out_shape=jax.ShapeDtypeStruct((n_rows, D_MODEL), F32),
        compiler_params=_cparams(("arbitrary",)),
        name="moe_dispatch",
    )(pad_start, pad_len, pos, hn)


def _ffn_grouped_body(te_ref, nv_ref, xs_ref, wg_ref, wu_ref, wd_ref, ys_ref):
    @pl.when(pl.program_id(0) < nv_ref[0])
    def _():
        ys_ref[...] = _swiglu(xs_ref[...].astype(BF16), wg_ref, wu_ref, wd_ref, (0,))

    @pl.when(pl.program_id(0) >= nv_ref[0])
    def _():
        ys_ref[...] = jnp.zeros_like(ys_ref)


def _ffn_grouped(xs, tile_expert, n_valid, wg, wu, wd):
    n_rows = xs.shape[0]
    row_tile = lambda j, te, nv: (jnp.minimum(j, nv[0] - 1), 0)
    expert = lambda j, te, nv: (te[j], 0, 0)
    grid_spec = pltpu.PrefetchScalarGridSpec(
        num_scalar_prefetch=2,
        grid=(n_rows // TMG,),
        in_specs=[pl.BlockSpec((TMG, D_MODEL), row_tile),
                  pl.BlockSpec((1, D_MODEL, D_FF), expert),
                  pl.BlockSpec((1, D_MODEL, D_FF), expert),
                  pl.BlockSpec((1, D_FF, D_MODEL), expert)],
        out_specs=pl.BlockSpec((TMG, D_MODEL), lambda j, te, nv: (j, 0)),
    )
    return pl.pallas_call(
        _ffn_grouped_body,
        grid_spec=grid_spec,
        out_shape=jax.ShapeDtypeStruct((n_rows, D_MODEL), F32),
        compiler_params=_cparams(("arbitrary",)),
        name="moe_ffn",
    )(tile_expert, n_valid, xs, wg, wu, wd)


def _combine_body(pos_ref, gate_ref, h1_ref, gfin_ref, ys_ref, out_ref, y_scr, sem):
    tm = h1_ref.shape[0]

    def issue(t, carry):
        for k in range(TOP_K):
            _row_copy(ys_ref, pos_ref[0, 0, TOP_K * t + k], y_scr.at[k], t, sem).start()
        return carry

    def drain(t, carry):
        for k in range(TOP_K):
            _row_copy(ys_ref, 0, y_scr.at[k], 0, sem).wait()
        return carry

    lax.fori_loop(0, tm, issue, 0, unroll=8)
    lax.fori_loop(0, tm, drain, 0, unroll=8)
    gate = gate_ref[...]
    val = h1_ref[...] + gate[:, 0:1] * y_scr[0] + gate[:, 1:2] * y_scr[1]
    out_ref[...] = _rms(val, gfin_ref[...])


def _combine(ys, pos, gate, h1, gfin):
    t = h1.shape[0]
    tm = TM_FFN
    return pl.pallas_call(
        _combine_body,
        grid=(t // tm,),
        in_specs=[pl.BlockSpec((1, 1, TOP_K * tm), lambda i: (i, 0, 0), memory_space=pltpu.SMEM),
                  pl.BlockSpec((tm, LANES), lambda i: (i, 0)),
                  pl.BlockSpec((tm, D_MODEL), lambda i: (i, 0)),
                  pl.BlockSpec((1, D_MODEL), lambda i: (0, 0)),
                  pl.BlockSpec(memory_space=pl.ANY)],
        out_specs=pl.BlockSpec((tm, D_MODEL), lambda i: (i, 0)),
        out_shape=jax.ShapeDtypeStruct((t, D_MODEL), F32),
        scratch_shapes=[pltpu.VMEM((TOP_K, tm, D_MODEL), F32), pltpu.SemaphoreType.DMA],
        compiler_params=_cparams(("arbitrary",)),
        name="moe_combine",
    )(pos, gate, h1, gfin, ys)


def _moe_routed(h1, hn, route, gate, counts, wg, wu, wd, gfin):
    t = h1.shape[0]
    n_tiles = (TOP_K * t) // TMG + N_EXPERTS
    cnt = counts[0, :N_EXPERTS]
    padded = ((cnt + TMG - 1) // TMG) * TMG
    g_end = jnp.cumsum(padded)
    g_start = g_end - padded
    n_valid = g_end[-1] // TMG
    tile_row = jnp.minimum(jnp.arange(n_tiles, dtype=jnp.int32), n_valid - 1) * TMG
    tile_expert = jnp.sum(tile_row[:, None] >= g_end[None, :], axis=1).astype(jnp.int32)
    expert = route[:, 0:TOP_K]
    base = jnp.sum(jnp.where(expert[:, :, None] == jnp.arange(N_EXPERTS)[None, None, :],
                             g_start[None, None, :], 0), axis=-1)
    pos = (base + route[:, TOP_K:2 * TOP_K]).astype(jnp.int32).reshape(t // TM_FFN, 1, TOP_K * TM_FFN)
    pad_len = (padded - cnt).at[N_EXPERTS - 1].add(n_tiles * TMG - g_end[-1])
    xs = _dispatch(hn, pos, (g_start + cnt).astype(jnp.int32), pad_len.astype(jnp.int32), n_tiles * TMG)
    ys = _ffn_grouped(xs, tile_expert, n_valid.reshape(1).astype(jnp.int32), wg, wu, wd)
    return _combine(ys, pos, gate, h1, gfin)


def _mix_consts(w_spatial, b_spatial, dec_seq):
    tril = jnp.tril(jnp.ones((CHUNK, CHUNK), F32))
    w_p = w_spatial * tril
    reps = CHUNK // dec_seq
    w_s = jnp.stack([jnp.kron(jnp.eye(reps, dtype=F32), w_p[g, :dec_seq, :dec_seq]) for g in range(G_A)])
    wmix = jnp.stack([w_p, w_s]).astype(BF16)
    b_p = jnp.repeat(b_spatial.T, CH_A, axis=1)
    b_s = jnp.tile(b_p[:dec_seq], (reps, 1))
    return wmix, jnp.stack([b_p, b_s])


def kernel(x_prompt, x_sample, state_hgrn, norm_mix_g, w_in, ln_v_g, ln_v_b, w_spatial, b_spatial,
           lower_bounds, hgrn_norm_g, w_branch_a, w_branch_b, w_out, norm_ffn_g, dense_w_gate,
           dense_w_up, dense_w_down, router_w, moe_w_gate, moe_w_up, moe_w_down, final_norm_g):
    n_seq, seq_len, _ = x_prompt.shape
    dec_batch, dec_seq, _ = x_sample.shape
    t_prompt = n_seq * seq_len
    t_sample = dec_batch * dec_seq
    n_prompt_tiles = t_prompt // TM_PROJ

    h = jnp.concatenate([x_prompt.reshape(t_prompt, D_MODEL), x_sample.reshape(t_sample, D_MODEL)])

    lb_cum = jnp.cumsum(jax.nn.softmax(lower_bounds.astype(F32), axis=0), axis=0)
    lb_all = lb_cum - lb_cum[0:1]
    log_lb, log1m_lb, one_m_lb = jnp.log(lb_all), jnp.log1p(-lb_all), 1.0 - lb_all

    row = lambda p: p.reshape(1, -1)
    states_p, states_s, v_rows = [], [], []
    for l in range(DEPTH):
        wmix, bmix = _mix_consts(w_spatial[l], b_spatial[l], dec_seq)
        a, vln, q, lf, k, iv, gs, sa, sb = _in_proj(
            h, row(norm_mix_g[l]), w_in[l].astype(BF16), row(ln_v_g[l]), row(ln_v_b[l]),
            row(log_lb[l]), row(log1m_lb[l]), row(one_m_lb[l]), wmix, bmix, n_prompt_tiles)
        gn = row(hgrn_norm_g[l])
        o_p, s_p = _hgrn_prompt(q, lf, k, iv, gn, n_seq, seq_len)
        o_s, s_s = _hgrn_sample(q, lf, k, iv, state_hgrn[l].astype(F32), gn, t_prompt, dec_batch, dec_seq)
        h1, hn = _post_mix(a, o_p, o_s, gs, sa, sb, h, w_branch_a[l].astype(BF16),
                           w_branch_b[l].astype(BF16), w_out[l].astype(BF16), row(norm_ffn_g[l]),
                           n_prompt_tiles)
        i = l // 2
        if l % 2 == 0:
            h = _ffn_dense(h1, hn, dense_w_gate[i].astype(BF16), dense_w_up[i].astype(BF16),
                           dense_w_down[i].astype(BF16))
        else:
            rw = jnp.pad(router_w[i], ((0, 0), (0, LANES - N_EXPERTS)))
            rw_hi = rw.astype(BF16)
            rw_lo = (rw - rw_hi.astype(F32)).astype(BF16)
            route, gate, counts = _router(hn, rw_hi, rw_lo)
            h = _moe_routed(h1, hn, route, gate, counts, moe_w_gate[i].astype(BF16),
                            moe_w_up[i].astype(BF16), moe_w_down[i].astype(BF16), row(final_norm_g))
        states_p.append(s_p)
        states_s.append(s_s)
        v_rows.append(vln[t_prompt:].reshape(dec_batch, dec_seq, D_A))

    y_prompt = h[:t_prompt].reshape(n_seq, seq_len, D_MODEL)
    y_sample = h[t_prompt:].reshape(dec_batch, dec_seq, D_MODEL)
    return (y_prompt, y_sample, jnp.stack(states_p).astype(x_prompt.dtype),
            jnp.stack(states_s).astype(state_hgrn.dtype), jnp.stack(v_rows))
```

```python
import functools

import numpy as np
import jax
import jax.numpy as jnp
from jax import lax
from jax.experimental import pallas as pl
from jax.experimental.pallas import tpu as pltpu

F32 = jnp.float32
BF16 = jnp.bfloat16

D_MODEL = 1024
DEPTH = 2
D_A = 512
G_A = 4
CH_A = 128
CHUNK = 128
D_B = 512
H_B = 4
DK = 128
IN_COLS = 5120
D_FF = 2816
N_EXPERTS = 8
TOP_K = 2
EPS = 1e-6

LANES = 128
SUBLANES = 8
VMEM_LIMIT = 52 * 1024 * 1024

TM_PROJ = 256
TM_FFN = 256
TMG = 256
R_HGRN = 256
BLK = 128
LEVEL_HALVES = (64, 32, 16, 8)
SEQ_PER_STEP = 16


def _cparams(sem):
    return pltpu.CompilerParams(dimension_semantics=sem, vmem_limit_bytes=VMEM_LIMIT)


def _dot(a, b):
    return jnp.dot(a, b, preferred_element_type=F32)


def _dot_nt(a, b):
    return lax.dot_general(a, b, (((1,), (1,)), ((), ())), preferred_element_type=F32)


def _split(x, terms):
    out = []
    for _ in range(terms - 1):
        hi = x.astype(BF16)
        out.append(hi)
        x = x - hi.astype(F32)
    out.append(x.astype(BF16))
    return out


def _rms(x, g):
    return x * lax.rsqrt(jnp.mean(x * x, axis=-1, keepdims=True) + EPS) * g


def _gelu(x):
    return 0.5 * x * (1.0 + lax.erf(x * np.float32(2.0 ** -0.5)))


def _sigmoid(x):
    return jax.nn.sigmoid(x)


def _split_rows(n_prompt_tiles):
    return (lambda i: (jnp.minimum(i, n_prompt_tiles - 1), 0),
            lambda i: (jnp.maximum(i - n_prompt_tiles, 0), 0))


def _select_rows(parts, n_prompt_tiles):
    if len(parts) == 1:
        return parts[0][...]
    return jnp.where(pl.program_id(0) < n_prompt_tiles, parts[0][...], parts[1][...])


def _in_proj_body(*refs, n_x, n_prompt_tiles):
    x_parts, refs = refs[:n_x], refs[n_x:]
    (g_ref, w_ref, lng_ref, lnb_ref, la_ref, l1m_ref, oml_ref, wmix_ref, bmix_ref,
     a_ref, vs_ref, q_ref, lf_ref, k_ref, iv_ref, gs_ref, sa_ref, sb_ref) = refs
    tm = a_ref.shape[0]
    xn = _rms(_select_rows(x_parts, n_prompt_tiles), g_ref[...]).astype(BF16)

    def seg(lo, n):
        return _dot(xn, w_ref[:, lo:lo + n])

    u = _gelu(seg(0, D_A))
    v = _gelu(seg(D_A, D_A))
    vc = v - jnp.mean(v, axis=-1, keepdims=True)
    v = vc * lax.rsqrt(jnp.mean(vc * vc, axis=-1, keepdims=True) + EPS) * lng_ref[...] + lnb_ref[...]

    @pl.when(pl.program_id(0) >= n_prompt_tiles)
    def _():
        vs_ref[...] = v

    vb = v.astype(BF16)
    for c in range(tm // CHUNK):
        rows = slice(c * CHUNK, (c + 1) * CHUNK)
        for g in range(G_A):
            cols = slice(g * CH_A, (g + 1) * CH_A)
            mixed = _dot(wmix_ref[0, g], vb[rows, cols]) + bmix_ref[0, :, cols]
            a_ref[rows, cols] = u[rows, cols] * mixed

    zq = seg(2 * D_A, D_B)
    q_ref[...] = zq * _sigmoid(zq)
    fz = seg(2 * D_A + D_B, D_B)
    log_sig = jnp.minimum(fz, 0.0) - jnp.log1p(jnp.exp(-jnp.abs(fz)))
    b = l1m_ref[...] + log_sig
    la = la_ref[...]
    lf_ref[...] = jnp.maximum(la, b) + jnp.log1p(jnp.exp(-jnp.abs(la - b)))
    k_ref[...] = oml_ref[...] * _sigmoid(-fz)
    iv_ref[...] = seg(2 * D_A + 2 * D_B, D_B)
    gz = seg(2 * D_A + 3 * D_B, D_B)
    gs_ref[...] = gz * _sigmoid(gz)
    sa_ref[...] = _sigmoid(seg(2 * D_A + 4 * D_B, D_MODEL))
    sb_ref[...] = _sigmoid(seg(2 * D_A + 4 * D_B + D_MODEL, D_MODEL))


def _in_proj(x_parts, g, w, lng, lnb, la, l1m, oml, wmix, bmix, t, n_prompt_tiles):
    tm = TM_PROJ
    row = lambda n: pl.BlockSpec((1, n), lambda i: (0, 0))
    tile = lambda n: pl.BlockSpec((tm, n), lambda i: (i, 0))
    sel = lambda i: jnp.minimum(i // n_prompt_tiles, 1)
    p_map, s_map = _split_rows(n_prompt_tiles)
    if len(x_parts) == 1:
        x_specs = [tile(D_MODEL)]
    else:
        x_specs = [pl.BlockSpec((tm, D_MODEL), p_map), pl.BlockSpec((tm, D_MODEL), s_map)]
    t_sample = t - n_prompt_tiles * tm
    shape = lambda rows, n: jax.ShapeDtypeStruct((rows, n), F32)
    out_shapes = ([shape(t, D_A), shape(t_sample, D_A)] + [shape(t, D_A)] * 5 + [shape(t, D_MODEL)] * 2)
    out_specs = ([tile(D_A), pl.BlockSpec((tm, D_A), s_map)] + [tile(D_A)] * 5 + [tile(D_MODEL)] * 2)
    return pl.pallas_call(
        functools.partial(_in_proj_body, n_x=len(x_parts), n_prompt_tiles=n_prompt_tiles),
        grid=(t // tm,),
        in_specs=x_specs + [row(D_MODEL),
                            pl.BlockSpec((D_MODEL, IN_COLS), lambda i: (0, 0)),
                            row(D_A), row(D_A), row(D_B), row(D_B), row(D_B),
                            pl.BlockSpec((1, G_A, CHUNK, CHUNK), lambda i: (sel(i), 0, 0, 0)),
                            pl.BlockSpec((1, CHUNK, D_A), lambda i: (sel(i), 0, 0))],
        out_specs=out_specs,
        out_shape=out_shapes,
        compiler_params=_cparams(("arbitrary",)),
        name="in_proj",
    )(*x_parts, g, w, lng, lnb, la, l1m, oml, wmix, bmix)


def _hgrn_block(q_ref, k_ref, v_ref, g_scr, masks_ref, st, r0, cols, lane, row):
    rows = slice(r0, r0 + BLK)
    qh, kh, vh = q_ref[rows, cols], k_ref[rows, cols], v_ref[rows, cols]
    gh = g_scr[rows, cols]
    g_tot = g_scr[r0 + BLK - 1:r0 + BLK, cols]

    o = _dot_nt((qh * jnp.exp(gh)).astype(BF16), st.astype(BF16))

    sc = None
    for l, m in enumerate(LEVEL_HALVES):
        refs = [jnp.broadcast_to(g_scr[r0 + b + m - 1:r0 + b + m, cols], (2 * m, DK))
                for b in range(0, BLK, 2 * m)]
        g_ref_rows = refs[0] if len(refs) == 1 else jnp.concatenate(refs, axis=0)
        right = (row & m) != 0
        x = jnp.where(right, gh - g_ref_rows, g_ref_rows - gh)
        z = (jnp.where(right, qh, kh) * jnp.exp(x)).astype(BF16)
        part = _dot_nt(z, z) * masks_ref[l]
        sc = part if sc is None else sc + part

    strips = []
    for g0 in range(0, BLK, SUBLANES):
        qg, gg = qh[g0:g0 + SUBLANES], gh[g0:g0 + SUBLANES]
        strip = jnp.zeros((SUBLANES, BLK), F32)
        for s in range(SUBLANES):
            src = r0 + g0 + s
            p = qg * jnp.exp(gg - g_scr[src:src + 1, cols]) * k_ref[src:src + 1, cols]
            strip = jnp.where(lane[:SUBLANES] == g0 + s, jnp.sum(p, axis=-1, keepdims=True), strip)
        strips.append(strip)
    diag = jnp.concatenate(strips, axis=0)
    sc = sc + jnp.where(lane <= row, diag, 0.0)

    o = o + _dot(sc.astype(BF16), vh.astype(BF16))
    kt = (kh * jnp.exp(g_tot - gh)).astype(BF16)
    st_new = st * jnp.exp(g_tot) + _dot(vh.T.astype(BF16), kt)
    return o, st_new


def _hgrn_prompt_body(q_ref, lf_ref, k_ref, v_ref, tri_ref, masks_ref, gn_ref, o_ref, sfin_ref,
                      st_ref, g_scr):
    j = pl.program_id(1)
    r = q_ref.shape[0]

    @pl.when(j == 0)
    def _():
        st_ref[...] = jnp.zeros_like(st_ref)

    tri = tri_ref[...]
    g_scr[...] = sum(_dot(tri, part) for part in _split(lf_ref[...], 2))
    lane = lax.broadcasted_iota(jnp.int32, (BLK, BLK), 1)
    row = lax.broadcasted_iota(jnp.int32, (BLK, BLK), 0)

    for h in range(H_B):
        cols = slice(h * DK, (h + 1) * DK)
        st = st_ref[h]
        for r0 in range(0, r, BLK):
            o, st = _hgrn_block(q_ref, k_ref, v_ref, g_scr, masks_ref, st, r0, cols, lane, row)
            o_ref[r0:r0 + BLK, cols] = _rms(o, gn_ref[:, cols])
        st_ref[h] = st

    @pl.when(j == pl.num_programs(1) - 1)
    def _():
        for h in range(H_B):
            sfin_ref[0, h] = st_ref[h].T


def _hgrn_consts():
    i = np.arange(R_HGRN)[:, None]
    j = np.arange(R_HGRN)[None, :]
    tri = ((i // BLK) == (j // BLK)) & (j <= i)
    i, j = np.arange(BLK)[:, None], np.arange(BLK)[None, :]
    masks = [((i // (2 * m)) == (j // (2 * m))) & ((i & m) != 0) & ((j & m) == 0) for m in LEVEL_HALVES]
    return (jnp.asarray(tri.astype(np.float32), dtype=BF16),
            jnp.asarray(np.stack(masks).astype(np.float32)))


def _hgrn_prompt(q, lf, k, iv, gn, n_seq, seq_len):
    r = R_HGRN
    nblk = seq_len // r
    tri, masks = _hgrn_consts()
    blk = pl.BlockSpec((r, D_B), lambda b, j: (b * nblk + j, 0))
    return pl.pallas_call(
        _hgrn_prompt_body,
        grid=(n_seq, nblk),
        in_specs=[blk, blk, blk, blk,
                  pl.BlockSpec(tri.shape, lambda b, j: (0, 0)),
                  pl.BlockSpec(masks.shape, lambda b, j: (0, 0, 0)),
                  pl.BlockSpec((1, D_B), lambda b, j: (0, 0))],
        out_specs=[blk, pl.BlockSpec((1, H_B, DK, DK), lambda b, j: (b, 0, 0, 0))],
        out_shape=[jax.ShapeDtypeStruct((n_seq * seq_len, D_B), F32),
                   jax.ShapeDtypeStruct((n_seq, H_B, DK, DK), F32)],
        scratch_shapes=[pltpu.VMEM((H_B, DK, DK), F32),
                        pltpu.VMEM((r, D_B), F32)],
        compiler_params=_cparams(("arbitrary", "arbitrary")),
        name="hgrn_prompt",
    )(q, lf, k, iv, tri, masks, gn)


def _sub_chunk_exact(q_ref, k_ref, v_ref, g_ref, r0, cols, sub):
    qg = q_ref[pl.ds(r0, sub), cols]
    gg = g_ref[pl.ds(r0, sub), cols]
    kg = k_ref[pl.ds(r0, sub), cols]
    vg = v_ref[pl.ds(r0, sub), cols]
    row = lax.broadcasted_iota(jnp.int32, (sub, 1), 0)
    acc = jnp.zeros((sub, DK), F32)
    for s in range(sub):
        kb = kg[s:s + 1, :]
        gb = gg[s:s + 1, :]
        vb = vg[s:s + 1, :]
        p = qg * jnp.exp(gg - gb) * kb
        rs = jnp.sum(p, axis=-1, keepdims=True)
        acc = acc + jnp.where(row >= s, rs, 0.0) * vb
    return acc


def _hgrn_sample_body(q_ref, lf_ref, k_ref, v_ref, s0_ref, cm_ref, gn_ref, o_ref, s1_ref,
                      g_scr, o_scr, *, dec_seq):
    rows = q_ref.shape[0]
    n_seq = rows // dec_seq
    parts = _split(lf_ref[...], 3)

    def cum(idx):
        m = cm_ref[idx]
        return _dot(m, parts[0]) + _dot(m, parts[1]) + _dot(m, parts[2])

    g_all = cum(0)
    g_scr[...] = g_all
    g_rev = cum(1)
    g_tot = cum(2)
    lane = lax.broadcasted_iota(jnp.int32, (DK, rows), 1)

    for h in range(H_B):
        cols = slice(h * DK, (h + 1) * DK)
        qt = (q_ref[:, cols] * jnp.exp(g_all[:, cols])).astype(BF16)
        kt_t = (k_ref[:, cols] * jnp.exp(g_rev[:, cols])).T
        dec_t = jnp.exp(g_tot[:, cols]).T
        vb = v_ref[:, cols].astype(BF16)
        for n in range(n_seq):
            r0 = n * dec_seq
            s0 = s0_ref[0, n, h]
            o_scr[r0:r0 + dec_seq, cols] = _dot(qt[r0:r0 + dec_seq, :], s0.astype(BF16))
            in_seq = (lane >= r0) & (lane < r0 + dec_seq)
            u = _dot(jnp.where(in_seq, kt_t, 0.0).astype(BF16), vb)
            s1_ref[n, h] = s0 * dec_t[:, r0:r0 + 1] + u

    def group(n, carry):
        r0 = pl.multiple_of(n * dec_seq, dec_seq)
        for h in range(H_B):
            cols = slice(h * DK, (h + 1) * DK)
            o_scr[pl.ds(r0, dec_seq), cols] += _sub_chunk_exact(q_ref, k_ref, v_ref, g_scr, r0, cols,
                                                                 dec_seq)
        return carry

    lax.fori_loop(0, n_seq, group, 0)

    for h in range(H_B):
        cols = slice(h * DK, (h + 1) * DK)
        o_ref[:, cols] = _rms(o_scr[:, cols], gn_ref[:, cols])


def _hgrn_sample(q, lf, k, iv, state_in, layer, gn, row0, n_seq, dec_seq):
    rows = SEQ_PER_STEP * dec_seq
    i = np.arange(rows)[:, None]
    j = np.arange(rows)[None, :]
    same = (i // dec_seq) == (j // dec_seq)
    cm = jnp.asarray(np.stack([same & (j <= i), same & (j > i), same]).astype(np.float32), dtype=BF16)
    blk0 = row0 // rows
    blk = pl.BlockSpec((rows, D_B), lambda n: (blk0 + n, 0))
    oblk = pl.BlockSpec((rows, D_B), lambda n: (n, 0))
    return pl.pallas_call(
        functools.partial(_hgrn_sample_body, dec_seq=dec_seq),
        grid=(n_seq // SEQ_PER_STEP,),
        in_specs=[blk, blk, blk, blk,
                  pl.BlockSpec((1, SEQ_PER_STEP, H_B, DK, DK), lambda n: (layer, n, 0, 0, 0)),
                  pl.BlockSpec(cm.shape, lambda n: (0, 0, 0)),
                  pl.BlockSpec((1, D_B), lambda n: (0, 0))],
        out_specs=[oblk, pl.BlockSpec((SEQ_PER_STEP, H_B, DK, DK), lambda n: (n, 0, 0, 0))],
        out_shape=[jax.ShapeDtypeStruct((n_seq * dec_seq, D_B), F32),
                   jax.ShapeDtypeStruct((n_seq, H_B, DK, DK), F32)],
        scratch_shapes=[pltpu.VMEM((rows, D_B), F32), pltpu.VMEM((rows, D_B), F32)],
        compiler_params=_cparams(("arbitrary",)),
        name="hgrn_sample",
    )(q, lf, k, iv, state_in, cm, gn)


def _post_mix_body(*refs, n_h, n_prompt_tiles):
    h_parts, refs = refs[:n_h], refs[n_h:]
    (a_ref, op_ref, os_ref, gs_ref, sa_ref, sb_ref, wa_ref, wb_ref, wo_ref, gf_ref,
     h1_ref, hn_ref) = refs
    o = _select_rows((op_ref, os_ref), n_prompt_tiles)
    pa = _dot(a_ref[...].astype(BF16), wa_ref[...])
    pb = _dot((o * gs_ref[...]).astype(BF16), wb_ref[...])
    merged = sa_ref[...] * pa + sb_ref[...] * pb
    h1 = _select_rows(h_parts, n_prompt_tiles) + _dot(merged.astype(BF16), wo_ref[...])
    h1_ref[...] = h1
    hn_ref[...] = _rms(h1, gf_ref[...])


def _post_mix(h_parts, a, o_p, o_s, gs, sa, sb, wa, wb, wo, gf, n_prompt_tiles):
    t = a.shape[0]
    tm = TM_PROJ
    tile = lambda n: pl.BlockSpec((tm, n), lambda i: (i, 0))
    full = lambda s: pl.BlockSpec(s, lambda i: (0, 0))
    p_map, s_map = _split_rows(n_prompt_tiles)
    if len(h_parts) == 1:
        h_specs = [tile(D_MODEL)]
    else:
        h_specs = [pl.BlockSpec((tm, D_MODEL), p_map), pl.BlockSpec((tm, D_MODEL), s_map)]
    return pl.pallas_call(
        functools.partial(_post_mix_body, n_h=len(h_parts), n_prompt_tiles=n_prompt_tiles),
        grid=(t // tm,),
        in_specs=h_specs + [tile(D_A),
                            pl.BlockSpec((tm, D_B), p_map), pl.BlockSpec((tm, D_B), s_map),
                            tile(D_B), tile(D_MODEL), tile(D_MODEL),
                            full((D_A, D_MODEL)), full((D_B, D_MODEL)), full((D_MODEL, D_MODEL)),
                            full((1, D_MODEL))],
        out_specs=[tile(D_MODEL), tile(D_MODEL)],
        out_shape=[jax.ShapeDtypeStruct((t, D_MODEL), F32)] * 2,
        compiler_params=_cparams(("arbitrary",)),
        name="post_mix",
    )(*h_parts, a, o_p, o_s, gs, sa, sb, wa, wb, wo, gf)


FF_SPLITS = ((0, 1536), (1536, 1280))


def _swiglu(hb, wg_ref, wu_ref, wd_ref, idx):
    acc = None
    for lo, n in FF_SPLITS:
        g = _dot(hb, wg_ref[idx + (slice(None), slice(lo, lo + n))])
        u = _dot(hb, wu_ref[idx + (slice(None), slice(lo, lo + n))])
        act = (g * _sigmoid(g) * u).astype(BF16)
        part = _dot(act, wd_ref[idx + (slice(lo, lo + n), slice(None))])
        acc = part if acc is None else acc + part
    return acc


def _ffn_body(h1_ref, hn_ref, wg_ref, wu_ref, wd_ref, out_ref):
    hb = hn_ref[...].astype(BF16)
    out_ref[...] = h1_ref[...] + _swiglu(hb, wg_ref, wu_ref, wd_ref, ())


def _ffn_dense(h1, hn, wg, wu, wd):
    t = h1.shape[0]
    tm = TM_FFN
    tile = pl.BlockSpec((tm, D_MODEL), lambda i: (i, 0))
    return pl.pallas_call(
        _ffn_body,
        grid=(t // tm,),
        in_specs=[tile, tile,
                  pl.BlockSpec((D_MODEL, D_FF), lambda i: (0, 0)),
                  pl.BlockSpec((D_MODEL, D_FF), lambda i: (0, 0)),
                  pl.BlockSpec((D_FF, D_MODEL), lambda i: (0, 0))],
        out_specs=tile,
        out_shape=jax.ShapeDtypeStruct((t, D_MODEL), F32),
        compiler_params=_cparams(("arbitrary",)),
        name="ffn_dense",
    )(h1, hn, wg, wu, wd)


def _router_body(hn_ref, wh_ref, wl_ref, route_ref, gate_ref, cnt_ref, carry_ref):
    @pl.when(pl.program_id(0) == 0)
    def _():
        carry_ref[...] = jnp.zeros_like(carry_ref)

    tm = hn_ref.shape[0]
    hn = hn_ref[...]
    hh = hn.astype(BF16)
    hl = (hn - hh.astype(F32)).astype(BF16)
    logits = _dot(hh, wh_ref[...]) + _dot(hh, wl_ref[...]) + _dot(hl, wh_ref[...])
    lane = lax.broadcasted_iota(jnp.int32, logits.shape, 1).astype(F32)
    neg = np.float32(-np.inf)
    logits = jnp.where(lane < N_EXPERTS, logits, neg)
    m1 = jnp.max(logits, axis=-1, keepdims=True)
    i1 = jnp.min(jnp.where(logits == m1, lane, float(LANES)), axis=-1, keepdims=True)
    rest = jnp.where(lane == i1, neg, logits)
    m2 = jnp.max(rest, axis=-1, keepdims=True)
    i2 = jnp.min(jnp.where(rest == m2, lane, float(LANES)), axis=-1, keepdims=True)
    e2 = jnp.exp(m2 - m1)
    den = 1.0 + e2
    gate_ref[...] = jnp.where(lane == 0.0, 1.0 / den, jnp.where(lane == 1.0, e2 / den, 0.0))

    sel = jnp.where((lane == i1) | (lane == i2), 1.0, 0.0)
    r = lax.broadcasted_iota(jnp.int32, (tm, tm), 0)
    c = lax.broadcasted_iota(jnp.int32, (tm, tm), 1)
    before = jnp.where(c < r, 1.0, 0.0).astype(BF16)
    rank = _dot(before, sel.astype(BF16)) + carry_ref[...]
    rank1 = jnp.sum(jnp.where(lane == i1, rank, 0.0), axis=-1, keepdims=True)
    rank2 = jnp.sum(jnp.where(lane == i2, rank, 0.0), axis=-1, keepdims=True)
    route = jnp.where(lane == 0.0, i1, jnp.where(lane == 1.0, i2,
                      jnp.where(lane == 2.0, rank1, jnp.where(lane == 3.0, rank2, 0.0))))
    route_ref[...] = route.astype(jnp.int32)
    carry_ref[...] += jnp.sum(sel, axis=0, keepdims=True)
    cnt_ref[...] = jnp.broadcast_to(carry_ref[...], cnt_ref.shape).astype(jnp.int32)


def _router(hn, wh, wl):
    t = hn.shape[0]
    tm = TM_FFN
    return pl.pallas_call(
        _router_body,
        grid=(t // tm,),
        in_specs=[pl.BlockSpec((tm, D_MODEL), lambda i: (i, 0)),
                  pl.BlockSpec((D_MODEL, LANES), lambda i: (0, 0)),
                  pl.BlockSpec((D_MODEL, LANES), lambda i: (0, 0))],
        out_specs=[pl.BlockSpec((tm, LANES), lambda i: (i, 0)),
                   pl.BlockSpec((tm, LANES), lambda i: (i, 0)),
                   pl.BlockSpec((SUBLANES, LANES), lambda i: (0, 0))],
        out_shape=[jax.ShapeDtypeStruct((t, LANES), jnp.int32),
                   jax.ShapeDtypeStruct((t, LANES), F32),
                   jax.ShapeDtypeStruct((SUBLANES, LANES), jnp.int32)],
        scratch_shapes=[pltpu.VMEM((1, LANES), F32)],
        compiler_params=_cparams(("arbitrary",)),
        name="router",
    )(hn, wh, wl)


def _row_copy(src_ref, src_row, dst_ref, dst_row, sem):
    return pltpu.make_async_copy(src_ref.at[pl.ds(src_row, 1)], dst_ref.at[pl.ds(dst_row, 1)], sem)


def _dispatch_body(pad_start_ref, pad_len_ref, pos_ref, hn_ref, xs_ref, zero_ref, sem, zsem):
    tm = hn_ref.shape[0]

    @pl.when(pl.program_id(0) == 0)
    def _():
        zero_ref[...] = jnp.zeros_like(zero_ref)
        for e in range(N_EXPERTS):
            s = pad_start_ref[e]

            def fill(r, carry):
                _row_copy(zero_ref, 0, xs_ref, s + r, zsem).start()
                return carry

            def fill_wait(r, carry):
                _row_copy(zero_ref, 0, xs_ref, 0, zsem).wait()
                return carry

            lax.fori_loop(0, pad_len_ref[e], fill, 0)
            lax.fori_loop(0, pad_len_ref[e], fill_wait, 0)

    def issue(t, carry):
        for k in range(TOP_K):
            _row_copy(hn_ref, t, xs_ref, pos_ref[0, 0, TOP_K * t + k], sem).start(priority=k)
        return carry

    def drain(t, carry):
        for k in range(TOP_K):
            _row_copy(hn_ref, 0, xs_ref, 0, sem).wait()
        return carry

    lax.fori_loop(0, tm, issue, 0, unroll=8)
    lax.fori_loop(0, tm, drain, 0, unroll=8)


def _dispatch(hn, pos, pad_start, pad_len, n_rows):
    t = hn.shape[0]
    tm = TM_FFN
    grid_spec = pltpu.PrefetchScalarGridSpec(
        num_scalar_prefetch=2,
        grid=(t // tm,),
        in_specs=[pl.BlockSpec((1, 1, TOP_K * tm), lambda i, ps, pn: (i, 0, 0), memory_space=pltpu.SMEM),
                  pl.BlockSpec((tm, D_MODEL), lambda i, ps, pn: (i, 0))],
        out_specs=pl.BlockSpec(memory_space=pl.ANY),
        scratch_shapes=[pltpu.VMEM((SUBLANES, D_MODEL), F32),
                        pltpu.SemaphoreType.DMA, pltpu.SemaphoreType.DMA],
    )
    return pl.pallas_call(
        _dispatch_body,
        grid_spec=grid_spec,
        out_shape=jax.ShapeDtypeStruct((n_rows, D_MODEL), F32),
        compiler_params=_cparams(("arbitrary",)),
        name="moe_dispatch",
    )(pad_start, pad_len, pos, hn)


def _ffn_grouped_body(te_ref, nv_ref, xs_ref, wg_ref, wu_ref, wd_ref, ys_ref):
    @pl.when(pl.program_id(0) < nv_ref[0])
    def _():
        ys_ref[...] = _swiglu(xs_ref[...].astype(BF16), wg_ref, wu_ref, wd_ref, (0,))

    @pl.when(pl.program_id(0) >= nv_ref[0])
    def _():
        ys_ref[...] = jnp.zeros_like(ys_ref)


def _ffn_grouped(xs, tile_expert, n_valid, wg, wu, wd):
    n_rows = xs.shape[0]
    row_tile = lambda j, te, nv: (jnp.minimum(j, nv[0] - 1), 0)
    expert = lambda j, te, nv: (te[j], 0, 0)
    grid_spec = pltpu.PrefetchScalarGridSpec(
        num_scalar_prefetch=2,
        grid=(n_rows // TMG,),
        in_specs=[pl.BlockSpec((TMG, D_MODEL), row_tile),
                  pl.BlockSpec((1, D_MODEL, D_FF), expert),
                  pl.BlockSpec((1, D_MODEL, D_FF), expert),
                  pl.BlockSpec((1, D_FF, D_MODEL), expert)],
        out_specs=pl.BlockSpec((TMG, D_MODEL), lambda j, te, nv: (j, 0)),
    )
    return pl.pallas_call(
        _ffn_grouped_body,
        grid_spec=grid_spec,
        out_shape=jax.ShapeDtypeStruct((n_rows, D_MODEL), F32),
        compiler_params=_cparams(("arbitrary",)),
        name="moe_ffn",
    )(tile_expert, n_valid, xs, wg, wu, wd)


def _combine_body(pos_ref, gate_ref, h1_ref, gfin_ref, ys_ref, yp_ref, ysm_ref, y_scr, sem,
                  *, n_prompt_tiles):
    i = pl.program_id(0)
    tm = h1_ref.shape[0]

    def issue(t, carry):
        for k in range(TOP_K):
            _row_copy(ys_ref, pos_ref[0, 0, TOP_K * t + k], y_scr.at[k], t, sem).start(priority=k)
        return carry

    def drain(t, carry):
        for k in range(TOP_K):
            _row_copy(ys_ref, 0, y_scr.at[k], 0, sem).wait()
        return carry

    lax.fori_loop(0, tm, issue, 0, unroll=8)
    lax.fori_loop(0, tm, drain, 0, unroll=8)
    gate = gate_ref[...]
    val = h1_ref[...] + gate[:, 0:1] * y_scr[0] + gate[:, 1:2] * y_scr[1]
    y = _rms(val, gfin_ref[...])

    @pl.when(i < n_prompt_tiles)
    def _():
        yp_ref[...] = y

    @pl.when(i >= n_prompt_tiles)
    def _():
        ysm_ref[...] = y


def _combine(ys, pos, gate, h1, gfin, n_prompt_tiles):
    t = h1.shape[0]
    tm = TM_FFN
    p_map, s_map = _split_rows(n_prompt_tiles)
    t_prompt = n_prompt_tiles * tm
    return pl.pallas_call(
        functools.partial(_combine_body, n_prompt_tiles=n_prompt_tiles),
        grid=(t // tm,),
        in_specs=[pl.BlockSpec((1, 1, TOP_K * tm), lambda i: (i, 0, 0), memory_space=pltpu.SMEM),
                  pl.BlockSpec((tm, LANES), lambda i: (i, 0)),
                  pl.BlockSpec((tm, D_MODEL), lambda i: (i, 0)),
                  pl.BlockSpec((1, D_MODEL), lambda i: (0, 0)),
                  pl.BlockSpec(memory_space=pl.ANY)],
        out_specs=[pl.BlockSpec((tm, D_MODEL), p_map), pl.BlockSpec((tm, D_MODEL), s_map)],
        out_shape=[jax.ShapeDtypeStruct((t_prompt, D_MODEL), F32),
                   jax.ShapeDtypeStruct((t - t_prompt, D_MODEL), F32)],
        scratch_shapes=[pltpu.VMEM((TOP_K, tm, D_MODEL), F32), pltpu.SemaphoreType.DMA],
        compiler_params=_cparams(("arbitrary",)),
        name="moe_combine",
    )(pos, gate, h1, gfin, ys)


def _moe_routed(h1, hn, route, gate, counts, wg, wu, wd, gfin, n_prompt_tiles):
    t = h1.shape[0]
    n_tiles = (TOP_K * t) // TMG + N_EXPERTS
    cnt = counts[0, :N_EXPERTS]
    padded = ((cnt + TMG - 1) // TMG) * TMG
    g_end = jnp.cumsum(padded)
    g_start = g_end - padded
    n_valid = g_end[-1] // TMG
    tile_row = jnp.minimum(jnp.arange(n_tiles, dtype=jnp.int32), n_valid - 1) * TMG
    tile_expert = jnp.sum(tile_row[:, None] >= g_end[None, :], axis=1).astype(jnp.int32)
    expert = route[:, 0:TOP_K]
    base = jnp.sum(jnp.where(expert[:, :, None] == jnp.arange(N_EXPERTS)[None, None, :],
                             g_start[None, None, :], 0), axis=-1)
    pos = (base + route[:, TOP_K:2 * TOP_K]).astype(jnp.int32).reshape(t // TM_FFN, 1, TOP_K * TM_FFN)
    pad_len = (padded - cnt).at[N_EXPERTS - 1].add(n_tiles * TMG - g_end[-1])
    xs = _dispatch(hn, pos, (g_start + cnt).astype(jnp.int32), pad_len.astype(jnp.int32), n_tiles * TMG)
    ys = _ffn_grouped(xs, tile_expert, n_valid.reshape(1).astype(jnp.int32), wg, wu, wd)
    return _combine(ys, pos, gate, h1, gfin, n_prompt_tiles)


def _mix_consts(w_spatial, b_spatial, dec_seq):
    tril = jnp.tril(jnp.ones((CHUNK, CHUNK), F32))
    w_p = w_spatial * tril
    reps = CHUNK // dec_seq
    w_s = jnp.stack([jnp.kron(jnp.eye(reps, dtype=F32), w_p[g, :dec_seq, :dec_seq]) for g in range(G_A)])
    wmix = jnp.stack([w_p, w_s]).astype(BF16)
    b_p = jnp.repeat(b_spatial.T, CH_A, axis=1)
    b_s = jnp.tile(b_p[:dec_seq], (reps, 1))
    return wmix, jnp.stack([b_p, b_s])


def kernel(x_prompt, x_sample, state_hgrn, norm_mix_g, w_in, ln_v_g, ln_v_b, w_spatial, b_spatial,
           lower_bounds, hgrn_norm_g, w_branch_a, w_branch_b, w_out, norm_ffn_g, dense_w_gate,
           dense_w_up, dense_w_down, router_w, moe_w_gate, moe_w_up, moe_w_down, final_norm_g):
    n_seq, seq_len, _ = x_prompt.shape
    dec_batch, dec_seq, _ = x_sample.shape
    t_prompt = n_seq * seq_len
    t_sample = dec_batch * dec_seq
    t = t_prompt + t_sample
    n_prompt_tiles = t_prompt // TM_PROJ

    lb_cum = jnp.cumsum(jax.nn.softmax(lower_bounds.astype(F32), axis=0), axis=0)
    lb_all = lb_cum - lb_cum[0:1]
    log_lb, log1m_lb, one_m_lb = jnp.log(lb_all), jnp.log1p(-lb_all), 1.0 - lb_all

    row = lambda p: p.reshape(1, -1)
    h_parts = (x_prompt.reshape(t_prompt, D_MODEL), x_sample.reshape(t_sample, D_MODEL))
    state_in = state_hgrn.astype(F32)
    states_p, states_s, v_rows = [], [], []
    for l in range(DEPTH):
        wmix, bmix = _mix_consts(w_spatial[l], b_spatial[l], dec_seq)
        a, v_s, q, lf, k, iv, gs, sa, sb = _in_proj(
            h_parts, row(norm_mix_g[l]), w_in[l].astype(BF16), row(ln_v_g[l]), row(ln_v_b[l]),
            row(log_lb[l]), row(log1m_lb[l]), row(one_m_lb[l]), wmix, bmix, t, n_prompt_tiles)
        gn = row(hgrn_norm_g[l])
        o_p, s_p = _hgrn_prompt(q, lf, k, iv, gn, n_seq, seq_len)
        o_s, s_s = _hgrn_sample(q, lf, k, iv, state_in, l, gn, t_prompt, dec_batch, dec_seq)
        h1, hn = _post_mix(h_parts, a, o_p, o_s, gs, sa, sb, w_branch_a[l].astype(BF16),
                           w_branch_b[l].astype(BF16), w_out[l].astype(BF16), row(norm_ffn_g[l]),
                           n_prompt_tiles)
        i = l // 2
        if l % 2 == 0:
            h_parts = (_ffn_dense(h1, hn, dense_w_gate[i].astype(BF16), dense_w_up[i].astype(BF16),
                                  dense_w_down[i].astype(BF16)),)
        else:
            rw = jnp.pad(router_w[i], ((0, 0), (0, LANES - N_EXPERTS)))
            rw_hi = rw.astype(BF16)
            rw_lo = (rw - rw_hi.astype(F32)).astype(BF16)
            route, gate, counts = _router(hn, rw_hi, rw_lo)
            h_parts = _moe_routed(h1, hn, route, gate, counts, moe_w_gate[i].astype(BF16),
                                  moe_w_up[i].astype(BF16), moe_w_down[i].astype(BF16),
                                  row(final_norm_g), n_prompt_tiles)
        states_p.append(s_p)
        states_s.append(s_s)
        v_rows.append(v_s.reshape(dec_batch, dec_seq, D_A))

    y_prompt = h_parts[0].reshape(n_seq, seq_len, D_MODEL)
    y_sample = h_parts[1].reshape(dec_batch, dec_seq, D_MODEL)
    return (y_prompt, y_sample, jnp.stack(states_p).astype(x_prompt.dtype),
            jnp.stack(states_s).astype(state_hgrn.dtype), jnp.stack(v_rows))
```

```python
import functools

import numpy as np
import jax
import jax.numpy as jnp
from jax import lax
from jax.experimental import pallas as pl
from jax.experimental.pallas import tpu as pltpu

F32 = jnp.float32
BF16 = jnp.bfloat16

D_MODEL = 1024
DEPTH = 2
D_A = 512
G_A = 4
CH_A = 128
CHUNK = 128
D_B = 512
H_B = 4
DK = 128
IN_COLS = 5120
D_FF = 2816
N_EXPERTS = 8
TOP_K = 2
EPS = 1e-6

LANES = 128
SUBLANES = 8
VMEM_LIMIT = 52 * 1024 * 1024

TM_PROJ = 256
TM_MOE = 512
TMG = 256
CH = 128
N_SLOTS = -(-(TOP_K * TM_MOE + N_EXPERTS * (CH - 1)) // CH)
SLOTS_PER_DOT = 4
KEY_MUL = 4096
CJ_NONE = N_SLOTS
ZERO_BITS = (128, 64, 32, 16, 8)
R_HGRN = 256
BLK = 128
LEVEL_HALVES = (64, 32, 16, 8)
SEQ_PER_STEP = 16


def _cparams(sem):
    return pltpu.CompilerParams(dimension_semantics=sem, vmem_limit_bytes=VMEM_LIMIT)


def _dot(a, b):
    return jnp.dot(a, b, preferred_element_type=F32)


def _dot_nt(a, b):
    return lax.dot_general(a, b, (((1,), (1,)), ((), ())), preferred_element_type=F32)


def _split(x, terms):
    out = []
    for _ in range(terms - 1):
        hi = x.astype(BF16)
        out.append(hi)
        x = x - hi.astype(F32)
    out.append(x.astype(BF16))
    return out


def _rms(x, g):
    return x * lax.rsqrt(jnp.mean(x * x, axis=-1, keepdims=True) + EPS) * g


def _gelu(x):
    return 0.5 * x * (1.0 + lax.erf(x * np.float32(2.0 ** -0.5)))


def _sigmoid(x):
    return jax.nn.sigmoid(x)


def _split_rows(n_prompt_tiles):
    return (lambda i: (jnp.minimum(i, n_prompt_tiles - 1), 0),
            lambda i: (jnp.maximum(i - n_prompt_tiles, 0), 0))


def _select_rows(parts, n_prompt_tiles):
    if len(parts) == 1:
        return parts[0][...]
    return jnp.where(pl.program_id(0) < n_prompt_tiles, parts[0][...], parts[1][...])


def _in_proj_body(*refs, n_x, n_prompt_tiles):
    x_parts, refs = refs[:n_x], refs[n_x:]
    (g_ref, w_ref, lng_ref, lnb_ref, la_ref, l1m_ref, oml_ref, wmix_ref, bmix_ref,
     a_ref, vs_ref, q_ref, lf_ref, k_ref, iv_ref, gs_ref, sa_ref, sb_ref) = refs
    tm = a_ref.shape[0]
    xn = _rms(_select_rows(x_parts, n_prompt_tiles), g_ref[...]).astype(BF16)

    def seg(lo, n):
        return _dot(xn, w_ref[:, lo:lo + n])

    u = _gelu(seg(0, D_A))
    v = _gelu(seg(D_A, D_A))
    vc = v - jnp.mean(v, axis=-1, keepdims=True)
    v = vc * lax.rsqrt(jnp.mean(vc * vc, axis=-1, keepdims=True) + EPS) * lng_ref[...] + lnb_ref[...]

    @pl.when(pl.program_id(0) >= n_prompt_tiles)
    def _():
        vs_ref[...] = v

    vb = v.astype(BF16)
    for c in range(tm // CHUNK):
        rows = slice(c * CHUNK, (c + 1) * CHUNK)
        for g in range(G_A):
            cols = slice(g * CH_A, (g + 1) * CH_A)
            mixed = _dot(wmix_ref[0, g], vb[rows, cols]) + bmix_ref[0, :, cols]
            a_ref[rows, cols] = (u[rows, cols] * mixed).astype(BF16)

    zq = seg(2 * D_A, D_B)
    q_ref[...] = zq * _sigmoid(zq)
    fz = seg(2 * D_A + D_B, D_B)
    log_sig = jnp.minimum(fz, 0.0) - jnp.log1p(jnp.exp(-jnp.abs(fz)))
    b = l1m_ref[...] + log_sig
    la = la_ref[...]
    lf_ref[...] = jnp.maximum(la, b) + jnp.log1p(jnp.exp(-jnp.abs(la - b)))
    k_ref[...] = oml_ref[...] * _sigmoid(-fz)
    iv_ref[...] = seg(2 * D_A + 2 * D_B, D_B)
    gz = seg(2 * D_A + 3 * D_B, D_B)
    gs_ref[...] = (gz * _sigmoid(gz)).astype(BF16)
    sa_ref[...] = _sigmoid(seg(2 * D_A + 4 * D_B, D_MODEL)).astype(BF16)
    sb_ref[...] = _sigmoid(seg(2 * D_A + 4 * D_B + D_MODEL, D_MODEL)).astype(BF16)


def _in_proj(x_parts, g, w, lng, lnb, la, l1m, oml, wmix, bmix, t, n_prompt_tiles):
    tm = TM_PROJ
    row = lambda n: pl.BlockSpec((1, n), lambda i: (0, 0))
    tile = lambda n: pl.BlockSpec((tm, n), lambda i: (i, 0))
    sel = lambda i: jnp.minimum(i // n_prompt_tiles, 1)
    p_map, s_map = _split_rows(n_prompt_tiles)
    if len(x_parts) == 1:
        x_specs = [tile(D_MODEL)]
    else:
        x_specs = [pl.BlockSpec((tm, D_MODEL), p_map), pl.BlockSpec((tm, D_MODEL), s_map)]
    t_sample = t - n_prompt_tiles * tm
    shape = lambda rows, n, dt=F32: jax.ShapeDtypeStruct((rows, n), dt)
    out_shapes = ([shape(t, D_A, BF16), shape(t_sample, D_A)] + [shape(t, D_A)] * 4
                  + [shape(t, D_A, BF16)] + [shape(t, D_MODEL, BF16)] * 2)
    out_specs = ([tile(D_A), pl.BlockSpec((tm, D_A), s_map)] + [tile(D_A)] * 5 + [tile(D_MODEL)] * 2)
    return pl.pallas_call(
        functools.partial(_in_proj_body, n_x=len(x_parts), n_prompt_tiles=n_prompt_tiles),
        grid=(t // tm,),
        in_specs=x_specs + [row(D_MODEL),
                            pl.BlockSpec((D_MODEL, IN_COLS), lambda i: (0, 0)),
                            row(D_A), row(D_A), row(D_B), row(D_B), row(D_B),
                            pl.BlockSpec((1, G_A, CHUNK, CHUNK), lambda i: (sel(i), 0, 0, 0)),
                            pl.BlockSpec((1, CHUNK, D_A), lambda i: (sel(i), 0, 0))],
        out_specs=out_specs,
        out_shape=out_shapes,
        compiler_params=_cparams(("arbitrary",)),
        name="in_proj",
    )(*x_parts, g, w, lng, lnb, la, l1m, oml, wmix, bmix)


def _hgrn_block(q_ref, k_ref, v_ref, g_scr, masks_ref, st, r0, cols, lane, row):
    rows = slice(r0, r0 + BLK)
    qh, kh, vh = q_ref[rows, cols], k_ref[rows, cols], v_ref[rows, cols]
    gh = g_scr[rows, cols]
    g_tot = g_scr[r0 + BLK - 1:r0 + BLK, cols]

    o = _dot_nt((qh * jnp.exp(gh)).astype(BF16), st.astype(BF16))

    sc = None
    for l, m in enumerate(LEVEL_HALVES):
        refs = [jnp.broadcast_to(g_scr[r0 + b + m - 1:r0 + b + m, cols], (2 * m, DK))
                for b in range(0, BLK, 2 * m)]
        g_ref_rows = refs[0] if len(refs) == 1 else jnp.concatenate(refs, axis=0)
        right = (row & m) != 0
        x = jnp.where(right, gh - g_ref_rows, g_ref_rows - gh)
        z = (jnp.where(right, qh, kh) * jnp.exp(x)).astype(BF16)
        part = _dot_nt(z, z) * masks_ref[l]
        sc = part if sc is None else sc + part

    strips = []
    for g0 in range(0, BLK, SUBLANES):
        qg, gg = qh[g0:g0 + SUBLANES], gh[g0:g0 + SUBLANES]
        strip = jnp.zeros((SUBLANES, BLK), F32)
        for s in range(SUBLANES):
            src = r0 + g0 + s
            p = qg * jnp.exp(gg - g_scr[src:src + 1, cols]) * k_ref[src:src + 1, cols]
            strip = jnp.where(lane[:SUBLANES] == g0 + s, jnp.sum(p, axis=-1, keepdims=True), strip)
        strips.append(strip)
    diag = jnp.concatenate(strips, axis=0)
    sc = sc + jnp.where(lane <= row, diag, 0.0)

    o = o + _dot(sc.astype(BF16), vh.astype(BF16))
    kt = (kh * jnp.exp(g_tot - gh)).astype(BF16)
    st_new = st * jnp.exp(g_tot) + _dot(vh.T.astype(BF16), kt)
    return o, st_new


def _hgrn_prompt_body(q_ref, lf_ref, k_ref, v_ref, tri_ref, masks_ref, gn_ref, o_ref, sfin_ref,
                      st_ref, g_scr):
    j = pl.program_id(1)
    r = q_ref.shape[0]

    @pl.when(j == 0)
    def _():
        st_ref[...] = jnp.zeros_like(st_ref)

    tri = tri_ref[...]
    g_scr[...] = sum(_dot(tri, part) for part in _split(lf_ref[...], 2))
    lane = lax.broadcasted_iota(jnp.int32, (BLK, BLK), 1)
    row = lax.broadcasted_iota(jnp.int32, (BLK, BLK), 0)

    for h in range(H_B):
        cols = slice(h * DK, (h + 1) * DK)
        st = st_ref[h]
        for r0 in range(0, r, BLK):
            o, st = _hgrn_block(q_ref, k_ref, v_ref, g_scr, masks_ref, st, r0, cols, lane, row)
            o_ref[r0:r0 + BLK, cols] = _rms(o, gn_ref[:, cols])
        st_ref[h] = st

    @pl.when(j == pl.num_programs(1) - 1)
    def _():
        for h in range(H_B):
            sfin_ref[0, h] = st_ref[h].T


def _hgrn_consts():
    i = np.arange(R_HGRN)[:, None]
    j = np.arange(R_HGRN)[None, :]
    tri = ((i // BLK) == (j // BLK)) & (j <= i)
    i, j = np.arange(BLK)[:, None], np.arange(BLK)[None, :]
    masks = [((i // (2 * m)) == (j // (2 * m))) & ((i & m) != 0) & ((j & m) == 0) for m in LEVEL_HALVES]
    return (jnp.asarray(tri.astype(np.float32), dtype=BF16),
            jnp.asarray(np.stack(masks).astype(np.float32)))


def _hgrn_prompt(q, lf, k, iv, gn, n_seq, seq_len):
    r = R_HGRN
    nblk = seq_len // r
    tri, masks = _hgrn_consts()
    blk = pl.BlockSpec((r, D_B), lambda b, j: (b * nblk + j, 0))
    return pl.pallas_call(
        _hgrn_prompt_body,
        grid=(n_seq, nblk),
        in_specs=[blk, blk, blk, blk,
                  pl.BlockSpec(tri.shape, lambda b, j: (0, 0)),
                  pl.BlockSpec(masks.shape, lambda b, j: (0, 0, 0)),
                  pl.BlockSpec((1, D_B), lambda b, j: (0, 0))],
        out_specs=[blk, pl.BlockSpec((1, H_B, DK, DK), lambda b, j: (b, 0, 0, 0))],
        out_shape=[jax.ShapeDtypeStruct((n_seq * seq_len, D_B), F32),
                   jax.ShapeDtypeStruct((n_seq, H_B, DK, DK), F32)],
        scratch_shapes=[pltpu.VMEM((H_B, DK, DK), F32),
                        pltpu.VMEM((r, D_B), F32)],
        compiler_params=_cparams(("arbitrary", "arbitrary")),
        name="hgrn_prompt",
    )(q, lf, k, iv, tri, masks, gn)


def _sub_chunk_exact(q_ref, k_ref, v_ref, g_ref, r0, cols, sub):
    qg = q_ref[pl.ds(r0, sub), cols]
    gg = g_ref[pl.ds(r0, sub), cols]
    kg = k_ref[pl.ds(r0, sub), cols]
    vg = v_ref[pl.ds(r0, sub), cols]
    row = lax.broadcasted_iota(jnp.int32, (sub, 1), 0)
    acc = jnp.zeros((sub, DK), F32)
    for s in range(sub):
        kb = kg[s:s + 1, :]
        gb = gg[s:s + 1, :]
        vb = vg[s:s + 1, :]
        p = qg * jnp.exp(gg - gb) * kb
        rs = jnp.sum(p, axis=-1, keepdims=True)
        acc = acc + jnp.where(row >= s, rs, 0.0) * vb
    return acc


def _hgrn_sample_body(q_ref, lf_ref, k_ref, v_ref, s0_ref, cm_ref, gn_ref, o_ref, s1_ref,
                      g_scr, o_scr, *, dec_seq):
    rows = q_ref.shape[0]
    n_seq = rows // dec_seq
    parts = _split(lf_ref[...], 3)

    def cum(idx):
        m = cm_ref[idx]
        return _dot(m, parts[0]) + _dot(m, parts[1]) + _dot(m, parts[2])

    g_all = cum(0)
    g_scr[...] = g_all
    g_rev = cum(1)
    g_tot = cum(2)
    lane = lax.broadcasted_iota(jnp.int32, (DK, rows), 1)

    for h in range(H_B):
        cols = slice(h * DK, (h + 1) * DK)
        qt = (q_ref[:, cols] * jnp.exp(g_all[:, cols])).astype(BF16)
        kt_t = (k_ref[:, cols] * jnp.exp(g_rev[:, cols])).T
        dec_t = jnp.exp(g_tot[:, cols]).T
        vb = v_ref[:, cols].astype(BF16)
        for n in range(n_seq):
            r0 = n * dec_seq
            s0 = s0_ref[0, n, h]
            o_scr[r0:r0 + dec_seq, cols] = _dot(qt[r0:r0 + dec_seq, :], s0.astype(BF16))
            in_seq = (lane >= r0) & (lane < r0 + dec_seq)
            u = _dot(jnp.where(in_seq, kt_t, 0.0).astype(BF16), vb)
            s1_ref[n, h] = s0 * dec_t[:, r0:r0 + 1] + u

    def group(n, carry):
        r0 = pl.multiple_of(n * dec_seq, dec_seq)
        for h in range(H_B):
            cols = slice(h * DK, (h + 1) * DK)
            o_scr[pl.ds(r0, dec_seq), cols] += _sub_chunk_exact(q_ref, k_ref, v_ref, g_scr, r0, cols,
                                                                 dec_seq)
        return carry

    lax.fori_loop(0, n_seq, group, 0)

    for h in range(H_B):
        cols = slice(h * DK, (h + 1) * DK)
        o_ref[:, cols] = _rms(o_scr[:, cols], gn_ref[:, cols])


def _hgrn_sample(q, lf, k, iv, state_in, layer, gn, row0, n_seq, dec_seq):
    rows = SEQ_PER_STEP * dec_seq
    i = np.arange(rows)[:, None]
    j = np.arange(rows)[None, :]
    same = (i // dec_seq) == (j // dec_seq)
    cm = jnp.asarray(np.stack([same & (j <= i), same & (j > i), same]).astype(np.float32), dtype=BF16)
    blk0 = row0 // rows
    blk = pl.BlockSpec((rows, D_B), lambda n: (blk0 + n, 0))
    oblk = pl.BlockSpec((rows, D_B), lambda n: (n, 0))
    return pl.pallas_call(
        functools.partial(_hgrn_sample_body, dec_seq=dec_seq),
        grid=(n_seq // SEQ_PER_STEP,),
        in_specs=[blk, blk, blk, blk,
                  pl.BlockSpec((1, SEQ_PER_STEP, H_B, DK, DK), lambda n: (layer, n, 0, 0, 0)),
                  pl.BlockSpec(cm.shape, lambda n: (0, 0, 0)),
                  pl.BlockSpec((1, D_B), lambda n: (0, 0))],
        out_specs=[oblk, pl.BlockSpec((SEQ_PER_STEP, H_B, DK, DK), lambda n: (n, 0, 0, 0))],
        out_shape=[jax.ShapeDtypeStruct((n_seq * dec_seq, D_B), F32),
                   jax.ShapeDtypeStruct((n_seq, H_B, DK, DK), F32)],
        scratch_shapes=[pltpu.VMEM((rows, D_B), F32), pltpu.VMEM((rows, D_B), F32)],
        compiler_params=_cparams(("arbitrary",)),
        name="hgrn_sample",
    )(q, lf, k, iv, state_in, cm, gn)


FF_SPLITS = ((0, 1536), (1536, 1280))


def _swiglu(hb, wg_ref, wu_ref, wd_ref, idx):
    acc = None
    for lo, n in FF_SPLITS:
        g = _dot(hb, wg_ref[idx + (slice(None), slice(lo, lo + n))])
        u = _dot(hb, wu_ref[idx + (slice(None), slice(lo, lo + n))])
        act = (g * _sigmoid(g) * u).astype(BF16)
        part = _dot(act, wd_ref[idx + (slice(lo, lo + n), slice(None))])
        acc = part if acc is None else acc + part
    return acc


def _route(hn, wh_ref, wl_ref, tok_ref, rank_t_ref, cnt_ref):
    tm = hn.shape[0]
    hh = hn.astype(BF16)
    hl = (hn - hh.astype(F32)).astype(BF16)
    logits = _dot(hh, wh_ref[...]) + _dot(hh, wl_ref[...]) + _dot(hl, wh_ref[...])
    lane = lax.broadcasted_iota(jnp.int32, logits.shape, 1).astype(F32)
    neg = np.float32(-np.inf)
    logits = jnp.where(lane < N_EXPERTS, logits, neg)
    m1 = jnp.max(logits, axis=-1, keepdims=True)
    i1 = jnp.min(jnp.where(logits == m1, lane, float(LANES)), axis=-1, keepdims=True)
    rest = jnp.where(lane == i1, neg, logits)
    m2 = jnp.max(rest, axis=-1, keepdims=True)
    i2 = jnp.min(jnp.where(rest == m2, lane, float(LANES)), axis=-1, keepdims=True)
    e2 = jnp.exp(m2 - m1)
    den = 1.0 + e2

    routed = (lane == i1) | (lane == i2)
    sel = jnp.where(routed, 1.0, 0.0)
    r = lax.broadcasted_iota(jnp.int32, (tm, tm), 0)
    c = lax.broadcasted_iota(jnp.int32, (tm, tm), 1)
    before = jnp.where(c < r, 1.0, 0.0).astype(BF16)
    rank = _dot(before, sel.astype(BF16))
    rank1 = jnp.sum(jnp.where(lane == i1, rank, 0.0), axis=-1, keepdims=True)
    rank2 = jnp.sum(jnp.where(lane == i2, rank, 0.0), axis=-1, keepdims=True)
    tok_ref[...] = jnp.where(lane == 0.0, i1 * KEY_MUL + rank1,
                             jnp.where(lane == 1.0, i2 * KEY_MUL + rank2,
                                       jnp.where(lane == 2.0, 1.0 / den,
                                                 jnp.where(lane == 3.0, e2 / den, 0.0))))
    rank_t = jnp.where(routed, rank, -1.0).T
    for e in range(N_EXPERTS):
        rank_t_ref[0, e] = rank_t[e:e + 1, :]
    cnt_ref[0] = jnp.sum(sel, axis=0, keepdims=True).astype(jnp.int32)


def _post_mix_body(*refs, n_h, n_prompt_tiles, mode):
    h_parts, refs = refs[:n_h], refs[n_h:]
    a_ref, op_ref, os_ref, gs_ref, sa_ref, sb_ref, wa_ref, wb_ref, wo_ref, gf_ref = refs[:10]
    o = _select_rows((op_ref, os_ref), n_prompt_tiles)
    pa = _dot(a_ref[...], wa_ref[...])
    pb = _dot((o * gs_ref[...].astype(F32)).astype(BF16), wb_ref[...])
    merged = sa_ref[...].astype(F32) * pa + sb_ref[...].astype(F32) * pb
    h1 = _select_rows(h_parts, n_prompt_tiles) + _dot(merged.astype(BF16), wo_ref[...])
    hn = _rms(h1, gf_ref[...])
    if mode == "dense":
        wg_ref, wu_ref, wd_ref, out_ref = refs[10:]
        out_ref[...] = h1 + _swiglu(hn.astype(BF16), wg_ref, wu_ref, wd_ref, ())
    else:
        wh_ref, wl_ref, h1_ref, hn_ref, tok_ref, rank_t_ref, cnt_ref = refs[10:]
        h1_ref[...] = h1
        hn_ref[...] = hn.astype(BF16)
        _route(hn, wh_ref, wl_ref, tok_ref, rank_t_ref, cnt_ref)


def _post_mix(h_parts, a, o_p, o_s, gs, sa, sb, wa, wb, wo, gf, extra_w, t_prompt, mode):
    t = a.shape[0]
    tm = TM_PROJ if mode == "dense" else TM_MOE
    n_prompt_tiles = t_prompt // tm
    tile = lambda n: pl.BlockSpec((tm, n), lambda i: (i, 0))
    full = lambda w: pl.BlockSpec(w.shape, lambda i: (0,) * w.ndim, pipeline_mode=pl.Buffered(1))
    p_map, s_map = _split_rows(n_prompt_tiles)
    if len(h_parts) == 1:
        h_specs = [tile(D_MODEL)]
    else:
        h_specs = [pl.BlockSpec((tm, D_MODEL), p_map), pl.BlockSpec((tm, D_MODEL), s_map)]
    act = jax.ShapeDtypeStruct((t, D_MODEL), F32)
    if mode == "dense":
        out_specs, out_shape, scratch = tile(D_MODEL), act, []
    else:
        out_specs = [tile(D_MODEL), tile(D_MODEL), tile(LANES),
                     pl.BlockSpec((1, N_EXPERTS, 1, tm), lambda i: (i, 0, 0, 0)),
                     pl.BlockSpec((1, 1, LANES), lambda i: (i, 0, 0))]
        out_shape = [act, jax.ShapeDtypeStruct((t, D_MODEL), BF16),
                     jax.ShapeDtypeStruct((t, LANES), F32),
                     jax.ShapeDtypeStruct((t // tm, N_EXPERTS, 1, tm), F32),
                     jax.ShapeDtypeStruct((t // tm, 1, LANES), jnp.int32)]
        scratch = []
    weights = (wa, wb, wo, gf) + tuple(extra_w)
    return pl.pallas_call(
        functools.partial(_post_mix_body, n_h=len(h_parts), n_prompt_tiles=n_prompt_tiles, mode=mode),
        grid=(t // tm,),
        in_specs=h_specs + [tile(D_A),
                            pl.BlockSpec((tm, D_B), p_map), pl.BlockSpec((tm, D_B), s_map),
                            tile(D_B), tile(D_MODEL), tile(D_MODEL)] + [full(w) for w in weights],
        out_specs=out_specs,
        out_shape=out_shape,
        scratch_shapes=scratch,
        compiler_params=_cparams(("arbitrary",)),
        name="post_mix_" + mode,
    )(*h_parts, a, o_p, o_s, gs, sa, sb, *weights)


def _chunk_copy(src_ref, src_row, dst_ref, dst_row, n, sem):
    return pltpu.make_async_copy(src_ref.at[pl.ds(src_row, n)], dst_ref.at[pl.ds(dst_row, n)], sem)


def _aligned(row):
    return pl.multiple_of(row, SUBLANES)


def _dispatch_body(ce_ref, cj_ref, crow_ref, tot_ref, zstart_ref, zrem_ref, tail_ref,
                   rank_t_ref, hn_ref, xs_ref, sel_scr, stage, zero_ref, sems, zsem):
    i = pl.program_id(0)
    tm = hn_ref.shape[0]
    par = i % 2

    def fills(fn):
        for e in range(N_EXPERTS):
            s = zstart_ref[e]
            fn(_chunk_copy(zero_ref, 0, xs_ref, _aligned(s), TMG, zsem))
            rem = zrem_ref[e]
            for b in ZERO_BITS:
                @pl.when((rem & b) != 0)
                def _():
                    fn(_chunk_copy(zero_ref, 0, xs_ref, _aligned(s + TMG + (rem & ~(2 * b - 1))), b, zsem))

        def tail(n, carry):
            fn(_chunk_copy(zero_ref, 0, xs_ref, _aligned(tail_ref[0] + n * TMG), TMG, zsem))
            return carry

        lax.fori_loop(0, tail_ref[1], tail, 0)

    @pl.when(i == 0)
    def _():
        zero_ref[...] = jnp.zeros_like(zero_ref)
        fills(lambda cp: cp.start())
        fills(lambda cp: cp.wait())

    row = lax.broadcasted_iota(jnp.int32, (CH, tm), 0).astype(F32)
    for s in range(N_SLOTS):
        e = ce_ref[i * N_SLOTS + s]
        first = (cj_ref[i * N_SLOTS + s] * CH).astype(F32)
        sel_scr[s * CH:(s + 1) * CH, :] = jnp.where(rank_t_ref[0, e] == row + first, 1.0, 0.0).astype(BF16)

    hb = hn_ref[...]
    for g in range(N_SLOTS // SLOTS_PER_DOT):
        rows = slice(g * SLOTS_PER_DOT * CH, (g + 1) * SLOTS_PER_DOT * CH)

        @pl.when(tot_ref[i] > g * SLOTS_PER_DOT)
        def _():
            stage[par, rows, :] = _dot(sel_scr[rows, :], hb)

    def chunk_copy(s, tile, buf):
        return _chunk_copy(stage.at[buf], s * CH, xs_ref, _aligned(crow_ref[tile * N_SLOTS + s]), CH,
                           sems.at[buf])

    @pl.when(i > 0)
    def _():
        for s in range(N_SLOTS):
            @pl.when(s < tot_ref[i - 1])
            def _():
                chunk_copy(s, i - 1, 1 - par).wait()

    for s in range(N_SLOTS):
        @pl.when(s < tot_ref[i])
        def _():
            chunk_copy(s, i, par).start()

    @pl.when(i == pl.num_programs(0) - 1)
    def _():
        for s in range(N_SLOTS):
            @pl.when(s < tot_ref[i])
            def _():
                chunk_copy(s, i, par).wait()


def _sorted_rows(t):
    n_seg = (t // TM_MOE) * N_EXPERTS
    rows = TOP_K * t + n_seg * (SUBLANES - 1) + N_EXPERTS * (TMG - 1) + N_EXPERTS * TMG
    return -(-rows // TMG) * TMG


def _dispatch(hn, rank_t, plan):
    t = hn.shape[0]
    tm = TM_MOE
    scalars = (plan["ce"], plan["cj"], plan["crow"], plan["tot"], plan["zstart"], plan["zrem"], plan["tail"])
    grid_spec = pltpu.PrefetchScalarGridSpec(
        num_scalar_prefetch=len(scalars),
        grid=(t // tm,),
        in_specs=[pl.BlockSpec((1, N_EXPERTS, 1, tm), lambda i, *_: (i, 0, 0, 0)),
                  pl.BlockSpec((tm, D_MODEL), lambda i, *_: (i, 0))],
        out_specs=pl.BlockSpec(memory_space=pl.ANY),
        scratch_shapes=[pltpu.VMEM((N_SLOTS * CH, tm), BF16),
                        pltpu.VMEM((2, N_SLOTS * CH, D_MODEL), F32),
                        pltpu.VMEM((TMG, D_MODEL), F32),
                        pltpu.SemaphoreType.DMA((2,)), pltpu.SemaphoreType.DMA],
    )
    return pl.pallas_call(
        _dispatch_body,
        grid_spec=grid_spec,
        out_shape=jax.ShapeDtypeStruct((_sorted_rows(t), D_MODEL), F32),
        compiler_params=_cparams(("arbitrary",)),
        name="moe_dispatch",
    )(*scalars, rank_t, hn)


def _ffn_grouped_body(te_ref, live_ref, xs_ref, wg_ref, wu_ref, wd_ref, ys_ref):
    j = pl.program_id(0)

    @pl.when(live_ref[j] != 0)
    def _():
        ys_ref[...] = _swiglu(xs_ref[...].astype(BF16), wg_ref, wu_ref, wd_ref, (0,))

    @pl.when(live_ref[j] == 0)
    def _():
        ys_ref[...] = jnp.zeros_like(ys_ref)


def _ffn_grouped(xs, tile_expert, tile_live, wg, wu, wd):
    n_rows = xs.shape[0]
    row_tile = lambda j, te, live: (j, 0)
    expert = lambda j, te, live: (te[j], 0, 0)
    grid_spec = pltpu.PrefetchScalarGridSpec(
        num_scalar_prefetch=2,
        grid=(n_rows // TMG,),
        in_specs=[pl.BlockSpec((TMG, D_MODEL), row_tile),
                  pl.BlockSpec((1, D_MODEL, D_FF), expert),
                  pl.BlockSpec((1, D_MODEL, D_FF), expert),
                  pl.BlockSpec((1, D_FF, D_MODEL), expert)],
        out_specs=pl.BlockSpec((TMG, D_MODEL), row_tile),
    )
    return pl.pallas_call(
        _ffn_grouped_body,
        grid_spec=grid_spec,
        out_shape=jax.ShapeDtypeStruct((n_rows, D_MODEL), F32),
        compiler_params=_cparams(("arbitrary",)),
        name="moe_ffn",
    )(tile_expert, tile_live, xs, wg, wu, wd)


def _combine_body(ce_ref, cj_ref, crow_ref, tot_ref, tok_ref, h1_ref, gfin_ref, ys_ref,
                  yp_ref, ysm_ref, sel_scr, stage, sems, *, n_prompt_tiles):
    i = pl.program_id(0)
    tm = h1_ref.shape[0]
    par = i % 2

    def fetch(s, tile, buf):
        return _chunk_copy(ys_ref, _aligned(crow_ref[tile * N_SLOTS + s]), stage.at[buf], s * CH, CH,
                           sems.at[buf])

    def fetches(tile, buf, fn):
        for s in range(N_SLOTS):
            @pl.when(s < tot_ref[tile])
            def _():
                fn(fetch(s, tile, buf))

    @pl.when(i == 0)
    def _():
        stage[...] = jnp.zeros_like(stage)
        fetches(0, 0, lambda cp: cp.start())

    @pl.when(i + 1 < pl.num_programs(0))
    def _():
        fetches(i + 1, 1 - par, lambda cp: cp.start())

    tok = tok_ref[...]
    key1, key2 = tok[:, 0:1], tok[:, 1:2]
    gate1, gate2 = tok[:, 2:3], tok[:, 3:4]
    lane = lax.broadcasted_iota(jnp.int32, (tm, CH), 1).astype(F32)
    for s in range(N_SLOTS):
        first = (ce_ref[i * N_SLOTS + s] * KEY_MUL + cj_ref[i * N_SLOTS + s] * CH).astype(F32)
        key = lane + first
        sel_scr[:, s * CH:(s + 1) * CH] = jnp.where(key1 == key, gate1,
                                                    jnp.where(key2 == key, gate2, 0.0)).astype(BF16)

    fetches(i, par, lambda cp: cp.wait())
    val = h1_ref[...] + _dot(sel_scr[...], stage[par].astype(BF16))
    y = _rms(val, gfin_ref[...])

    @pl.when(i < n_prompt_tiles)
    def _():
        yp_ref[...] = y

    @pl.when(i >= n_prompt_tiles)
    def _():
        ysm_ref[...] = y


def _combine(ys, tok, h1, gfin, plan, t_prompt):
    t = h1.shape[0]
    tm = TM_MOE
    n_prompt_tiles = t_prompt // tm
    p_map, s_map = _split_rows(n_prompt_tiles)
    scalars = (plan["ce"], plan["cj"], plan["crow"], plan["tot"])
    grid_spec = pltpu.PrefetchScalarGridSpec(
        num_scalar_prefetch=len(scalars),
        grid=(t // tm,),
        in_specs=[pl.BlockSpec((tm, LANES), lambda i, *_: (i, 0)),
                  pl.BlockSpec((tm, D_MODEL), lambda i, *_: (i, 0)),
                  pl.BlockSpec((1, D_MODEL), lambda i, *_: (0, 0)),
                  pl.BlockSpec(memory_space=pl.ANY)],
        out_specs=[pl.BlockSpec((tm, D_MODEL), lambda i, *_: p_map(i)),
                   pl.BlockSpec((tm, D_MODEL), lambda i, *_: s_map(i))],
        scratch_shapes=[pltpu.VMEM((tm, N_SLOTS * CH), BF16),
                        pltpu.VMEM((2, N_SLOTS * CH, D_MODEL), F32),
                        pltpu.SemaphoreType.DMA((2,))],
    )
    return pl.pallas_call(
        functools.partial(_combine_body, n_prompt_tiles=n_prompt_tiles),
        grid_spec=grid_spec,
        out_shape=[jax.ShapeDtypeStruct((t_prompt, D_MODEL), F32),
                   jax.ShapeDtypeStruct((t - t_prompt, D_MODEL), F32)],
        compiler_params=_cparams(("arbitrary",)),
        name="moe_combine",
    )(*scalars, tok, h1, gfin, ys)


def _moe_plan(cnt, n_rows):
    i32 = jnp.int32
    seg = (cnt + SUBLANES - 1) // SUBLANES * SUBLANES
    g_len = jnp.sum(seg, axis=0)
    g_pad = (g_len + TMG - 1) // TMG * TMG
    g_span = g_pad + TMG
    g_end = jnp.cumsum(g_span)
    g_start = g_end - g_span
    seg_start = g_start[None, :] + jnp.cumsum(seg, axis=0) - seg
    n_chunks = (cnt + CH - 1) // CH
    c_end = jnp.cumsum(n_chunks, axis=1)
    tot = c_end[:, -1]
    slot = jnp.arange(N_SLOTS, dtype=i32)[None, :]
    live = slot < tot[:, None]
    ce = jnp.minimum(jnp.sum(slot[:, :, None] >= c_end[:, None, :], axis=-1), N_EXPERTS - 1)
    cj = slot - jnp.take_along_axis(c_end - n_chunks, ce, axis=1)
    cj = jnp.where(live, cj, CJ_NONE)
    crow = jnp.where(live, jnp.take_along_axis(seg_start, ce, axis=1) + CH * cj, 0)
    tile_row = jnp.arange(n_rows // TMG, dtype=i32) * TMG
    tile_expert = jnp.minimum(jnp.sum(tile_row[:, None] >= g_end[None, :], axis=1), N_EXPERTS - 1)
    tile_live = tile_row < (g_start + g_pad)[tile_expert]
    flat = lambda x: x.reshape(-1).astype(i32)
    return dict(ce=flat(ce), cj=flat(cj), crow=flat(crow), tot=flat(tot),
                zstart=flat(g_start + g_len), zrem=flat(g_pad - g_len),
                tail=jnp.stack([g_end[-1], (n_rows - g_end[-1]) // TMG]).astype(i32),
                tile_expert=flat(tile_expert), tile_live=flat(tile_live))


def _moe_routed(h1, hn, tok, rank_t, cnt, wg, wu, wd, gfin, t_prompt):
    plan = _moe_plan(cnt[:, 0, :N_EXPERTS], _sorted_rows(h1.shape[0]))
    xs = _dispatch(hn, rank_t, plan)
    ys = _ffn_grouped(xs, plan["tile_expert"], plan["tile_live"], wg, wu, wd)
    return _combine(ys, tok, h1, gfin, plan, t_prompt)


def _mix_consts(w_spatial, b_spatial, dec_seq):
    tril = jnp.tril(jnp.ones((CHUNK, CHUNK), F32))
    w_p = w_spatial * tril
    reps = CHUNK // dec_seq
    w_s = jnp.stack([jnp.kron(jnp.eye(reps, dtype=F32), w_p[g, :dec_seq, :dec_seq]) for g in range(G_A)])
    wmix = jnp.stack([w_p, w_s]).astype(BF16)
    b_p = jnp.repeat(b_spatial.T, CH_A, axis=1)
    b_s = jnp.tile(b_p[:dec_seq], (reps, 1))
    return wmix, jnp.stack([b_p, b_s])


def kernel(x_prompt, x_sample, state_hgrn, norm_mix_g, w_in, ln_v_g, ln_v_b, w_spatial, b_spatial,
           lower_bounds, hgrn_norm_g, w_branch_a, w_branch_b, w_out, norm_ffn_g, dense_w_gate,
           dense_w_up, dense_w_down, router_w, moe_w_gate, moe_w_up, moe_w_down, final_norm_g):
    n_seq, seq_len, _ = x_prompt.shape
    dec_batch, dec_seq, _ = x_sample.shape
    t_prompt = n_seq * seq_len
    t_sample = dec_batch * dec_seq
    t = t_prompt + t_sample
    n_prompt_tiles = t_prompt // TM_PROJ

    lb_cum = jnp.cumsum(jax.nn.softmax(lower_bounds.astype(F32), axis=0), axis=0)
    lb_all = lb_cum - lb_cum[0:1]
    log_lb, log1m_lb, one_m_lb = jnp.log(lb_all), jnp.log1p(-lb_all), 1.0 - lb_all

    row = lambda p: p.reshape(1, -1)
    h_parts = (x_prompt.reshape(t_prompt, D_MODEL), x_sample.reshape(t_sample, D_MODEL))
    state_in = state_hgrn.astype(F32)
    states_p, states_s, v_rows = [], [], []
    for l in range(DEPTH):
        wmix, bmix = _mix_consts(w_spatial[l], b_spatial[l], dec_seq)
        a, v_s, q, lf, k, iv, gs, sa, sb = _in_proj(
            h_parts, row(norm_mix_g[l]), w_in[l].astype(BF16), row(ln_v_g[l]), row(ln_v_b[l]),
            row(log_lb[l]), row(log1m_lb[l]), row(one_m_lb[l]), wmix, bmix, t, n_prompt_tiles)
        gn = row(hgrn_norm_g[l])
        o_p, s_p = _hgrn_prompt(q, lf, k, iv, gn, n_seq, seq_len)
        o_s, s_s = _hgrn_sample(q, lf, k, iv, state_in, l, gn, t_prompt, dec_batch, dec_seq)
        mix_w = (w_branch_a[l].astype(BF16), w_branch_b[l].astype(BF16), w_out[l].astype(BF16),
                 row(norm_ffn_g[l]))
        i = l // 2
        if l % 2 == 0:
            ffn_w = (dense_w_gate[i].astype(BF16), dense_w_up[i].astype(BF16), dense_w_down[i].astype(BF16))
            h_parts = (_post_mix(h_parts, a, o_p, o_s, gs, sa, sb, *mix_w, ffn_w, t_prompt, "dense"),)
        else:
            rw = jnp.pad(router_w[i], ((0, 0), (0, LANES - N_EXPERTS)))
            rw_hi = rw.astype(BF16)
            rw_lo = (rw - rw_hi.astype(F32)).astype(BF16)
            h1, hn, tok, rank_t, cnt = _post_mix(h_parts, a, o_p, o_s, gs, sa, sb, *mix_w,
                                                 (rw_hi, rw_lo), t_prompt, "router")
            h_parts = _moe_routed(h1, hn, tok, rank_t, cnt, moe_w_gate[i].astype(BF16),
                                  moe_w_up[i].astype(BF16), moe_w_down[i].astype(BF16),
                                  row(final_norm_g), t_prompt)
        states_p.append(s_p)
        states_s.append(s_s)
        v_rows.append(v_s.reshape(dec_batch, dec_seq, D_A))

    y_prompt = h_parts[0].reshape(n_seq, seq_len, D_MODEL)
    y_sample = h_parts[1].reshape(dec_batch, dec_seq, D_MODEL)
    return (y_prompt, y_sample, jnp.stack(states_p).astype(x_prompt.dtype),
            jnp.stack(states_s).astype(state_hgrn.dtype), jnp.stack(v_rows))
```

```python
import functools

import numpy as np
import jax
import jax.numpy as jnp
from jax import lax
from jax.experimental import pallas as pl
from jax.experimental.pallas import tpu as pltpu

F32 = jnp.float32
BF16 = jnp.bfloat16

D_MODEL = 1024
DEPTH = 2
D_A = 512
G_A = 4
CH_A = 128
CHUNK = 128
D_B = 512
H_B = 4
DK = 128
IN_COLS = 5120
D_FF = 2816
N_EXPERTS = 8
TOP_K = 2
EPS = 1e-6

LANES = 128
SUBLANES = 8
VMEM_LIMIT = 52 * 1024 * 1024

TM_IN = 256
TM_PROJ = 256
CAST_STEPS = 64
TM_MOE = 512
TMG = 256
CH = 128
N_SLOTS = -(-(TOP_K * TM_MOE + N_EXPERTS * (CH - 1)) // CH)
SLOTS_PER_DOT = 4
KEY_MUL = 4096
CJ_NONE = N_SLOTS
ZERO_BITS = (128, 64, 32, 16, 8)
R_HGRN = 256
BLK = 128
LEVEL_HALVES = (64, 32, 16, 8)
SEQ_PER_STEP = 16


def _cparams(sem):
    return pltpu.CompilerParams(dimension_semantics=sem, vmem_limit_bytes=VMEM_LIMIT)


def _dot(a, b):
    return jnp.dot(a, b, preferred_element_type=F32)


def _dot_nt(a, b):
    return lax.dot_general(a, b, (((1,), (1,)), ((), ())), preferred_element_type=F32)


def _split(x, terms):
    out = []
    for _ in range(terms - 1):
        hi = x.astype(BF16)
        out.append(hi)
        x = x - hi.astype(F32)
    out.append(x.astype(BF16))
    return out


def _rms(x, g):
    return x * lax.rsqrt(jnp.mean(x * x, axis=-1, keepdims=True) + EPS) * g


def _gelu(x):
    return 0.5 * x * (1.0 + lax.erf(x * np.float32(2.0 ** -0.5)))


def _sigmoid(x):
    return jax.nn.sigmoid(x)


def _split_rows(n_prompt_tiles):
    return (lambda i: (jnp.minimum(i, n_prompt_tiles - 1), 0),
            lambda i: (jnp.maximum(i - n_prompt_tiles, 0), 0))


def _select_rows(parts, n_prompt_tiles):
    if len(parts) == 1:
        return parts[0][...]
    return jnp.where(pl.program_id(0) < n_prompt_tiles, parts[0][...], parts[1][...])


def _softplus_neg_abs(x):
    return jnp.log(1.0 + jnp.exp(-jnp.abs(x)))


def _in_proj_body(*refs, n_x, n_prompt_tiles):
    x_parts, refs = refs[:n_x], refs[n_x:]
    (g_ref, w_ref, lng_ref, lnb_ref, la_ref, l1m_ref, oml_ref, wmix_ref, bmix_ref,
     a_ref, vs_ref, q_ref, lf_ref, k_ref, iv_ref, gs_ref, sa_ref, sb_ref) = refs
    tm = a_ref.shape[0]
    xn = _rms(_select_rows(x_parts, n_prompt_tiles), g_ref[...]).astype(BF16)

    def seg(lo, n):
        return _dot(xn, w_ref[:, lo:lo + n])

    u = _gelu(seg(0, D_A))
    iv_ref[...] = seg(2 * D_A + 2 * D_B, D_B)
    v = _gelu(seg(D_A, D_A))
    vc = v - jnp.mean(v, axis=-1, keepdims=True)
    v = vc * lax.rsqrt(jnp.mean(vc * vc, axis=-1, keepdims=True) + EPS) * lng_ref[...] + lnb_ref[...]
    vs_ref[...] = v
    gate_a, gate_b = 2 * D_A + 4 * D_B, 2 * D_A + 4 * D_B + D_MODEL
    half = D_MODEL // 2
    sa_ref[:, :half] = _sigmoid(seg(gate_a, half)).astype(BF16)

    vb = v.astype(BF16)
    for c in range(tm // CHUNK):
        rows = slice(c * CHUNK, (c + 1) * CHUNK)
        for g in range(G_A):
            cols = slice(g * CH_A, (g + 1) * CH_A)
            mixed = _dot(wmix_ref[0, g], vb[rows, cols]) + bmix_ref[0, :, cols]
            a_ref[rows, cols] = (u[rows, cols] * mixed).astype(BF16)

    fz = seg(2 * D_A + D_B, D_B)
    log_sig = jnp.minimum(fz, 0.0) - _softplus_neg_abs(fz)
    b = l1m_ref[...] + log_sig
    la = la_ref[...]
    lf_ref[...] = jnp.maximum(la, b) + _softplus_neg_abs(la - b)
    k_ref[...] = oml_ref[...] * _sigmoid(-fz)
    sa_ref[:, half:] = _sigmoid(seg(gate_a + half, half)).astype(BF16)
    zq = seg(2 * D_A, D_B)
    q_ref[...] = zq * _sigmoid(zq)
    sb_ref[:, :half] = _sigmoid(seg(gate_b, half)).astype(BF16)
    gz = seg(2 * D_A + 3 * D_B, D_B)
    gs_ref[...] = (gz * _sigmoid(gz)).astype(BF16)
    sb_ref[:, half:] = _sigmoid(seg(gate_b + half, half)).astype(BF16)


def _in_proj(x_parts, g, w, lng, lnb, la, l1m, oml, wmix, bmix, t, t_prompt):
    tm = TM_IN
    n_prompt_tiles = t_prompt // tm
    row = lambda n: pl.BlockSpec((1, n), lambda i: (0, 0))
    tile = lambda n: pl.BlockSpec((tm, n), lambda i: (i, 0))
    sel = lambda i: jnp.minimum(i // n_prompt_tiles, 1)
    p_map, s_map = _split_rows(n_prompt_tiles)
    if len(x_parts) == 1:
        x_specs = [tile(D_MODEL)]
    else:
        x_specs = [pl.BlockSpec((tm, D_MODEL), p_map), pl.BlockSpec((tm, D_MODEL), s_map)]
    t_sample = t - n_prompt_tiles * tm
    shape = lambda rows, n, dt=F32: jax.ShapeDtypeStruct((rows, n), dt)
    out_shapes = ([shape(t, D_A, BF16), shape(t_sample, D_A)] + [shape(t, D_A)] * 4
                  + [shape(t, D_A, BF16)] + [shape(t, D_MODEL, BF16)] * 2)
    out_specs = ([tile(D_A), pl.BlockSpec((tm, D_A), s_map)] + [tile(D_A)] * 5 + [tile(D_MODEL)] * 2)
    return pl.pallas_call(
        functools.partial(_in_proj_body, n_x=len(x_parts), n_prompt_tiles=n_prompt_tiles),
        grid=(t // tm,),
        in_specs=x_specs + [row(D_MODEL),
                            pl.BlockSpec((D_MODEL, IN_COLS), lambda i: (0, 0), pipeline_mode=pl.Buffered(1)),
                            row(D_A), row(D_A), row(D_B), row(D_B), row(D_B),
                            pl.BlockSpec((1, G_A, CHUNK, CHUNK), lambda i: (sel(i), 0, 0, 0)),
                            pl.BlockSpec((1, CHUNK, D_A), lambda i: (sel(i), 0, 0))],
        out_specs=out_specs,
        out_shape=out_shapes,
        compiler_params=_cparams(("arbitrary",)),
        name="in_proj",
    )(*x_parts, g, w, lng, lnb, la, l1m, oml, wmix, bmix)


def _hgrn_block(q_ref, k_ref, v_ref, g_scr, masks_ref, st, r0, cols, lane, row):
    rows = slice(r0, r0 + BLK)
    qh, kh, vh = q_ref[rows, cols], k_ref[rows, cols], v_ref[rows, cols]
    gh = g_scr[rows, cols]
    g_tot = g_scr[r0 + BLK - 1:r0 + BLK, cols]

    o = _dot_nt((qh * jnp.exp(gh)).astype(BF16), st.astype(BF16))

    sc = None
    for l, m in enumerate(LEVEL_HALVES):
        refs = [jnp.broadcast_to(g_scr[r0 + b + m - 1:r0 + b + m, cols], (2 * m, DK))
                for b in range(0, BLK, 2 * m)]
        g_ref_rows = refs[0] if len(refs) == 1 else jnp.concatenate(refs, axis=0)
        right = (row & m) != 0
        x = jnp.where(right, gh - g_ref_rows, g_ref_rows - gh)
        z = (jnp.where(right, qh, kh) * jnp.exp(x)).astype(BF16)
        part = _dot_nt(z, z) * masks_ref[l]
        sc = part if sc is None else sc + part

    strips = []
    for g0 in range(0, BLK, SUBLANES):
        qg, gg = qh[g0:g0 + SUBLANES], gh[g0:g0 + SUBLANES]
        strip = jnp.zeros((SUBLANES, BLK), F32)
        for s in range(SUBLANES):
            src = r0 + g0 + s
            p = qg * jnp.exp(gg - g_scr[src:src + 1, cols]) * k_ref[src:src + 1, cols]
            strip = jnp.where(lane[:SUBLANES] == g0 + s, jnp.sum(p, axis=-1, keepdims=True), strip)
        strips.append(strip)
    diag = jnp.concatenate(strips, axis=0)
    sc = sc + jnp.where(lane <= row, diag, 0.0)

    o = o + _dot(sc.astype(BF16), vh.astype(BF16))
    kt = (kh * jnp.exp(g_tot - gh)).astype(BF16)
    st_new = st * jnp.exp(g_tot) + _dot(vh.T.astype(BF16), kt)
    return o, st_new


def _hgrn_prompt_body(q_ref, lf_ref, k_ref, v_ref, tri_ref, masks_ref, gn_ref, o_ref, sfin_ref,
                      st_ref, g_scr):
    j = pl.program_id(1)
    r = q_ref.shape[0]

    @pl.when(j == 0)
    def _():
        st_ref[...] = jnp.zeros_like(st_ref)

    tri = tri_ref[...]
    g_scr[...] = sum(_dot(tri, part) for part in _split(lf_ref[...], 2))
    lane = lax.broadcasted_iota(jnp.int32, (BLK, BLK), 1)
    row = lax.broadcasted_iota(jnp.int32, (BLK, BLK), 0)

    for h in range(H_B):
        cols = slice(h * DK, (h + 1) * DK)
        st = st_ref[h]
        for r0 in range(0, r, BLK):
            o, st = _hgrn_block(q_ref, k_ref, v_ref, g_scr, masks_ref, st, r0, cols, lane, row)
            o_ref[r0:r0 + BLK, cols] = _rms(o, gn_ref[:, cols])
        st_ref[h] = st

    @pl.when(j == pl.num_programs(1) - 1)
    def _():
        for h in range(H_B):
            sfin_ref[0, h] = st_ref[h].T


def _hgrn_consts():
    i = np.arange(R_HGRN)[:, None]
    j = np.arange(R_HGRN)[None, :]
    tri = ((i // BLK) == (j // BLK)) & (j <= i)
    i, j = np.arange(BLK)[:, None], np.arange(BLK)[None, :]
    masks = [((i // (2 * m)) == (j // (2 * m))) & ((i & m) != 0) & ((j & m) == 0) for m in LEVEL_HALVES]
    return (jnp.asarray(tri.astype(np.float32), dtype=BF16),
            jnp.asarray(np.stack(masks).astype(np.float32)))


def _hgrn_prompt(q, lf, k, iv, gn, n_seq, seq_len):
    r = R_HGRN
    nblk = seq_len // r
    tri, masks = _hgrn_consts()
    blk = pl.BlockSpec((r, D_B), lambda b, j: (b * nblk + j, 0))
    return pl.pallas_call(
        _hgrn_prompt_body,
        grid=(n_seq, nblk),
        in_specs=[blk, blk, blk, blk,
                  pl.BlockSpec(tri.shape, lambda b, j: (0, 0)),
                  pl.BlockSpec(masks.shape, lambda b, j: (0, 0, 0)),
                  pl.BlockSpec((1, D_B), lambda b, j: (0, 0))],
        out_specs=[blk, pl.BlockSpec((1, H_B, DK, DK), lambda b, j: (b, 0, 0, 0))],
        out_shape=[jax.ShapeDtypeStruct((n_seq * seq_len, D_B), F32),
                   jax.ShapeDtypeStruct((n_seq, H_B, DK, DK), F32)],
        scratch_shapes=[pltpu.VMEM((H_B, DK, DK), F32),
                        pltpu.VMEM((r, D_B), F32)],
        compiler_params=_cparams(("arbitrary", "arbitrary")),
        name="hgrn_prompt",
    )(q, lf, k, iv, tri, masks, gn)


def _sub_chunk_exact(q_ref, k_ref, v_ref, g_ref, r0, cols, sub):
    qg = q_ref[pl.ds(r0, sub), cols]
    gg = g_ref[pl.ds(r0, sub), cols]
    kg = k_ref[pl.ds(r0, sub), cols]
    vg = v_ref[pl.ds(r0, sub), cols]
    row = lax.broadcasted_iota(jnp.int32, (sub, 1), 0)
    acc = jnp.zeros((sub, DK), F32)
    for s in range(sub):
        kb = kg[s:s + 1, :]
        gb = gg[s:s + 1, :]
        vb = vg[s:s + 1, :]
        p = qg * jnp.exp(gg - gb) * kb
        rs = jnp.sum(p, axis=-1, keepdims=True)
        acc = acc + jnp.where(row >= s, rs, 0.0) * vb
    return acc


def _hgrn_sample_body(q_ref, lf_ref, k_ref, v_ref, s0_ref, cm_ref, gn_ref, o_ref, s1_ref,
                      g_scr, o_scr, *, dec_seq):
    rows = q_ref.shape[0]
    n_seq = rows // dec_seq
    parts = _split(lf_ref[...], 3)

    def cum(idx):
        m = cm_ref[idx]
        return _dot(m, parts[0]) + _dot(m, parts[1]) + _dot(m, parts[2])

    g_all = cum(0)
    g_scr[...] = g_all
    g_rev = cum(1)
    g_tot = cum(2)
    lane = lax.broadcasted_iota(jnp.int32, (DK, rows), 1)

    for h in range(H_B):
        cols = slice(h * DK, (h + 1) * DK)
        qt = (q_ref[:, cols] * jnp.exp(g_all[:, cols])).astype(BF16)
        kt_t = (k_ref[:, cols] * jnp.exp(g_rev[:, cols])).T
        dec_t = jnp.exp(g_tot[:, cols]).T
        vb = v_ref[:, cols].astype(BF16)
        for n in range(n_seq):
            r0 = n * dec_seq
            s0 = s0_ref[0, n, h]
            o_scr[r0:r0 + dec_seq, cols] = _dot(qt[r0:r0 + dec_seq, :], s0.astype(BF16))
            in_seq = (lane >= r0) & (lane < r0 + dec_seq)
            u = _dot(jnp.where(in_seq, kt_t, 0.0).astype(BF16), vb)
            s1_ref[n, h] = s0 * dec_t[:, r0:r0 + 1] + u

    def group(n, carry):
        r0 = pl.multiple_of(n * dec_seq, dec_seq)
        for h in range(H_B):
            cols = slice(h * DK, (h + 1) * DK)
            o_scr[pl.ds(r0, dec_seq), cols] += _sub_chunk_exact(q_ref, k_ref, v_ref, g_scr, r0, cols,
                                                                 dec_seq)
        return carry

    lax.fori_loop(0, n_seq, group, 0)

    for h in range(H_B):
        cols = slice(h * DK, (h + 1) * DK)
        o_ref[:, cols] = _rms(o_scr[:, cols], gn_ref[:, cols])


def _hgrn_sample(q, lf, k, iv, state_in, layer, gn, row0, n_seq, dec_seq):
    rows = SEQ_PER_STEP * dec_seq
    i = np.arange(rows)[:, None]
    j = np.arange(rows)[None, :]
    same = (i // dec_seq) == (j // dec_seq)
    cm = jnp.asarray(np.stack([same & (j <= i), same & (j > i), same]).astype(np.float32), dtype=BF16)
    blk0 = row0 // rows
    blk = pl.BlockSpec((rows, D_B), lambda n: (blk0 + n, 0))
    oblk = pl.BlockSpec((rows, D_B), lambda n: (n, 0))
    return pl.pallas_call(
        functools.partial(_hgrn_sample_body, dec_seq=dec_seq),
        grid=(n_seq // SEQ_PER_STEP,),
        in_specs=[blk, blk, blk, blk,
                  pl.BlockSpec((1, SEQ_PER_STEP, H_B, DK, DK), lambda n: (layer, n, 0, 0, 0)),
                  pl.BlockSpec(cm.shape, lambda n: (0, 0, 0)),
                  pl.BlockSpec((1, D_B), lambda n: (0, 0))],
        out_specs=[oblk, pl.BlockSpec((SEQ_PER_STEP, H_B, DK, DK), lambda n: (n, 0, 0, 0))],
        out_shape=[jax.ShapeDtypeStruct((n_seq * dec_seq, D_B), F32),
                   jax.ShapeDtypeStruct((n_seq, H_B, DK, DK), F32)],
        scratch_shapes=[pltpu.VMEM((rows, D_B), F32), pltpu.VMEM((rows, D_B), F32)],
        compiler_params=_cparams(("arbitrary",)),
        name="hgrn_sample",
    )(q, lf, k, iv, state_in, cm, gn)


FF_SPLITS = ((0, 1536), (1536, 1280))


def _swiglu(hb, wg_ref, wu_ref, wd_ref, idx):
    acc = None
    for lo, n in FF_SPLITS:
        g = _dot(hb, wg_ref[idx + (slice(None), slice(lo, lo + n))])
        u = _dot(hb, wu_ref[idx + (slice(None), slice(lo, lo + n))])
        act = (g * _sigmoid(g) * u).astype(BF16)
        part = _dot(act, wd_ref[idx + (slice(lo, lo + n), slice(None))])
        acc = part if acc is None else acc + part
    return acc


def _route(hn, wh_ref, wl_ref, tok_ref, rank_t_ref, cnt_ref):
    tm = hn.shape[0]
    hh = hn.astype(BF16)
    hl = (hn - hh.astype(F32)).astype(BF16)
    logits = _dot(hh, wh_ref[...]) + _dot(hh, wl_ref[...]) + _dot(hl, wh_ref[...])
    lane = lax.broadcasted_iota(jnp.int32, logits.shape, 1).astype(F32)
    neg = np.float32(-np.inf)
    logits = jnp.where(lane < N_EXPERTS, logits, neg)
    m1 = jnp.max(logits, axis=-1, keepdims=True)
    i1 = jnp.min(jnp.where(logits == m1, lane, float(LANES)), axis=-1, keepdims=True)
    rest = jnp.where(lane == i1, neg, logits)
    m2 = jnp.max(rest, axis=-1, keepdims=True)
    i2 = jnp.min(jnp.where(rest == m2, lane, float(LANES)), axis=-1, keepdims=True)
    e2 = jnp.exp(m2 - m1)
    den = 1.0 + e2

    routed = (lane == i1) | (lane == i2)
    sel = jnp.where(routed, 1.0, 0.0)
    r = lax.broadcasted_iota(jnp.int32, (tm, tm), 0)
    c = lax.broadcasted_iota(jnp.int32, (tm, tm), 1)
    before = jnp.where(c < r, 1.0, 0.0).astype(BF16)
    rank = _dot(before, sel.astype(BF16))
    rank1 = jnp.sum(jnp.where(lane == i1, rank, 0.0), axis=-1, keepdims=True)
    rank2 = jnp.sum(jnp.where(lane == i2, rank, 0.0), axis=-1, keepdims=True)
    tok_ref[...] = jnp.where(lane == 0.0, i1 * KEY_MUL + rank1,
                             jnp.where(lane == 1.0, i2 * KEY_MUL + rank2,
                                       jnp.where(lane == 2.0, 1.0 / den,
                                                 jnp.where(lane == 3.0, e2 / den, 0.0))))
    rank_t = jnp.where(routed, rank, -1.0).T
    for e in range(N_EXPERTS):
        rank_t_ref[0, e] = rank_t[e:e + 1, :]
    cnt_ref[0] = jnp.sum(sel, axis=0, keepdims=True).astype(jnp.int32)


def _post_mix_body(*refs, n_h, n_prompt_tiles, mode):
    h_parts, refs = refs[:n_h], refs[n_h:]
    a_ref, op_ref, os_ref, gs_ref, sa_ref, sb_ref, wa_ref, wb_ref, wo_ref, gf_ref = refs[:10]
    o = _select_rows((op_ref, os_ref), n_prompt_tiles)
    pa = _dot(a_ref[...], wa_ref[...])
    pb = _dot((o * gs_ref[...].astype(F32)).astype(BF16), wb_ref[...])
    merged = sa_ref[...].astype(F32) * pa + sb_ref[...].astype(F32) * pb
    h1 = _select_rows(h_parts, n_prompt_tiles) + _dot(merged.astype(BF16), wo_ref[...])
    hn = _rms(h1, gf_ref[...])
    if mode == "dense":
        wg_ref, wu_ref, wd_ref = refs[10:13]
        n_cast = (len(refs) - 14) // 2
        cast_in, out_ref, cast_out = refs[13:13 + n_cast], refs[13 + n_cast], refs[14 + n_cast:]
        out_ref[...] = h1 + _swiglu(hn.astype(BF16), wg_ref, wu_ref, wd_ref, ())
        for src_ref, dst_ref in zip(cast_in, cast_out):
            dst_ref[...] = src_ref[...].astype(BF16)
    else:
        wh_ref, wl_ref, h1_ref, hn_ref, tok_ref, rank_t_ref, cnt_ref = refs[10:]
        h1_ref[...] = h1
        hn_ref[...] = hn.astype(BF16)
        _route(hn, wh_ref, wl_ref, tok_ref, rank_t_ref, cnt_ref)


def _post_mix(h_parts, a, o_p, o_s, gs, sa, sb, wa, wb, wo, gf, extra_w, t_prompt, mode, to_cast=()):
    t = a.shape[0]
    tm = TM_PROJ if mode == "dense" else TM_MOE
    n_prompt_tiles = t_prompt // tm
    assert not to_cast or t // tm >= CAST_STEPS
    tile = lambda n: pl.BlockSpec((tm, n), lambda i: (i, 0))
    cast_spec = lambda w: pl.BlockSpec((w.shape[0] // CAST_STEPS, w.shape[1]),
                                       lambda i: (jnp.minimum(i, CAST_STEPS - 1), 0))
    full = lambda w: pl.BlockSpec(w.shape, lambda i: (0,) * w.ndim, pipeline_mode=pl.Buffered(1))
    p_map, s_map = _split_rows(n_prompt_tiles)
    if len(h_parts) == 1:
        h_specs = [tile(D_MODEL)]
    else:
        h_specs = [pl.BlockSpec((tm, D_MODEL), p_map), pl.BlockSpec((tm, D_MODEL), s_map)]
    act = jax.ShapeDtypeStruct((t, D_MODEL), F32)
    if mode == "dense":
        out_specs = [tile(D_MODEL)] + [cast_spec(w) for w in to_cast]
        out_shape = [act] + [jax.ShapeDtypeStruct(w.shape, BF16) for w in to_cast]
        scratch = []
    else:
        out_specs = [tile(D_MODEL), tile(D_MODEL), tile(LANES),
                     pl.BlockSpec((1, N_EXPERTS, 1, tm), lambda i: (i, 0, 0, 0)),
                     pl.BlockSpec((1, 1, LANES), lambda i: (i, 0, 0))]
        out_shape = [act, jax.ShapeDtypeStruct((t, D_MODEL), BF16),
                     jax.ShapeDtypeStruct((t, LANES), F32),
                     jax.ShapeDtypeStruct((t // tm, N_EXPERTS, 1, tm), F32),
                     jax.ShapeDtypeStruct((t // tm, 1, LANES), jnp.int32)]
        scratch = []
    weights = (wa, wb, wo, gf) + tuple(extra_w)
    return pl.pallas_call(
        functools.partial(_post_mix_body, n_h=len(h_parts), n_prompt_tiles=n_prompt_tiles, mode=mode),
        grid=(t // tm,),
        in_specs=h_specs + [tile(D_A),
                            pl.BlockSpec((tm, D_B), p_map), pl.BlockSpec((tm, D_B), s_map),
                            tile(D_B), tile(D_MODEL), tile(D_MODEL)] + [full(w) for w in weights]
        + [cast_spec(w) for w in to_cast],
        out_specs=out_specs,
        out_shape=out_shape,
        scratch_shapes=scratch,
        compiler_params=_cparams(("arbitrary",)),
        name="post_mix_" + mode,
    )(*h_parts, a, o_p, o_s, gs, sa, sb, *weights, *to_cast)


def _chunk_copy(src_ref, src_row, dst_ref, dst_row, n, sem):
    return pltpu.make_async_copy(src_ref.at[pl.ds(src_row, n)], dst_ref.at[pl.ds(dst_row, n)], sem)


def _aligned(row):
    return pl.multiple_of(row, SUBLANES)


def _dispatch_body(ce_ref, cj_ref, crow_ref, tot_ref, zstart_ref, zrem_ref, tail_ref,
                   rank_t_ref, hn_ref, xs_ref, sel_scr, stage, zero_ref, sems, zsem):
    i = pl.program_id(0)
    tm = hn_ref.shape[0]
    par = i % 2

    def fills(fn):
        for e in range(N_EXPERTS):
            s = zstart_ref[e]
            fn(_chunk_copy(zero_ref, 0, xs_ref, _aligned(s), TMG, zsem))
            rem = zrem_ref[e]
            for b in ZERO_BITS:
                @pl.when((rem & b) != 0)
                def _():
                    fn(_chunk_copy(zero_ref, 0, xs_ref, _aligned(s + TMG + (rem & ~(2 * b - 1))), b, zsem))

        def tail(n, carry):
            fn(_chunk_copy(zero_ref, 0, xs_ref, _aligned(tail_ref[0] + n * TMG), TMG, zsem))
            return carry

        lax.fori_loop(0, tail_ref[1], tail, 0)

    @pl.when(i == 0)
    def _():
        zero_ref[...] = jnp.zeros_like(zero_ref)
        fills(lambda cp: cp.start())
        fills(lambda cp: cp.wait())

    row = lax.broadcasted_iota(jnp.int32, (CH, tm), 0).astype(F32)
    for s in range(N_SLOTS):
        e = ce_ref[i * N_SLOTS + s]
        first = (cj_ref[i * N_SLOTS + s] * CH).astype(F32)
        sel_scr[s * CH:(s + 1) * CH, :] = jnp.where(rank_t_ref[0, e] == row + first, 1.0, 0.0).astype(BF16)

    hb = hn_ref[...]
    for g in range(N_SLOTS // SLOTS_PER_DOT):
        rows = slice(g * SLOTS_PER_DOT * CH, (g + 1) * SLOTS_PER_DOT * CH)

        @pl.when(tot_ref[i] > g * SLOTS_PER_DOT)
        def _():
            stage[par, rows, :] = _dot(sel_scr[rows, :], hb)

    def chunk_copy(s, tile, buf):
        return _chunk_copy(stage.at[buf], s * CH, xs_ref, _aligned(crow_ref[tile * N_SLOTS + s]), CH,
                           sems.at[buf])

    @pl.when(i > 0)
    def _():
        for s in range(N_SLOTS):
            @pl.when(s < tot_ref[i - 1])
            def _():
                chunk_copy(s, i - 1, 1 - par).wait()

    for s in range(N_SLOTS):
        @pl.when(s < tot_ref[i])
        def _():
            chunk_copy(s, i, par).start()

    @pl.when(i == pl.num_programs(0) - 1)
    def _():
        for s in range(N_SLOTS):
            @pl.when(s < tot_ref[i])
            def _():
                chunk_copy(s, i, par).wait()


def _sorted_rows(t):
    n_seg = (t // TM_MOE) * N_EXPERTS
    rows = TOP_K * t + n_seg * (SUBLANES - 1) + N_EXPERTS * (TMG - 1) + N_EXPERTS * TMG
    return -(-rows // TMG) * TMG


def _dispatch(hn, rank_t, plan):
    t = hn.shape[0]
    tm = TM_MOE
    scalars = (plan["ce"], plan["cj"], plan["crow"], plan["tot"], plan["zstart"], plan["zrem"], plan["tail"])
    grid_spec = pltpu.PrefetchScalarGridSpec(
        num_scalar_prefetch=len(scalars),
        grid=(t // tm,),
        in_specs=[pl.BlockSpec((1, N_EXPERTS, 1, tm), lambda i, *_: (i, 0, 0, 0)),
                  pl.BlockSpec((tm, D_MODEL), lambda i, *_: (i, 0))],
        out_specs=pl.BlockSpec(memory_space=pl.ANY),
        scratch_shapes=[pltpu.VMEM((N_SLOTS * CH, tm), BF16),
                        pltpu.VMEM((2, N_SLOTS * CH, D_MODEL), F32),
                        pltpu.VMEM((TMG, D_MODEL), F32),
                        pltpu.SemaphoreType.DMA((2,)), pltpu.SemaphoreType.DMA],
    )
    return pl.pallas_call(
        _dispatch_body,
        grid_spec=grid_spec,
        out_shape=jax.ShapeDtypeStruct((_sorted_rows(t), D_MODEL), F32),
        compiler_params=_cparams(("arbitrary",)),
        name="moe_dispatch",
    )(*scalars, rank_t, hn)


def _ffn_grouped_body(te_ref, live_ref, xs_ref, wg_ref, wu_ref, wd_ref, ys_ref):
    j = pl.program_id(0)

    @pl.when(live_ref[j] != 0)
    def _():
        ys_ref[...] = _swiglu(xs_ref[...].astype(BF16), wg_ref, wu_ref, wd_ref, (0,))

    @pl.when(live_ref[j] == 0)
    def _():
        ys_ref[...] = jnp.zeros_like(ys_ref)


def _ffn_grouped(xs, tile_expert, tile_live, wg, wu, wd):
    n_rows = xs.shape[0]
    row_tile = lambda j, te, live: (j, 0)
    expert = lambda j, te, live: (te[j], 0, 0)
    grid_spec = pltpu.PrefetchScalarGridSpec(
        num_scalar_prefetch=2,
        grid=(n_rows // TMG,),
        in_specs=[pl.BlockSpec((TMG, D_MODEL), row_tile),
                  pl.BlockSpec((1, D_MODEL, D_FF), expert),
                  pl.BlockSpec((1, D_MODEL, D_FF), expert),
                  pl.BlockSpec((1, D_FF, D_MODEL), expert)],
        out_specs=pl.BlockSpec((TMG, D_MODEL), row_tile),
    )
    return pl.pallas_call(
        _ffn_grouped_body,
        grid_spec=grid_spec,
        out_shape=jax.ShapeDtypeStruct((n_rows, D_MODEL), F32),
        compiler_params=_cparams(("arbitrary",)),
        name="moe_ffn",
    )(tile_expert, tile_live, xs, wg, wu, wd)


def _combine_body(ce_ref, cj_ref, crow_ref, tot_ref, tok_ref, h1_ref, gfin_ref, ys_ref,
                  yp_ref, ysm_ref, sel_scr, stage, sems, *, n_prompt_tiles):
    i = pl.program_id(0)
    tm = h1_ref.shape[0]
    par = i % 2

    def fetch(s, tile, buf):
        return _chunk_copy(ys_ref, _aligned(crow_ref[tile * N_SLOTS + s]), stage.at[buf], s * CH, CH,
                           sems.at[buf])

    def fetches(tile, buf, fn):
        for s in range(N_SLOTS):
            @pl.when(s < tot_ref[tile])
            def _():
                fn(fetch(s, tile, buf))

    @pl.when(i == 0)
    def _():
        stage[...] = jnp.zeros_like(stage)
        fetches(0, 0, lambda cp: cp.start())

    @pl.when(i + 1 < pl.num_programs(0))
    def _():
        fetches(i + 1, 1 - par, lambda cp: cp.start())

    tok = tok_ref[...]
    key1, key2 = tok[:, 0:1], tok[:, 1:2]
    gate1, gate2 = tok[:, 2:3], tok[:, 3:4]
    lane = lax.broadcasted_iota(jnp.int32, (tm, CH), 1).astype(F32)
    for s in range(N_SLOTS):
        first = (ce_ref[i * N_SLOTS + s] * KEY_MUL + cj_ref[i * N_SLOTS + s] * CH).astype(F32)
        key = lane + first
        sel_scr[:, s * CH:(s + 1) * CH] = jnp.where(key1 == key, gate1,
                                                    jnp.where(key2 == key, gate2, 0.0)).astype(BF16)

    fetches(i, par, lambda cp: cp.wait())
    val = h1_ref[...] + _dot(sel_scr[...], stage[par].astype(BF16))
    y = _rms(val, gfin_ref[...])

    @pl.when(i < n_prompt_tiles)
    def _():
        yp_ref[...] = y

    @pl.when(i >= n_prompt_tiles)
    def _():
        ysm_ref[...] = y


def _combine(ys, tok, h1, gfin, plan, t_prompt):
    t = h1.shape[0]
    tm = TM_MOE
    n_prompt_tiles = t_prompt // tm
    p_map, s_map = _split_rows(n_prompt_tiles)
    scalars = (plan["ce"], plan["cj"], plan["crow"], plan["tot"])
    grid_spec = pltpu.PrefetchScalarGridSpec(
        num_scalar_prefetch=len(scalars),
        grid=(t // tm,),
        in_specs=[pl.BlockSpec((tm, LANES), lambda i, *_: (i, 0)),
                  pl.BlockSpec((tm, D_MODEL), lambda i, *_: (i, 0)),
                  pl.BlockSpec((1, D_MODEL), lambda i, *_: (0, 0)),
                  pl.BlockSpec(memory_space=pl.ANY)],
        out_specs=[pl.BlockSpec((tm, D_MODEL), lambda i, *_: p_map(i)),
                   pl.BlockSpec((tm, D_MODEL), lambda i, *_: s_map(i))],
        scratch_shapes=[pltpu.VMEM((tm, N_SLOTS * CH), BF16),
                        pltpu.VMEM((2, N_SLOTS * CH, D_MODEL), F32),
                        pltpu.SemaphoreType.DMA((2,))],
    )
    return pl.pallas_call(
        functools.partial(_combine_body, n_prompt_tiles=n_prompt_tiles),
        grid_spec=grid_spec,
        out_shape=[jax.ShapeDtypeStruct((t_prompt, D_MODEL), F32),
                   jax.ShapeDtypeStruct((t - t_prompt, D_MODEL), F32)],
        compiler_params=_cparams(("arbitrary",)),
        name="moe_combine",
    )(*scalars, tok, h1, gfin, ys)


def _moe_plan(cnt, n_rows):
    i32 = jnp.int32
    seg = (cnt + SUBLANES - 1) // SUBLANES * SUBLANES
    g_len = jnp.sum(seg, axis=0)
    g_pad = (g_len + TMG - 1) // TMG * TMG
    g_span = g_pad + TMG
    g_end = jnp.cumsum(g_span)
    g_start = g_end - g_span
    seg_start = g_start[None, :] + jnp.cumsum(seg, axis=0) - seg
    n_chunks = (cnt + CH - 1) // CH
    c_end = jnp.cumsum(n_chunks, axis=1)
    tot = c_end[:, -1]
    slot = jnp.arange(N_SLOTS, dtype=i32)[None, :]
    live = slot < tot[:, None]
    ce = jnp.minimum(jnp.sum(slot[:, :, None] >= c_end[:, None, :], axis=-1), N_EXPERTS - 1)
    cj = slot - jnp.take_along_axis(c_end - n_chunks, ce, axis=1)
    cj = jnp.where(live, cj, CJ_NONE)
    crow = jnp.where(live, jnp.take_along_axis(seg_start, ce, axis=1) + CH * cj, 0)
    tile_row = jnp.arange(n_rows // TMG, dtype=i32) * TMG
    tile_expert = jnp.minimum(jnp.sum(tile_row[:, None] >= g_end[None, :], axis=1), N_EXPERTS - 1)
    tile_live = tile_row < (g_start + g_pad)[tile_expert]
    flat = lambda x: x.reshape(-1).astype(i32)
    return dict(ce=flat(ce), cj=flat(cj), crow=flat(crow), tot=flat(tot),
                zstart=flat(g_start + g_len), zrem=flat(g_pad - g_len),
                tail=jnp.stack([g_end[-1], (n_rows - g_end[-1]) // TMG]).astype(i32),
                tile_expert=flat(tile_expert), tile_live=flat(tile_live))


def _moe_routed(h1, hn, tok, rank_t, cnt, wg, wu, wd, gfin, t_prompt):
    plan = _moe_plan(cnt[:, 0, :N_EXPERTS], _sorted_rows(h1.shape[0]))
    xs = _dispatch(hn, rank_t, plan)
    ys = _ffn_grouped(xs, plan["tile_expert"], plan["tile_live"], wg, wu, wd)
    return _combine(ys, tok, h1, gfin, plan, t_prompt)


def _mix_consts(w_spatial, b_spatial, dec_seq):
    tril = jnp.tril(jnp.ones((CHUNK, CHUNK), F32))
    w_p = w_spatial * tril
    reps = CHUNK // dec_seq
    w_s = jnp.stack([jnp.kron(jnp.eye(reps, dtype=F32), w_p[g, :dec_seq, :dec_seq]) for g in range(G_A)])
    wmix = jnp.stack([w_p, w_s]).astype(BF16)
    b_p = jnp.repeat(b_spatial.T, CH_A, axis=1)
    b_s = jnp.tile(b_p[:dec_seq], (reps, 1))
    return wmix, jnp.stack([b_p, b_s])


def kernel(x_prompt, x_sample, state_hgrn, norm_mix_g, w_in, ln_v_g, ln_v_b, w_spatial, b_spatial,
           lower_bounds, hgrn_norm_g, w_branch_a, w_branch_b, w_out, norm_ffn_g, dense_w_gate,
           dense_w_up, dense_w_down, router_w, moe_w_gate, moe_w_up, moe_w_down, final_norm_g):
    n_seq, seq_len, _ = x_prompt.shape
    dec_batch, dec_seq, _ = x_sample.shape
    t_prompt = n_seq * seq_len
    t_sample = dec_batch * dec_seq
    t = t_prompt + t_sample

    lb_cum = jnp.cumsum(jax.nn.softmax(lower_bounds.astype(F32), axis=0), axis=0)
    lb_all = lb_cum - lb_cum[0:1]
    log_lb, log1m_lb, one_m_lb = jnp.log(lb_all), jnp.log1p(-lb_all), 1.0 - lb_all

    row = lambda p: p.reshape(1, -1)
    h_parts = (x_prompt.reshape(t_prompt, D_MODEL), x_sample.reshape(t_sample, D_MODEL))
    state_in = state_hgrn.astype(F32)
    states_p, states_s, v_rows = [], [], []
    for l in range(DEPTH):
        wmix, bmix = _mix_consts(w_spatial[l], b_spatial[l], dec_seq)
        a, v_s, q, lf, k, iv, gs, sa, sb = _in_proj(
            h_parts, row(norm_mix_g[l]), w_in[l].astype(BF16), row(ln_v_g[l]), row(ln_v_b[l]),
            row(log_lb[l]), row(log1m_lb[l]), row(one_m_lb[l]), wmix, bmix, t, t_prompt)
        gn = row(hgrn_norm_g[l])
        o_p, s_p = _hgrn_prompt(q, lf, k, iv, gn, n_seq, seq_len)
        o_s, s_s = _hgrn_sample(q, lf, k, iv, state_in, l, gn, t_prompt, dec_batch, dec_seq)
        mix_w = (w_branch_a[l].astype(BF16), w_branch_b[l].astype(BF16), w_out[l].astype(BF16),
                 row(norm_ffn_g[l]))
        i = l // 2
        if l % 2 == 0:
            ffn_w = (dense_w_gate[i].astype(BF16), dense_w_up[i].astype(BF16), dense_w_down[i].astype(BF16))
            moe_f32 = (moe_w_gate[i].reshape(-1, D_FF), moe_w_up[i].reshape(-1, D_FF),
                       moe_w_down[i].reshape(-1, D_MODEL))
            h_next, *moe_bf16 = _post_mix(h_parts, a, o_p, o_s, gs, sa, sb, *mix_w, ffn_w, t_prompt,
                                          "dense", moe_f32)
            h_parts = (h_next,)
        else:
            rw = jnp.pad(router_w[i], ((0, 0), (0, LANES - N_EXPERTS)))
            rw_hi = rw.astype(BF16)
            rw_lo = (rw - rw_hi.astype(F32)).astype(BF16)
            h1, hn, tok, rank_t, cnt = _post_mix(h_parts, a, o_p, o_s, gs, sa, sb, *mix_w,
                                                 (rw_hi, rw_lo), t_prompt, "router")
            h_parts = _moe_routed(h1, hn, tok, rank_t, cnt, moe_bf16[0].reshape(moe_w_gate[i].shape),
                                  moe_bf16[1].reshape(moe_w_up[i].shape),
                                  moe_bf16[2].reshape(moe_w_down[i].shape), row(final_norm_g), t_prompt)
        states_p.append(s_p)
        states_s.append(s_s)
        v_rows.append(v_s.reshape(dec_batch, dec_seq, D_A))

    y_prompt = h_parts[0].reshape(n_seq, seq_len, D_MODEL)
    y_sample = h_parts[1].reshape(dec_batch, dec_seq, D_MODEL)
    return (y_prompt, y_sample, jnp.stack(states_p).astype(x_prompt.dtype),
            jnp.stack(states_s).astype(state_hgrn.dtype), jnp.stack(v_rows))
```

```python
import functools

import numpy as np
import jax
import jax.numpy as jnp
from jax import lax
from jax.experimental import pallas as pl
from jax.experimental.pallas import tpu as pltpu

F32 = jnp.float32
BF16 = jnp.bfloat16

D_MODEL = 1024
DEPTH = 2
D_A = 512
G_A = 4
CH_A = 128
CHUNK = 128
D_B = 512
H_B = 4
DK = 128
IN_COLS = 5120
D_FF = 2816
N_EXPERTS = 8
TOP_K = 2
EPS = 1e-6

LANES = 128
SUBLANES = 8
VMEM_LIMIT = 52 * 1024 * 1024

TM_IN = 256
TM_PROJ = 256
CAST_STEPS = 64
TM_MOE = 512
TMG = 256
CH = 128
N_SLOTS = -(-(TOP_K * TM_MOE + N_EXPERTS * (CH - 1)) // CH)
SLOTS_PER_DOT = 4
KEY_MUL = 4096
CJ_NONE = N_SLOTS
ZERO_BITS = (128, 64, 32, 16, 8)
R_HGRN = 256
BLK = 128
LEVEL_HALVES = (64, 32, 16, 8)
SEQ_PER_STEP = 16


def _cparams(sem):
    return pltpu.CompilerParams(dimension_semantics=sem, vmem_limit_bytes=VMEM_LIMIT)


def _dot(a, b):
    return jnp.dot(a, b, preferred_element_type=F32)


def _dot_nt(a, b):
    return lax.dot_general(a, b, (((1,), (1,)), ((), ())), preferred_element_type=F32)


def _split(x, terms):
    out = []
    for _ in range(terms - 1):
        hi = x.astype(BF16)
        out.append(hi)
        x = x - hi.astype(F32)
    out.append(x.astype(BF16))
    return out


def _rms(x, g):
    return x * lax.rsqrt(jnp.mean(x * x, axis=-1, keepdims=True) + EPS) * g


def _gelu(x):
    return 0.5 * x * (1.0 + lax.erf(x * np.float32(2.0 ** -0.5)))


def _sigmoid(x):
    return jax.nn.sigmoid(x)


def _split_rows(n_prompt_tiles):
    return (lambda i: (jnp.minimum(i, n_prompt_tiles - 1), 0),
            lambda i: (jnp.maximum(i - n_prompt_tiles, 0), 0))


def _select_rows(parts, n_prompt_tiles):
    if len(parts) == 1:
        return parts[0][...]
    return jnp.where(pl.program_id(0) < n_prompt_tiles, parts[0][...], parts[1][...])


def _softplus_neg_abs(x):
    return jnp.log(1.0 + jnp.exp(-jnp.abs(x)))


def _in_proj_body(*refs, n_x, n_prompt_tiles):
    x_parts, refs = refs[:n_x], refs[n_x:]
    (g_ref, w_ref, lng_ref, lnb_ref, la_ref, l1m_ref, oml_ref, wmix_ref, bmix_ref,
     a_ref, vs_ref, q_ref, lf_ref, k_ref, iv_ref, gs_ref, sa_ref, sb_ref) = refs
    tm = a_ref.shape[0]
    xn = _rms(_select_rows(x_parts, n_prompt_tiles), g_ref[...]).astype(BF16)

    def seg(lo, n):
        return _dot(xn, w_ref[:, lo:lo + n])

    u = _gelu(seg(0, D_A))
    iv_ref[...] = seg(2 * D_A + 2 * D_B, D_B)
    v = _gelu(seg(D_A, D_A))
    vc = v - jnp.mean(v, axis=-1, keepdims=True)
    v = vc * lax.rsqrt(jnp.mean(vc * vc, axis=-1, keepdims=True) + EPS) * lng_ref[...] + lnb_ref[...]
    vs_ref[...] = v
    gate_a, gate_b = 2 * D_A + 4 * D_B, 2 * D_A + 4 * D_B + D_MODEL
    half = D_MODEL // 2
    sa_ref[:, :half] = _sigmoid(seg(gate_a, half)).astype(BF16)

    vb = v.astype(BF16)
    for c in range(tm // CHUNK):
        rows = slice(c * CHUNK, (c + 1) * CHUNK)
        for g in range(G_A):
            cols = slice(g * CH_A, (g + 1) * CH_A)
            mixed = _dot(wmix_ref[0, g], vb[rows, cols]) + bmix_ref[0, :, cols]
            a_ref[rows, cols] = (u[rows, cols] * mixed).astype(BF16)

    fz = seg(2 * D_A + D_B, D_B)
    log_sig = jnp.minimum(fz, 0.0) - _softplus_neg_abs(fz)
    b = l1m_ref[...] + log_sig
    la = la_ref[...]
    lf_ref[...] = jnp.maximum(la, b) + _softplus_neg_abs(la - b)
    k_ref[...] = oml_ref[...] * _sigmoid(-fz)
    sa_ref[:, half:] = _sigmoid(seg(gate_a + half, half)).astype(BF16)
    zq = seg(2 * D_A, D_B)
    q_ref[...] = zq * _sigmoid(zq)
    sb_ref[:, :half] = _sigmoid(seg(gate_b, half)).astype(BF16)
    gz = seg(2 * D_A + 3 * D_B, D_B)
    gs_ref[...] = (gz * _sigmoid(gz)).astype(BF16)
    sb_ref[:, half:] = _sigmoid(seg(gate_b + half, half)).astype(BF16)


def _in_proj(x_parts, g, w, lng, lnb, la, l1m, oml, wmix, bmix, t, t_prompt):
    tm = TM_IN
    n_prompt_tiles = t_prompt // tm
    row = lambda n: pl.BlockSpec((1, n), lambda i: (0, 0))
    tile = lambda n: pl.BlockSpec((tm, n), lambda i: (i, 0))
    sel = lambda i: jnp.minimum(i // n_prompt_tiles, 1)
    p_map, s_map = _split_rows(n_prompt_tiles)
    if len(x_parts) == 1:
        x_specs = [tile(D_MODEL)]
    else:
        x_specs = [pl.BlockSpec((tm, D_MODEL), p_map), pl.BlockSpec((tm, D_MODEL), s_map)]
    t_sample = t - n_prompt_tiles * tm
    shape = lambda rows, n, dt=F32: jax.ShapeDtypeStruct((rows, n), dt)
    out_shapes = ([shape(t, D_A, BF16), shape(t_sample, D_A)] + [shape(t, D_A)] * 4
                  + [shape(t, D_A, BF16)] + [shape(t, D_MODEL, BF16)] * 2)
    out_specs = ([tile(D_A), pl.BlockSpec((tm, D_A), s_map)] + [tile(D_A)] * 5 + [tile(D_MODEL)] * 2)
    return pl.pallas_call(
        functools.partial(_in_proj_body, n_x=len(x_parts), n_prompt_tiles=n_prompt_tiles),
        grid=(t // tm,),
        in_specs=x_specs + [row(D_MODEL),
                            pl.BlockSpec((D_MODEL, IN_COLS), lambda i: (0, 0), pipeline_mode=pl.Buffered(1)),
                            row(D_A), row(D_A), row(D_B), row(D_B), row(D_B),
                            pl.BlockSpec((1, G_A, CHUNK, CHUNK), lambda i: (sel(i), 0, 0, 0)),
                            pl.BlockSpec((1, CHUNK, D_A), lambda i: (sel(i), 0, 0))],
        out_specs=out_specs,
        out_shape=out_shapes,
        compiler_params=_cparams(("arbitrary",)),
        name="in_proj",
    )(*x_parts, g, w, lng, lnb, la, l1m, oml, wmix, bmix)


def _hgrn_block(q_ref, k_ref, v_ref, g_scr, masks_ref, st, r0, cols, lane, row):
    rows = slice(r0, r0 + BLK)
    qh, kh, vh = q_ref[rows, cols], k_ref[rows, cols], v_ref[rows, cols]
    gh = g_scr[rows, cols]
    g_tot = g_scr[r0 + BLK - 1:r0 + BLK, cols]

    o = _dot_nt((qh * jnp.exp(gh)).astype(BF16), st.astype(BF16))

    sc = None
    for l, m in enumerate(LEVEL_HALVES):
        refs = [jnp.broadcast_to(g_scr[r0 + b + m - 1:r0 + b + m, cols], (2 * m, DK))
                for b in range(0, BLK, 2 * m)]
        g_ref_rows = refs[0] if len(refs) == 1 else jnp.concatenate(refs, axis=0)
        right = (row & m) != 0
        x = jnp.where(right, gh - g_ref_rows, g_ref_rows - gh)
        z = (jnp.where(right, qh, kh) * jnp.exp(x)).astype(BF16)
        part = _dot_nt(z, z) * masks_ref[l]
        sc = part if sc is None else sc + part

    strips = []
    for g0 in range(0, BLK, SUBLANES):
        qg, gg = qh[g0:g0 + SUBLANES], gh[g0:g0 + SUBLANES]
        strip = jnp.zeros((SUBLANES, BLK), F32)
        for s in range(SUBLANES):
            src = r0 + g0 + s
            p = qg * jnp.exp(gg - g_scr[src:src + 1, cols]) * k_ref[src:src + 1, cols]
            strip = jnp.where(lane[:SUBLANES] == g0 + s, jnp.sum(p, axis=-1, keepdims=True), strip)
        strips.append(strip)
    diag = jnp.concatenate(strips, axis=0)
    sc = sc + jnp.where(lane <= row, diag, 0.0)

    o = o + _dot(sc.astype(BF16), vh.astype(BF16))
    kt = (kh * jnp.exp(g_tot - gh)).astype(BF16)
    st_new = st * jnp.exp(g_tot) + _dot(vh.T.astype(BF16), kt)
    return o, st_new


def _hgrn_prompt_body(q_ref, lf_ref, k_ref, v_ref, tri_ref, masks_ref, gn_ref, o_ref, sfin_ref,
                      st_ref, g_scr):
    j = pl.program_id(1)
    r = q_ref.shape[0]

    @pl.when(j == 0)
    def _():
        st_ref[...] = jnp.zeros_like(st_ref)

    tri = tri_ref[...]
    g_scr[...] = sum(_dot(tri, part) for part in _split(lf_ref[...], 2))
    lane = lax.broadcasted_iota(jnp.int32, (BLK, BLK), 1)
    row = lax.broadcasted_iota(jnp.int32, (BLK, BLK), 0)

    for h in range(H_B):
        cols = slice(h * DK, (h + 1) * DK)
        st = st_ref[h]
        for r0 in range(0, r, BLK):
            o, st = _hgrn_block(q_ref, k_ref, v_ref, g_scr, masks_ref, st, r0, cols, lane, row)
            o_ref[r0:r0 + BLK, cols] = _rms(o, gn_ref[:, cols])
        st_ref[h] = st

    @pl.when(j == pl.num_programs(1) - 1)
    def _():
        for h in range(H_B):
            sfin_ref[0, h] = st_ref[h].T


def _hgrn_consts():
    i = np.arange(R_HGRN)[:, None]
    j = np.arange(R_HGRN)[None, :]
    tri = ((i // BLK) == (j // BLK)) & (j <= i)
    i, j = np.arange(BLK)[:, None], np.arange(BLK)[None, :]
    masks = [((i // (2 * m)) == (j // (2 * m))) & ((i & m) != 0) & ((j & m) == 0) for m in LEVEL_HALVES]
    return (jnp.asarray(tri.astype(np.float32), dtype=BF16),
            jnp.asarray(np.stack(masks).astype(np.float32)))


def _hgrn_prompt(q, lf, k, iv, gn, n_seq, seq_len):
    r = R_HGRN
    nblk = seq_len // r
    tri, masks = _hgrn_consts()
    blk = pl.BlockSpec((r, D_B), lambda b, j: (b * nblk + j, 0))
    return pl.pallas_call(
        _hgrn_prompt_body,
        grid=(n_seq, nblk),
        in_specs=[blk, blk, blk, blk,
                  pl.BlockSpec(tri.shape, lambda b, j: (0, 0)),
                  pl.BlockSpec(masks.shape, lambda b, j: (0, 0, 0)),
                  pl.BlockSpec((1, D_B), lambda b, j: (0, 0))],
        out_specs=[blk, pl.BlockSpec((1, H_B, DK, DK), lambda b, j: (b, 0, 0, 0))],
        out_shape=[jax.ShapeDtypeStruct((n_seq * seq_len, D_B), F32),
                   jax.ShapeDtypeStruct((n_seq, H_B, DK, DK), F32)],
        scratch_shapes=[pltpu.VMEM((H_B, DK, DK), F32),
                        pltpu.VMEM((r, D_B), F32)],
        compiler_params=_cparams(("arbitrary", "arbitrary")),
        name="hgrn_prompt",
    )(q, lf, k, iv, tri, masks, gn)


def _sub_chunk_exact(q_ref, k_ref, v_ref, g_ref, r0, cols, sub):
    qg = q_ref[pl.ds(r0, sub), cols]
    gg = g_ref[pl.ds(r0, sub), cols]
    kg = k_ref[pl.ds(r0, sub), cols]
    vg = v_ref[pl.ds(r0, sub), cols]
    row = lax.broadcasted_iota(jnp.int32, (sub, 1), 0)
    acc = jnp.zeros((sub, DK), F32)
    for s in range(sub):
        kb = kg[s:s + 1, :]
        gb = gg[s:s + 1, :]
        vb = vg[s:s + 1, :]
        p = qg * jnp.exp(gg - gb) * kb
        rs = jnp.sum(p, axis=-1, keepdims=True)
        acc = acc + jnp.where(row >= s, rs, 0.0) * vb
    return acc


def _hgrn_sample_body(q_ref, lf_ref, k_ref, v_ref, s0_ref, cm_ref, gn_ref, o_ref, s1_ref,
                      g_scr, o_scr, *, dec_seq):
    rows = q_ref.shape[0]
    n_seq = rows // dec_seq
    parts = _split(lf_ref[...], 3)

    def cum(idx):
        m = cm_ref[idx]
        return _dot(m, parts[0]) + _dot(m, parts[1]) + _dot(m, parts[2])

    g_all = cum(0)
    g_scr[...] = g_all
    g_rev = cum(1)
    g_tot = cum(2)
    lane = lax.broadcasted_iota(jnp.int32, (DK, rows), 1)

    for h in range(H_B):
        cols = slice(h * DK, (h + 1) * DK)
        qt = (q_ref[:, cols] * jnp.exp(g_all[:, cols])).astype(BF16)
        kt_t = (k_ref[:, cols] * jnp.exp(g_rev[:, cols])).T
        dec_t = jnp.exp(g_tot[:, cols]).T
        vb = v_ref[:, cols].astype(BF16)
        for n in range(n_seq):
            r0 = n * dec_seq
            s0 = s0_ref[0, n, h]
            o_scr[r0:r0 + dec_seq, cols] = _dot(qt[r0:r0 + dec_seq, :], s0.astype(BF16))
            in_seq = (lane >= r0) & (lane < r0 + dec_seq)
            u = _dot(jnp.where(in_seq, kt_t, 0.0).astype(BF16), vb)
            s1_ref[n, h] = s0 * dec_t[:, r0:r0 + 1] + u

    def group(n, carry):
        r0 = pl.multiple_of(n * dec_seq, dec_seq)
        for h in range(H_B):
            cols = slice(h * DK, (h + 1) * DK)
            o_scr[pl.ds(r0, dec_seq), cols] += _sub_chunk_exact(q_ref, k_ref, v_ref, g_scr, r0, cols,
                                                                 dec_seq)
        return carry

    lax.fori_loop(0, n_seq, group, 0)

    for h in range(H_B):
        cols = slice(h * DK, (h + 1) * DK)
        o_ref[:, cols] = _rms(o_scr[:, cols], gn_ref[:, cols])


def _hgrn_sample(q, lf, k, iv, state_in, layer, gn, row0, n_seq, dec_seq):
    rows = SEQ_PER_STEP * dec_seq
    i = np.arange(rows)[:, None]
    j = np.arange(rows)[None, :]
    same = (i // dec_seq) == (j // dec_seq)
    cm = jnp.asarray(np.stack([same & (j <= i), same & (j > i), same]).astype(np.float32), dtype=BF16)
    blk0 = row0 // rows
    blk = pl.BlockSpec((rows, D_B), lambda n: (blk0 + n, 0))
    oblk = pl.BlockSpec((rows, D_B), lambda n: (n, 0))
    return pl.pallas_call(
        functools.partial(_hgrn_sample_body, dec_seq=dec_seq),
        grid=(n_seq // SEQ_PER_STEP,),
        in_specs=[blk, blk, blk, blk,
                  pl.BlockSpec((1, SEQ_PER_STEP, H_B, DK, DK), lambda n: (layer, n, 0, 0, 0)),
                  pl.BlockSpec(cm.shape, lambda n: (0, 0, 0)),
                  pl.BlockSpec((1, D_B), lambda n: (0, 0))],
        out_specs=[oblk, pl.BlockSpec((SEQ_PER_STEP, H_B, DK, DK), lambda n: (n, 0, 0, 0))],
        out_shape=[jax.ShapeDtypeStruct((n_seq * dec_seq, D_B), F32),
                   jax.ShapeDtypeStruct((n_seq, H_B, DK, DK), F32)],
        scratch_shapes=[pltpu.VMEM((rows, D_B), F32), pltpu.VMEM((rows, D_B), F32)],
        compiler_params=_cparams(("arbitrary",)),
        name="hgrn_sample",
    )(q, lf, k, iv, state_in, cm, gn)


FF_SPLITS = ((0, 1536), (1536, 1280))


def _swiglu(hb, wg_ref, wu_ref, wd_ref, idx):
    acc = None
    for lo, n in FF_SPLITS:
        g = _dot(hb, wg_ref[idx + (slice(None), slice(lo, lo + n))])
        u = _dot(hb, wu_ref[idx + (slice(None), slice(lo, lo + n))])
        act = (g * _sigmoid(g) * u).astype(BF16)
        part = _dot(act, wd_ref[idx + (slice(lo, lo + n), slice(None))])
        acc = part if acc is None else acc + part
    return acc


def _route(hn, wh_ref, wl_ref, tok_ref, rank_t_ref, cnt_ref):
    tm = hn.shape[0]
    hh = hn.astype(BF16)
    hl = (hn - hh.astype(F32)).astype(BF16)
    logits = _dot(hh, wh_ref[...]) + _dot(hh, wl_ref[...]) + _dot(hl, wh_ref[...])
    lane = lax.broadcasted_iota(jnp.int32, logits.shape, 1).astype(F32)
    neg = np.float32(-np.inf)
    logits = jnp.where(lane < N_EXPERTS, logits, neg)
    m1 = jnp.max(logits, axis=-1, keepdims=True)
    i1 = jnp.min(jnp.where(logits == m1, lane, float(LANES)), axis=-1, keepdims=True)
    rest = jnp.where(lane == i1, neg, logits)
    m2 = jnp.max(rest, axis=-1, keepdims=True)
    i2 = jnp.min(jnp.where(rest == m2, lane, float(LANES)), axis=-1, keepdims=True)
    e2 = jnp.exp(m2 - m1)
    den = 1.0 + e2

    routed = (lane == i1) | (lane == i2)
    sel = jnp.where(routed, 1.0, 0.0)
    r = lax.broadcasted_iota(jnp.int32, (tm, tm), 0)
    c = lax.broadcasted_iota(jnp.int32, (tm, tm), 1)
    before = jnp.where(c < r, 1.0, 0.0).astype(BF16)
    rank = _dot(before, sel.astype(BF16))
    rank1 = jnp.sum(jnp.where(lane == i1, rank, 0.0), axis=-1, keepdims=True)
    rank2 = jnp.sum(jnp.where(lane == i2, rank, 0.0), axis=-1, keepdims=True)
    tok_ref[...] = jnp.where(lane == 0.0, i1 * KEY_MUL + rank1,
                             jnp.where(lane == 1.0, i2 * KEY_MUL + rank2,
                                       jnp.where(lane == 2.0, 1.0 / den,
                                                 jnp.where(lane == 3.0, e2 / den, 0.0))))
    rank_t = jnp.where(routed, rank, -1.0).T
    for e in range(N_EXPERTS):
        rank_t_ref[0, e] = rank_t[e:e + 1, :]
    cnt_ref[0] = jnp.sum(sel, axis=0, keepdims=True).astype(jnp.int32)


def _post_mix_body(*refs, n_h, n_prompt_tiles, mode):
    h_parts, refs = refs[:n_h], refs[n_h:]
    a_ref, op_ref, os_ref, gs_ref, sa_ref, sb_ref, wa_ref, wb_ref, wo_ref, gf_ref = refs[:10]
    tm = a_ref.shape[0]
    h1s, hns = [], []
    for r0 in range(0, tm, TM_PROJ):
        rows = slice(r0, r0 + TM_PROJ)
        o = _select_rows((op_ref.at[rows], os_ref.at[rows]), n_prompt_tiles)
        pa = _dot(a_ref[rows, :], wa_ref[...])
        pb = _dot((o * gs_ref[rows, :].astype(F32)).astype(BF16), wb_ref[...])
        merged = sa_ref[rows, :].astype(F32) * pa + sb_ref[rows, :].astype(F32) * pb
        h1s.append(_select_rows([p.at[rows] for p in h_parts], n_prompt_tiles)
                   + _dot(merged.astype(BF16), wo_ref[...]))
        hns.append(_rms(h1s[-1], gf_ref[...]))
    h1 = h1s[0] if len(h1s) == 1 else jnp.concatenate(h1s, axis=0)
    hn = hns[0] if len(hns) == 1 else jnp.concatenate(hns, axis=0)
    if mode == "dense":
        wg_ref, wu_ref, wd_ref = refs[10:13]
        n_cast = (len(refs) - 14) // 2
        cast_in, out_ref, cast_out = refs[13:13 + n_cast], refs[13 + n_cast], refs[14 + n_cast:]
        out_ref[...] = h1 + _swiglu(hn.astype(BF16), wg_ref, wu_ref, wd_ref, ())
        for src_ref, dst_ref in zip(cast_in, cast_out):
            dst_ref[...] = src_ref[...].astype(BF16)
    else:
        wh_ref, wl_ref, h1_ref, hn_ref, tok_ref, rank_t_ref, cnt_ref = refs[10:]
        h1_ref[...] = h1
        hn_ref[...] = hn.astype(BF16)
        _route(hn, wh_ref, wl_ref, tok_ref, rank_t_ref, cnt_ref)


def _post_mix(h_parts, a, o_p, o_s, gs, sa, sb, wa, wb, wo, gf, extra_w, t_prompt, mode, to_cast=()):
    t = a.shape[0]
    tm = TM_PROJ if mode == "dense" else TM_MOE
    n_prompt_tiles = t_prompt // tm
    assert not to_cast or t // tm >= CAST_STEPS
    tile = lambda n: pl.BlockSpec((tm, n), lambda i: (i, 0))
    cast_spec = lambda w: pl.BlockSpec((w.shape[0] // CAST_STEPS, w.shape[1]),
                                       lambda i: (jnp.minimum(i, CAST_STEPS - 1), 0))
    full = lambda w: pl.BlockSpec(w.shape, lambda i: (0,) * w.ndim, pipeline_mode=pl.Buffered(1))
    p_map, s_map = _split_rows(n_prompt_tiles)
    if len(h_parts) == 1:
        h_specs = [tile(D_MODEL)]
    else:
        h_specs = [pl.BlockSpec((tm, D_MODEL), p_map), pl.BlockSpec((tm, D_MODEL), s_map)]
    act = jax.ShapeDtypeStruct((t, D_MODEL), F32)
    if mode == "dense":
        out_specs = [tile(D_MODEL)] + [cast_spec(w) for w in to_cast]
        out_shape = [act] + [jax.ShapeDtypeStruct(w.shape, BF16) for w in to_cast]
        scratch = []
    else:
        out_specs = [tile(D_MODEL), tile(D_MODEL), tile(LANES),
                     pl.BlockSpec((1, N_EXPERTS, 1, tm), lambda i: (i, 0, 0, 0)),
                     pl.BlockSpec((1, 1, LANES), lambda i: (i, 0, 0))]
        out_shape = [act, jax.ShapeDtypeStruct((t, D_MODEL), BF16),
                     jax.ShapeDtypeStruct((t, LANES), F32),
                     jax.ShapeDtypeStruct((t // tm, N_EXPERTS, 1, tm), F32),
                     jax.ShapeDtypeStruct((t // tm, 1, LANES), jnp.int32)]
        scratch = []
    weights = (wa, wb, wo, gf) + tuple(extra_w)
    return pl.pallas_call(
        functools.partial(_post_mix_body, n_h=len(h_parts), n_prompt_tiles=n_prompt_tiles, mode=mode),
        grid=(t // tm,),
        in_specs=h_specs + [tile(D_A),
                            pl.BlockSpec((tm, D_B), p_map), pl.BlockSpec((tm, D_B), s_map),
                            tile(D_B), tile(D_MODEL), tile(D_MODEL)] + [full(w) for w in weights]
        + [cast_spec(w) for w in to_cast],
        out_specs=out_specs,
        out_shape=out_shape,
        scratch_shapes=scratch,
        compiler_params=_cparams(("arbitrary",)),
        name="post_mix_" + mode,
    )(*h_parts, a, o_p, o_s, gs, sa, sb, *weights, *to_cast)


def _chunk_copy(src_ref, src_row, dst_ref, dst_row, n, sem):
    return pltpu.make_async_copy(src_ref.at[pl.ds(src_row, n)], dst_ref.at[pl.ds(dst_row, n)], sem)


def _aligned(row):
    return pl.multiple_of(row, SUBLANES)


def _dispatch_body(ce_ref, cj_ref, crow_ref, tot_ref, zstart_ref, zrem_ref, tail_ref,
                   rank_t_ref, hn_ref, xs_ref, sel_scr, stage, zero_ref, sems, zsem):
    i = pl.program_id(0)
    tm = hn_ref.shape[0]
    par = i % 2

    def fills(fn):
        for e in range(N_EXPERTS):
            s = zstart_ref[e]
            fn(_chunk_copy(zero_ref, 0, xs_ref, _aligned(s), TMG, zsem))
            rem = zrem_ref[e]
            for b in ZERO_BITS:
                @pl.when((rem & b) != 0)
                def _():
                    fn(_chunk_copy(zero_ref, 0, xs_ref, _aligned(s + TMG + (rem & ~(2 * b - 1))), b, zsem))

        def tail(n, carry):
            fn(_chunk_copy(zero_ref, 0, xs_ref, _aligned(tail_ref[0] + n * TMG), TMG, zsem))
            return carry

        lax.fori_loop(0, tail_ref[1], tail, 0)

    @pl.when(i == 0)
    def _():
        zero_ref[...] = jnp.zeros_like(zero_ref)
        fills(lambda cp: cp.start())
        fills(lambda cp: cp.wait())

    row = lax.broadcasted_iota(jnp.int32, (CH, tm), 0).astype(F32)
    hb = hn_ref[...]
    for g in range(N_SLOTS // SLOTS_PER_DOT):
        for s in range(g * SLOTS_PER_DOT, (g + 1) * SLOTS_PER_DOT):
            e = ce_ref[i * N_SLOTS + s]
            first = (cj_ref[i * N_SLOTS + s] * CH).astype(F32)
            sel_scr[s * CH:(s + 1) * CH, :] = jnp.where(rank_t_ref[0, e] == row + first, 1.0, 0.0).astype(BF16)
        rows = slice(g * SLOTS_PER_DOT * CH, (g + 1) * SLOTS_PER_DOT * CH)
        stage[par, rows, :] = _dot(sel_scr[rows, :], hb)

    def chunk_copy(s, tile, buf):
        return _chunk_copy(stage.at[buf], s * CH, xs_ref, _aligned(crow_ref[tile * N_SLOTS + s]), CH,
                           sems.at[buf])

    @pl.when(i > 0)
    def _():
        for s in range(N_SLOTS):
            @pl.when(s < tot_ref[i - 1])
            def _():
                chunk_copy(s, i - 1, 1 - par).wait()

    for s in range(N_SLOTS):
        @pl.when(s < tot_ref[i])
        def _():
            chunk_copy(s, i, par).start()

    @pl.when(i == pl.num_programs(0) - 1)
    def _():
        for s in range(N_SLOTS):
            @pl.when(s < tot_ref[i])
            def _():
                chunk_copy(s, i, par).wait()


def _sorted_rows(t):
    n_seg = (t // TM_MOE) * N_EXPERTS
    rows = TOP_K * t + n_seg * (SUBLANES - 1) + N_EXPERTS * (TMG - 1) + N_EXPERTS * TMG
    return -(-rows // TMG) * TMG


def _dispatch(hn, rank_t, plan):
    t = hn.shape[0]
    tm = TM_MOE
    scalars = (plan["ce"], plan["cj"], plan["crow"], plan["tot"], plan["zstart"], plan["zrem"], plan["tail"])
    grid_spec = pltpu.PrefetchScalarGridSpec(
        num_scalar_prefetch=len(scalars),
        grid=(t // tm,),
        in_specs=[pl.BlockSpec((1, N_EXPERTS, 1, tm), lambda i, *_: (i, 0, 0, 0)),
                  pl.BlockSpec((tm, D_MODEL), lambda i, *_: (i, 0))],
        out_specs=pl.BlockSpec(memory_space=pl.ANY),
        scratch_shapes=[pltpu.VMEM((N_SLOTS * CH, tm), BF16),
                        pltpu.VMEM((2, N_SLOTS * CH, D_MODEL), F32),
                        pltpu.VMEM((TMG, D_MODEL), F32),
                        pltpu.SemaphoreType.DMA((2,)), pltpu.SemaphoreType.DMA],
    )
    return pl.pallas_call(
        _dispatch_body,
        grid_spec=grid_spec,
        out_shape=jax.ShapeDtypeStruct((_sorted_rows(t), D_MODEL), F32),
        compiler_params=_cparams(("arbitrary",)),
        name="moe_dispatch",
    )(*scalars, rank_t, hn)


def _ffn_grouped_body(te_ref, live_ref, xs_ref, wg_ref, wu_ref, wd_ref, ys_ref):
    j = pl.program_id(0)

    @pl.when(live_ref[j] != 0)
    def _():
        ys_ref[...] = _swiglu(xs_ref[...].astype(BF16), wg_ref, wu_ref, wd_ref, (0,))

    @pl.when(live_ref[j] == 0)
    def _():
        ys_ref[...] = jnp.zeros_like(ys_ref)


def _ffn_grouped(xs, tile_expert, tile_live, wg, wu, wd):
    n_rows = xs.shape[0]
    row_tile = lambda j, te, live: (j, 0)
    expert = lambda j, te, live: (te[j], 0, 0)
    grid_spec = pltpu.PrefetchScalarGridSpec(
        num_scalar_prefetch=2,
        grid=(n_rows // TMG,),
        in_specs=[pl.BlockSpec((TMG, D_MODEL), row_tile),
                  pl.BlockSpec((1, D_MODEL, D_FF), expert),
                  pl.BlockSpec((1, D_MODEL, D_FF), expert),
                  pl.BlockSpec((1, D_FF, D_MODEL), expert)],
        out_specs=pl.BlockSpec((TMG, D_MODEL), row_tile),
    )
    return pl.pallas_call(
        _ffn_grouped_body,
        grid_spec=grid_spec,
        out_shape=jax.ShapeDtypeStruct((n_rows, D_MODEL), F32),
        compiler_params=_cparams(("arbitrary",)),
        name="moe_ffn",
    )(tile_expert, tile_live, xs, wg, wu, wd)


def _combine_body(ce_ref, cj_ref, crow_ref, tot_ref, tok_ref, h1_ref, gfin_ref, ys_ref,
                  yp_ref, ysm_ref, sel_scr, stage, sems, *, n_prompt_tiles):
    i = pl.program_id(0)
    tm = h1_ref.shape[0]
    par = i % 2

    def fetch(s, tile, buf):
        return _chunk_copy(ys_ref, _aligned(crow_ref[tile * N_SLOTS + s]), stage.at[buf], s * CH, CH,
                           sems.at[buf])

    def fetches(tile, buf, fn):
        for s in range(N_SLOTS):
            @pl.when(s < tot_ref[tile])
            def _():
                fn(fetch(s, tile, buf))

    @pl.when(i == 0)
    def _():
        stage[...] = jnp.zeros_like(stage)
        fetches(0, 0, lambda cp: cp.start())

    @pl.when(i + 1 < pl.num_programs(0))
    def _():
        fetches(i + 1, 1 - par, lambda cp: cp.start())

    tok = tok_ref[...]
    key1, key2 = tok[:, 0:1], tok[:, 1:2]
    gate1, gate2 = tok[:, 2:3], tok[:, 3:4]
    lane = lax.broadcasted_iota(jnp.int32, (tm, CH), 1).astype(F32)
    for s in range(N_SLOTS):
        first = (ce_ref[i * N_SLOTS + s] * KEY_MUL + cj_ref[i * N_SLOTS + s] * CH).astype(F32)
        key = lane + first
        sel_scr[:, s * CH:(s + 1) * CH] = jnp.where(key1 == key, gate1,
                                                    jnp.where(key2 == key, gate2, 0.0)).astype(BF16)

    fetches(i, par, lambda cp: cp.wait())
    ys_tile = stage[par].astype(BF16)
    halves = []
    for r0 in range(0, tm, tm // 2):
        rows = slice(r0, r0 + tm // 2)
        halves.append(_rms(h1_ref[rows, :] + _dot(sel_scr[rows, :], ys_tile), gfin_ref[...]))
    y = jnp.concatenate(halves, axis=0)

    @pl.when(i < n_prompt_tiles)
    def _():
        yp_ref[...] = y

    @pl.when(i >= n_prompt_tiles)
    def _():
        ysm_ref[...] = y


def _combine(ys, tok, h1, gfin, plan, t_prompt):
    t = h1.shape[0]
    tm = TM_MOE
    n_prompt_tiles = t_prompt // tm
    p_map, s_map = _split_rows(n_prompt_tiles)
    scalars = (plan["ce"], plan["cj"], plan["crow"], plan["tot"])
    grid_spec = pltpu.PrefetchScalarGridSpec(
        num_scalar_prefetch=len(scalars),
        grid=(t // tm,),
        in_specs=[pl.BlockSpec((tm, LANES), lambda i, *_: (i, 0)),
                  pl.BlockSpec((tm, D_MODEL), lambda i, *_: (i, 0)),
                  pl.BlockSpec((1, D_MODEL), lambda i, *_: (0, 0)),
                  pl.BlockSpec(memory_space=pl.ANY)],
        out_specs=[pl.BlockSpec((tm, D_MODEL), lambda i, *_: p_map(i)),
                   pl.BlockSpec((tm, D_MODEL), lambda i, *_: s_map(i))],
        scratch_shapes=[pltpu.VMEM((tm, N_SLOTS * CH), BF16),
                        pltpu.VMEM((2, N_SLOTS * CH, D_MODEL), F32),
                        pltpu.SemaphoreType.DMA((2,))],
    )
    return pl.pallas_call(
        functools.partial(_combine_body, n_prompt_tiles=n_prompt_tiles),
        grid_spec=grid_spec,
        out_shape=[jax.ShapeDtypeStruct((t_prompt, D_MODEL), F32),
                   jax.ShapeDtypeStruct((t - t_prompt, D_MODEL), F32)],
        compiler_params=_cparams(("arbitrary",)),
        name="moe_combine",
    )(*scalars, tok, h1, gfin, ys)


def _moe_plan(cnt, n_rows):
    i32 = jnp.int32
    seg = (cnt + SUBLANES - 1) // SUBLANES * SUBLANES
    g_len = jnp.sum(seg, axis=0)
    g_pad = (g_len + TMG - 1) // TMG * TMG
    g_span = g_pad + TMG
    g_end = jnp.cumsum(g_span)
    g_start = g_end - g_span
    seg_start = g_start[None, :] + jnp.cumsum(seg, axis=0) - seg
    n_chunks = (cnt + CH - 1) // CH
    c_end = jnp.cumsum(n_chunks, axis=1)
    tot = c_end[:, -1]
    slot = jnp.arange(N_SLOTS, dtype=i32)[None, :]
    live = slot < tot[:, None]
    ce = jnp.minimum(jnp.sum(slot[:, :, None] >= c_end[:, None, :], axis=-1), N_EXPERTS - 1)
    cj = slot - jnp.take_along_axis(c_end - n_chunks, ce, axis=1)
    cj = jnp.where(live, cj, CJ_NONE)
    crow = jnp.where(live, jnp.take_along_axis(seg_start, ce, axis=1) + CH * cj, 0)
    tile_row = jnp.arange(n_rows // TMG, dtype=i32) * TMG
    tile_expert = jnp.minimum(jnp.sum(tile_row[:, None] >= g_end[None, :], axis=1), N_EXPERTS - 1)
    tile_live = tile_row < (g_start + g_pad)[tile_expert]
    flat = lambda x: x.reshape(-1).astype(i32)
    return dict(ce=flat(ce), cj=flat(cj), crow=flat(crow), tot=flat(tot),
                zstart=flat(g_start + g_len), zrem=flat(g_pad - g_len),
                tail=jnp.stack([g_end[-1], (n_rows - g_end[-1]) // TMG]).astype(i32),
                tile_expert=flat(tile_expert), tile_live=flat(tile_live))


def _moe_routed(h1, hn, tok, rank_t, cnt, wg, wu, wd, gfin, t_prompt):
    plan = _moe_plan(cnt[:, 0, :N_EXPERTS], _sorted_rows(h1.shape[0]))
    xs = _dispatch(hn, rank_t, plan)
    ys = _ffn_grouped(xs, plan["tile_expert"], plan["tile_live"], wg, wu, wd)
    return _combine(ys, tok, h1, gfin, plan, t_prompt)


def _mix_consts(w_spatial, b_spatial, dec_seq):
    tril = jnp.tril(jnp.ones((CHUNK, CHUNK), F32))
    w_p = w_spatial * tril
    reps = CHUNK // dec_seq
    w_s = jnp.stack([jnp.kron(jnp.eye(reps, dtype=F32), w_p[g, :dec_seq, :dec_seq]) for g in range(G_A)])
    wmix = jnp.stack([w_p, w_s]).astype(BF16)
    b_p = jnp.repeat(b_spatial.T, CH_A, axis=1)
    b_s = jnp.tile(b_p[:dec_seq], (reps, 1))
    return wmix, jnp.stack([b_p, b_s])


def kernel(x_prompt, x_sample, state_hgrn, norm_mix_g, w_in, ln_v_g, ln_v_b, w_spatial, b_spatial,
           lower_bounds, hgrn_norm_g, w_branch_a, w_branch_b, w_out, norm_ffn_g, dense_w_gate,
           dense_w_up, dense_w_down, router_w, moe_w_gate, moe_w_up, moe_w_down, final_norm_g):
    n_seq, seq_len, _ = x_prompt.shape
    dec_batch, dec_seq, _ = x_sample.shape
    t_prompt = n_seq * seq_len
    t_sample = dec_batch * dec_seq
    t = t_prompt + t_sample

    lb_cum = jnp.cumsum(jax.nn.softmax(lower_bounds.astype(F32), axis=0), axis=0)
    lb_all = lb_cum - lb_cum[0:1]
    log_lb, log1m_lb, one_m_lb = jnp.log(lb_all), jnp.log1p(-lb_all), 1.0 - lb_all

    row = lambda p: p.reshape(1, -1)
    h_parts = (x_prompt.reshape(t_prompt, D_MODEL), x_sample.reshape(t_sample, D_MODEL))
    state_in = state_hgrn.astype(F32)
    states_p, states_s, v_rows = [], [], []
    for l in range(DEPTH):
        wmix, bmix = _mix_consts(w_spatial[l], b_spatial[l], dec_seq)
        a, v_s, q, lf, k, iv, gs, sa, sb = _in_proj(
            h_parts, row(norm_mix_g[l]), w_in[l].astype(BF16), row(ln_v_g[l]), row(ln_v_b[l]),
            row(log_lb[l]), row(log1m_lb[l]), row(one_m_lb[l]), wmix, bmix, t, t_prompt)
        gn = row(hgrn_norm_g[l])
        o_p, s_p = _hgrn_prompt(q, lf, k, iv, gn, n_seq, seq_len)
        o_s, s_s = _hgrn_sample(q, lf, k, iv, state_in, l, gn, t_prompt, dec_batch, dec_seq)
        mix_w = (w_branch_a[l].astype(BF16), w_branch_b[l].astype(BF16), w_out[l].astype(BF16),
                 row(norm_ffn_g[l]))
        i = l // 2
        if l % 2 == 0:
            ffn_w = (dense_w_gate[i].astype(BF16), dense_w_up[i].astype(BF16), dense_w_down[i].astype(BF16))
            moe_f32 = (moe_w_gate[i].reshape(-1, D_FF), moe_w_up[i].reshape(-1, D_FF),
                       moe_w_down[i].reshape(-1, D_MODEL))
            h_next, *moe_bf16 = _post_mix(h_parts, a, o_p, o_s, gs, sa, sb, *mix_w, ffn_w, t_prompt,
                                          "dense", moe_f32)
            h_parts = (h_next,)
        else:
            rw = jnp.pad(router_w[i], ((0, 0), (0, LANES - N_EXPERTS)))
            rw_hi = rw.astype(BF16)
            rw_lo = (rw - rw_hi.astype(F32)).astype(BF16)
            h1, hn, tok, rank_t, cnt = _post_mix(h_parts, a, o_p, o_s, gs, sa, sb, *mix_w,
                                                 (rw_hi, rw_lo), t_prompt, "router")
            h_parts = _moe_routed(h1, hn, tok, rank_t, cnt, moe_bf16[0].reshape(moe_w_gate[i].shape),
                                  moe_bf16[1].reshape(moe_w_up[i].shape),
                                  moe_bf16[2].reshape(moe_w_down[i].shape), row(final_norm_g), t_prompt)
        states_p.append(s_p)
        states_s.append(s_s)
        v_rows.append(v_s.reshape(dec_batch, dec_seq, D_A))

    y_prompt = h_parts[0].reshape(n_seq, seq_len, D_MODEL)
    y_sample = h_parts[1].reshape(dec_batch, dec_seq, D_MODEL)
    return (y_prompt, y_sample, jnp.stack(states_p).astype(x_prompt.dtype),
            jnp.stack(states_s).astype(state_hgrn.dtype), jnp.stack(v_rows))
```

```python
import functools

import numpy as np
import jax
import jax.numpy as jnp
from jax import lax
from jax.experimental import pallas as pl
from jax.experimental.pallas import tpu as pltpu

F32 = jnp.float32
BF16 = jnp.bfloat16

D_MODEL = 1024
DEPTH = 2
D_A = 512
G_A = 4
CH_A = 128
CHUNK = 128
D_B = 512
H_B = 4
DK = 128
IN_COLS = 5120
D_FF = 2816
N_EXPERTS = 8
TOP_K = 2
EPS = 1e-6

LANES = 128
SUBLANES = 8
VMEM_LIMIT = 52 * 1024 * 1024

TM_IN = 512
SUB_IN = 256
TM_PROJ = 256
CAST_STEPS = 64
TM_MOE = 512
TMG = 256
CH = 64
N_SLOTS = -(-(TOP_K * TM_MOE + N_EXPERTS * (CH - 1)) // CH)
SLOTS_PER_DOT = 8
KEY_MUL = 4096
CJ_NONE = N_SLOTS
ZERO_BITS = (128, 64, 32, 16, 8)
R_HGRN = 512
BLK = 128
LEVEL_HALVES = (64, 32, 16, 8)
SEQ_PER_STEP = 16


def _cparams(sem):
    return pltpu.CompilerParams(dimension_semantics=sem, vmem_limit_bytes=VMEM_LIMIT)


def _dot(a, b):
    return jnp.dot(a, b, preferred_element_type=F32)


def _dot_nt(a, b):
    return lax.dot_general(a, b, (((1,), (1,)), ((), ())), preferred_element_type=F32)


def _split(x, terms):
    out = []
    for _ in range(terms - 1):
        hi = x.astype(BF16)
        out.append(hi)
        x = x - hi.astype(F32)
    out.append(x.astype(BF16))
    return out


def _rms(x, g):
    return x * lax.rsqrt(jnp.mean(x * x, axis=-1, keepdims=True) + EPS) * g


def _gelu(x):
    return 0.5 * x * (1.0 + lax.erf(x * np.float32(2.0 ** -0.5)))


def _sigmoid(x):
    return jax.nn.sigmoid(x)


def _split_rows(n_prompt_tiles):
    return (lambda i: (jnp.minimum(i, n_prompt_tiles - 1), 0),
            lambda i: (jnp.maximum(i - n_prompt_tiles, 0), 0))


def _select_rows(parts, n_prompt_tiles):
    if len(parts) == 1:
        return parts[0][...]
    return jnp.where(pl.program_id(0) < n_prompt_tiles, parts[0][...], parts[1][...])


def _softplus_neg_abs(x):
    return jnp.log(1.0 + jnp.exp(-jnp.abs(x)))


def _in_proj_body(*refs, n_x, n_prompt_tiles):
    x_parts, refs = refs[:n_x], refs[n_x:]
    (g_ref, w_ref, lng_ref, lnb_ref, la_ref, l1m_ref, oml_ref, wmix_ref, bmix_ref,
     a_ref, vs_ref, q_ref, lf_ref, k_ref, iv_ref, gs_ref, sa_ref, sb_ref) = refs
    tm = a_ref.shape[0]
    gate_a, gate_b = 2 * D_A + 4 * D_B, 2 * D_A + 4 * D_B + D_MODEL
    half = D_MODEL // 2
    for r0 in range(0, tm, SUB_IN):
        rs = slice(r0, r0 + SUB_IN)
        xn = _rms(_select_rows([p.at[rs] for p in x_parts], n_prompt_tiles), g_ref[...]).astype(BF16)

        def seg(lo, n):
            return _dot(xn, w_ref[:, lo:lo + n])

        u = _gelu(seg(0, D_A))
        iv_ref[rs, :] = seg(2 * D_A + 2 * D_B, D_B)
        v = _gelu(seg(D_A, D_A))
        vc = v - jnp.mean(v, axis=-1, keepdims=True)
        v = vc * lax.rsqrt(jnp.mean(vc * vc, axis=-1, keepdims=True) + EPS) * lng_ref[...] + lnb_ref[...]
        vs_ref[rs, :] = v
        sa_ref[rs, :half] = _sigmoid(seg(gate_a, half)).astype(BF16)

        vb = v.astype(BF16)
        for c in range(SUB_IN // CHUNK):
            rows = slice(c * CHUNK, (c + 1) * CHUNK)
            out_rows = slice(r0 + c * CHUNK, r0 + (c + 1) * CHUNK)
            for g in range(G_A):
                cols = slice(g * CH_A, (g + 1) * CH_A)
                mixed = _dot(wmix_ref[0, g], vb[rows, cols]) + bmix_ref[0, :, cols]
                a_ref[out_rows, cols] = (u[rows, cols] * mixed).astype(BF16)

        fz = seg(2 * D_A + D_B, D_B)
        log_sig = jnp.minimum(fz, 0.0) - _softplus_neg_abs(fz)
        b = l1m_ref[...] + log_sig
        la = la_ref[...]
        lf_ref[rs, :] = jnp.maximum(la, b) + _softplus_neg_abs(la - b)
        k_ref[rs, :] = oml_ref[...] * _sigmoid(-fz)
        sa_ref[rs, half:] = _sigmoid(seg(gate_a + half, half)).astype(BF16)
        zq = seg(2 * D_A, D_B)
        q_ref[rs, :] = zq * _sigmoid(zq)
        sb_ref[rs, :half] = _sigmoid(seg(gate_b, half)).astype(BF16)
        gz = seg(2 * D_A + 3 * D_B, D_B)
        gs_ref[rs, :] = (gz * _sigmoid(gz)).astype(BF16)
        sb_ref[rs, half:] = _sigmoid(seg(gate_b + half, half)).astype(BF16)


def _in_proj(x_parts, g, w, lng, lnb, la, l1m, oml, wmix, bmix, t, t_prompt):
    tm = TM_IN
    n_prompt_tiles = t_prompt // tm
    row = lambda n: pl.BlockSpec((1, n), lambda i: (0, 0))
    tile = lambda n: pl.BlockSpec((tm, n), lambda i: (i, 0))
    sel = lambda i: jnp.minimum(i // n_prompt_tiles, 1)
    p_map, s_map = _split_rows(n_prompt_tiles)
    if len(x_parts) == 1:
        x_specs = [tile(D_MODEL)]
    else:
        x_specs = [pl.BlockSpec((tm, D_MODEL), p_map), pl.BlockSpec((tm, D_MODEL), s_map)]
    t_sample = t - n_prompt_tiles * tm
    shape = lambda rows, n, dt=F32: jax.ShapeDtypeStruct((rows, n), dt)
    out_shapes = ([shape(t, D_A, BF16), shape(t_sample, D_A)] + [shape(t, D_A)] * 4
                  + [shape(t, D_A, BF16)] + [shape(t, D_MODEL, BF16)] * 2)
    out_specs = ([tile(D_A), pl.BlockSpec((tm, D_A), s_map)] + [tile(D_A)] * 5 + [tile(D_MODEL)] * 2)
    return pl.pallas_call(
        functools.partial(_in_proj_body, n_x=len(x_parts), n_prompt_tiles=n_prompt_tiles),
        grid=(t // tm,),
        in_specs=x_specs + [row(D_MODEL),
                            pl.BlockSpec((D_MODEL, IN_COLS), lambda i: (0, 0), pipeline_mode=pl.Buffered(1)),
                            row(D_A), row(D_A), row(D_B), row(D_B), row(D_B),
                            pl.BlockSpec((1, G_A, CHUNK, CHUNK), lambda i: (sel(i), 0, 0, 0)),
                            pl.BlockSpec((1, CHUNK, D_A), lambda i: (sel(i), 0, 0))],
        out_specs=out_specs,
        out_shape=out_shapes,
        compiler_params=_cparams(("arbitrary",)),
        name="in_proj",
    )(*x_parts, g, w, lng, lnb, la, l1m, oml, wmix, bmix)


def _hgrn_block(q_ref, k_ref, v_ref, g_scr, masks_ref, st, r0, cols, lane, row):
    rows = slice(r0, r0 + BLK)
    qh, kh, vh = q_ref[rows, cols], k_ref[rows, cols], v_ref[rows, cols]
    gh = g_scr[rows, cols]
    g_tot = g_scr[r0 + BLK - 1:r0 + BLK, cols]

    o = _dot_nt((qh * jnp.exp(gh)).astype(BF16), st.astype(BF16))

    sc = None
    for l, m in enumerate(LEVEL_HALVES):
        refs = [jnp.broadcast_to(g_scr[r0 + b + m - 1:r0 + b + m, cols], (2 * m, DK))
                for b in range(0, BLK, 2 * m)]
        g_ref_rows = refs[0] if len(refs) == 1 else jnp.concatenate(refs, axis=0)
        right = (row & m) != 0
        x = jnp.where(right, gh - g_ref_rows, g_ref_rows - gh)
        z = (jnp.where(right, qh, kh) * jnp.exp(x)).astype(BF16)
        part = _dot_nt(z, z) * masks_ref[l]
        sc = part if sc is None else sc + part

    strips = []
    for g0 in range(0, BLK, SUBLANES):
        qg, gg = qh[g0:g0 + SUBLANES], gh[g0:g0 + SUBLANES]
        strip = jnp.zeros((SUBLANES, BLK), F32)
        for s in range(SUBLANES):
            src = r0 + g0 + s
            p = qg * jnp.exp(gg - g_scr[src:src + 1, cols]) * k_ref[src:src + 1, cols]
            strip = jnp.where(lane[:SUBLANES] == g0 + s, jnp.sum(p, axis=-1, keepdims=True), strip)
        strips.append(strip)
    diag = jnp.concatenate(strips, axis=0)
    sc = sc + jnp.where(lane <= row, diag, 0.0)

    o = o + _dot(sc.astype(BF16), vh.astype(BF16))
    kt = (kh * jnp.exp(g_tot - gh)).astype(BF16)
    st_new = st * jnp.exp(g_tot) + _dot(vh.T.astype(BF16), kt)
    return o, st_new


def _hgrn_prompt_body(q_ref, lf_ref, k_ref, v_ref, tri_ref, masks_ref, gn_ref, o_ref, sfin_ref,
                      st_ref, g_scr):
    j = pl.program_id(1)
    r = q_ref.shape[0]

    @pl.when(j == 0)
    def _():
        st_ref[...] = jnp.zeros_like(st_ref)

    tri = tri_ref[...]
    g_scr[...] = sum(_dot(tri, part) for part in _split(lf_ref[...], 2))
    lane = lax.broadcasted_iota(jnp.int32, (BLK, BLK), 1)
    row = lax.broadcasted_iota(jnp.int32, (BLK, BLK), 0)

    for h in range(H_B):
        cols = slice(h * DK, (h + 1) * DK)
        st = st_ref[h]
        for r0 in range(0, r, BLK):
            o, st = _hgrn_block(q_ref, k_ref, v_ref, g_scr, masks_ref, st, r0, cols, lane, row)
            o_ref[r0:r0 + BLK, cols] = _rms(o, gn_ref[:, cols])
        st_ref[h] = st

    @pl.when(j == pl.num_programs(1) - 1)
    def _():
        for h in range(H_B):
            sfin_ref[0, h] = st_ref[h].T


def _hgrn_consts():
    i = np.arange(R_HGRN)[:, None]
    j = np.arange(R_HGRN)[None, :]
    tri = ((i // BLK) == (j // BLK)) & (j <= i)
    i, j = np.arange(BLK)[:, None], np.arange(BLK)[None, :]
    masks = [((i // (2 * m)) == (j // (2 * m))) & ((i & m) != 0) & ((j & m) == 0) for m in LEVEL_HALVES]
    return (jnp.asarray(tri.astype(np.float32), dtype=BF16),
            jnp.asarray(np.stack(masks).astype(np.float32)))


def _hgrn_prompt(q, lf, k, iv, gn, n_seq, seq_len):
    r = R_HGRN
    nblk = seq_len // r
    tri, masks = _hgrn_consts()
    blk = pl.BlockSpec((r, D_B), lambda b, j: (b * nblk + j, 0))
    return pl.pallas_call(
        _hgrn_prompt_body,
        grid=(n_seq, nblk),
        in_specs=[blk, blk, blk, blk,
                  pl.BlockSpec(tri.shape, lambda b, j: (0, 0)),
                  pl.BlockSpec(masks.shape, lambda b, j: (0, 0, 0)),
                  pl.BlockSpec((1, D_B), lambda b, j: (0, 0))],
        out_specs=[blk, pl.BlockSpec((1, H_B, DK, DK), lambda b, j: (b, 0, 0, 0))],
        out_shape=[jax.ShapeDtypeStruct((n_seq * seq_len, D_B), F32),
                   jax.ShapeDtypeStruct((n_seq, H_B, DK, DK), F32)],
        scratch_shapes=[pltpu.VMEM((H_B, DK, DK), F32),
                        pltpu.VMEM((r, D_B), F32)],
        compiler_params=_cparams(("arbitrary", "arbitrary")),
        name="hgrn_prompt",
    )(q, lf, k, iv, tri, masks, gn)


def _sub_chunk_exact(q_ref, k_ref, v_ref, g_ref, r0, cols, sub):
    qg = q_ref[pl.ds(r0, sub), cols]
    gg = g_ref[pl.ds(r0, sub), cols]
    kg = k_ref[pl.ds(r0, sub), cols]
    vg = v_ref[pl.ds(r0, sub), cols]
    row = lax.broadcasted_iota(jnp.int32, (sub, 1), 0)
    acc = jnp.zeros((sub, DK), F32)
    for s in range(sub):
        kb = kg[s:s + 1, :]
        gb = gg[s:s + 1, :]
        vb = vg[s:s + 1, :]
        p = qg * jnp.exp(gg - gb) * kb
        rs = jnp.sum(p, axis=-1, keepdims=True)
        acc = acc + jnp.where(row >= s, rs, 0.0) * vb
    return acc


def _hgrn_sample_body(q_ref, lf_ref, k_ref, v_ref, s0_ref, cm_ref, gn_ref, o_ref, s1_ref,
                      g_scr, o_scr, *, dec_seq):
    rows = q_ref.shape[0]
    n_seq = rows // dec_seq
    parts = _split(lf_ref[...], 3)

    def cum(idx):
        m = cm_ref[idx]
        return _dot(m, parts[0]) + _dot(m, parts[1]) + _dot(m, parts[2])

    g_all = cum(0)
    g_scr[...] = g_all
    g_rev = cum(1)
    g_tot = cum(2)
    lane = lax.broadcasted_iota(jnp.int32, (DK, rows), 1)

    for h in range(H_B):
        cols = slice(h * DK, (h + 1) * DK)
        qt = (q_ref[:, cols] * jnp.exp(g_all[:, cols])).astype(BF16)
        kt_t = (k_ref[:, cols] * jnp.exp(g_rev[:, cols])).T
        dec_t = jnp.exp(g_tot[:, cols]).T
        vb = v_ref[:, cols].astype(BF16)
        for n in range(n_seq):
            r0 = n * dec_seq
            s0 = s0_ref[0, n, h]
            o_scr[r0:r0 + dec_seq, cols] = _dot(qt[r0:r0 + dec_seq, :], s0.astype(BF16))
            in_seq = (lane >= r0) & (lane < r0 + dec_seq)
            u = _dot(jnp.where(in_seq, kt_t, 0.0).astype(BF16), vb)
            s1_ref[n, h] = s0 * dec_t[:, r0:r0 + 1] + u

    def group(n, carry):
        r0 = pl.multiple_of(n * dec_seq, dec_seq)
        for h in range(H_B):
            cols = slice(h * DK, (h + 1) * DK)
            o_scr[pl.ds(r0, dec_seq), cols] += _sub_chunk_exact(q_ref, k_ref, v_ref, g_scr, r0, cols,
                                                                 dec_seq)
        return carry

    lax.fori_loop(0, n_seq, group, 0)

    for h in range(H_B):
        cols = slice(h * DK, (h + 1) * DK)
        o_ref[:, cols] = _rms(o_scr[:, cols], gn_ref[:, cols])


def _hgrn_sample(q, lf, k, iv, state_in, layer, gn, row0, n_seq, dec_seq):
    rows = SEQ_PER_STEP * dec_seq
    i = np.arange(rows)[:, None]
    j = np.arange(rows)[None, :]
    same = (i // dec_seq) == (j // dec_seq)
    cm = jnp.asarray(np.stack([same & (j <= i), same & (j > i), same]).astype(np.float32), dtype=BF16)
    blk0 = row0 // rows
    blk = pl.BlockSpec((rows, D_B), lambda n: (blk0 + n, 0))
    oblk = pl.BlockSpec((rows, D_B), lambda n: (n, 0))
    return pl.pallas_call(
        functools.partial(_hgrn_sample_body, dec_seq=dec_seq),
        grid=(n_seq // SEQ_PER_STEP,),
        in_specs=[blk, blk, blk, blk,
                  pl.BlockSpec((1, SEQ_PER_STEP, H_B, DK, DK), lambda n: (layer, n, 0, 0, 0)),
                  pl.BlockSpec(cm.shape, lambda n: (0, 0, 0)),
                  pl.BlockSpec((1, D_B), lambda n: (0, 0))],
        out_specs=[oblk, pl.BlockSpec((SEQ_PER_STEP, H_B, DK, DK), lambda n: (n, 0, 0, 0))],
        out_shape=[jax.ShapeDtypeStruct((n_seq * dec_seq, D_B), F32),
                   jax.ShapeDtypeStruct((n_seq, H_B, DK, DK), F32)],
        scratch_shapes=[pltpu.VMEM((rows, D_B), F32), pltpu.VMEM((rows, D_B), F32)],
        compiler_params=_cparams(("arbitrary",)),
        name="hgrn_sample",
    )(q, lf, k, iv, state_in, cm, gn)


FF_SPLITS = ((0, 1536), (1536, 1280))


def _swiglu(hb, wg_ref, wu_ref, wd_ref, idx):
    acc = None
    for lo, n in FF_SPLITS:
        g = _dot(hb, wg_ref[idx + (slice(None), slice(lo, lo + n))])
        u = _dot(hb, wu_ref[idx + (slice(None), slice(lo, lo + n))])
        act = (g * _sigmoid(g) * u).astype(BF16)
        part = _dot(act, wd_ref[idx + (slice(lo, lo + n), slice(None))])
        acc = part if acc is None else acc + part
    return acc


def _route(hn, wh_ref, wl_ref, tok_ref, rank_t_ref, cnt_ref):
    tm = hn.shape[0]
    hh = hn.astype(BF16)
    hl = (hn - hh.astype(F32)).astype(BF16)
    logits = _dot(hh, wh_ref[...]) + _dot(hh, wl_ref[...]) + _dot(hl, wh_ref[...])
    lane = lax.broadcasted_iota(jnp.int32, logits.shape, 1).astype(F32)
    neg = np.float32(-np.inf)
    logits = jnp.where(lane < N_EXPERTS, logits, neg)
    m1 = jnp.max(logits, axis=-1, keepdims=True)
    i1 = jnp.min(jnp.where(logits == m1, lane, float(LANES)), axis=-1, keepdims=True)
    rest = jnp.where(lane == i1, neg, logits)
    m2 = jnp.max(rest, axis=-1, keepdims=True)
    i2 = jnp.min(jnp.where(rest == m2, lane, float(LANES)), axis=-1, keepdims=True)
    e2 = jnp.exp(m2 - m1)
    den = 1.0 + e2

    routed = (lane == i1) | (lane == i2)
    sel = jnp.where(routed, 1.0, 0.0)
    r = lax.broadcasted_iota(jnp.int32, (tm, tm), 0)
    c = lax.broadcasted_iota(jnp.int32, (tm, tm), 1)
    before = jnp.where(c < r, 1.0, 0.0).astype(BF16)
    rank = _dot(before, sel.astype(BF16))
    rank1 = jnp.sum(jnp.where(lane == i1, rank, 0.0), axis=-1, keepdims=True)
    rank2 = jnp.sum(jnp.where(lane == i2, rank, 0.0), axis=-1, keepdims=True)
    tok_ref[...] = jnp.where(lane == 0.0, i1 * KEY_MUL + rank1,
                             jnp.where(lane == 1.0, i2 * KEY_MUL + rank2,
                                       jnp.where(lane == 2.0, 1.0 / den,
                                                 jnp.where(lane == 3.0, e2 / den, 0.0))))
    rank_t = jnp.where(routed, rank, -1.0).T
    for e in range(N_EXPERTS):
        rank_t_ref[0, e] = rank_t[e:e + 1, :]
    cnt_ref[0] = jnp.sum(sel, axis=0, keepdims=True).astype(jnp.int32)


def _post_mix_body(*refs, n_h, n_prompt_tiles, mode):
    h_parts, refs = refs[:n_h], refs[n_h:]
    a_ref, op_ref, os_ref, gs_ref, sa_ref, sb_ref, wa_ref, wb_ref, wo_ref, gf_ref = refs[:10]
    tm = a_ref.shape[0]
    h1s, hns = [], []
    for r0 in range(0, tm, TM_PROJ):
        rows = slice(r0, r0 + TM_PROJ)
        o = _select_rows((op_ref.at[rows], os_ref.at[rows]), n_prompt_tiles)
        pa = _dot(a_ref[rows, :], wa_ref[...])
        pb = _dot((o * gs_ref[rows, :].astype(F32)).astype(BF16), wb_ref[...])
        merged = sa_ref[rows, :].astype(F32) * pa + sb_ref[rows, :].astype(F32) * pb
        h1s.append(_select_rows([p.at[rows] for p in h_parts], n_prompt_tiles)
                   + _dot(merged.astype(BF16), wo_ref[...]))
        hns.append(_rms(h1s[-1], gf_ref[...]))
    h1 = h1s[0] if len(h1s) == 1 else jnp.concatenate(h1s, axis=0)
    hn = hns[0] if len(hns) == 1 else jnp.concatenate(hns, axis=0)
    if mode == "dense":
        wg_ref, wu_ref, wd_ref = refs[10:13]
        n_cast = (len(refs) - 14) // 2
        cast_in, out_ref, cast_out = refs[13:13 + n_cast], refs[13 + n_cast], refs[14 + n_cast:]
        out_ref[...] = h1 + _swiglu(hn.astype(BF16), wg_ref, wu_ref, wd_ref, ())
        for src_ref, dst_ref in zip(cast_in, cast_out):
            dst_ref[...] = src_ref[...].astype(BF16)
    else:
        wh_ref, wl_ref, h1_ref, hn_ref, tok_ref, rank_t_ref, cnt_ref = refs[10:]
        h1_ref[...] = h1
        hn_ref[...] = hn.astype(BF16)
        _route(hn, wh_ref, wl_ref, tok_ref, rank_t_ref, cnt_ref)


def _post_mix(h_parts, a, o_p, o_s, gs, sa, sb, wa, wb, wo, gf, extra_w, t_prompt, mode, to_cast=()):
    t = a.shape[0]
    tm = TM_PROJ if mode == "dense" else TM_MOE
    n_prompt_tiles = t_prompt // tm
    assert not to_cast or t // tm >= CAST_STEPS
    tile = lambda n: pl.BlockSpec((tm, n), lambda i: (i, 0))
    cast_spec = lambda w: pl.BlockSpec((w.shape[0] // CAST_STEPS, w.shape[1]),
                                       lambda i: (jnp.minimum(i, CAST_STEPS - 1), 0))
    full = lambda w: pl.BlockSpec(w.shape, lambda i: (0,) * w.ndim, pipeline_mode=pl.Buffered(1))
    p_map, s_map = _split_rows(n_prompt_tiles)
    if len(h_parts) == 1:
        h_specs = [tile(D_MODEL)]
    else:
        h_specs = [pl.BlockSpec((tm, D_MODEL), p_map), pl.BlockSpec((tm, D_MODEL), s_map)]
    act = jax.ShapeDtypeStruct((t, D_MODEL), F32)
    if mode == "dense":
        out_specs = [tile(D_MODEL)] + [cast_spec(w) for w in to_cast]
        out_shape = [act] + [jax.ShapeDtypeStruct(w.shape, BF16) for w in to_cast]
        scratch = []
    else:
        out_specs = [tile(D_MODEL), tile(D_MODEL), tile(LANES),
                     pl.BlockSpec((1, N_EXPERTS, 1, tm), lambda i: (i, 0, 0, 0)),
                     pl.BlockSpec((1, 1, LANES), lambda i: (i, 0, 0))]
        out_shape = [act, jax.ShapeDtypeStruct((t, D_MODEL), BF16),
                     jax.ShapeDtypeStruct((t, LANES), F32),
                     jax.ShapeDtypeStruct((t // tm, N_EXPERTS, 1, tm), F32),
                     jax.ShapeDtypeStruct((t // tm, 1, LANES), jnp.int32)]
        scratch = []
    weights = (wa, wb, wo, gf) + tuple(extra_w)
    return pl.pallas_call(
        functools.partial(_post_mix_body, n_h=len(h_parts), n_prompt_tiles=n_prompt_tiles, mode=mode),
        grid=(t // tm,),
        in_specs=h_specs + [tile(D_A),
                            pl.BlockSpec((tm, D_B), p_map), pl.BlockSpec((tm, D_B), s_map),
                            tile(D_B), tile(D_MODEL), tile(D_MODEL)] + [full(w) for w in weights]
        + [cast_spec(w) for w in to_cast],
        out_specs=out_specs,
        out_shape=out_shape,
        scratch_shapes=scratch,
        compiler_params=_cparams(("arbitrary",)),
        name="post_mix_" + mode,
    )(*h_parts, a, o_p, o_s, gs, sa, sb, *weights, *to_cast)


def _chunk_copy(src_ref, src_row, dst_ref, dst_row, n, sem):
    return pltpu.make_async_copy(src_ref.at[pl.ds(src_row, n)], dst_ref.at[pl.ds(dst_row, n)], sem)


def _aligned(row):
    return pl.multiple_of(row, SUBLANES)


def _dispatch_body(ce_ref, cj_ref, crow_ref, tot_ref, zstart_ref, zrem_ref, tail_ref,
                   rank_t_ref, hn_ref, xs_ref, sel_scr, stage, zero_ref, sems, zsem):
    i = pl.program_id(0)
    tm = hn_ref.shape[0]
    par = i % 2

    def fills(fn):
        for e in range(N_EXPERTS):
            s = zstart_ref[e]
            fn(_chunk_copy(zero_ref, 0, xs_ref, _aligned(s), TMG, zsem))
            rem = zrem_ref[e]
            for b in ZERO_BITS:
                @pl.when((rem & b) != 0)
                def _():
                    fn(_chunk_copy(zero_ref, 0, xs_ref, _aligned(s + TMG + (rem & ~(2 * b - 1))), b, zsem))

        def tail(n, carry):
            fn(_chunk_copy(zero_ref, 0, xs_ref, _aligned(tail_ref[0] + n * TMG), TMG, zsem))
            return carry

        lax.fori_loop(0, tail_ref[1], tail, 0)

    @pl.when(i == 0)
    def _():
        zero_ref[...] = jnp.zeros_like(zero_ref)
        fills(lambda cp: cp.start())
        fills(lambda cp: cp.wait())

    row = lax.broadcasted_iota(jnp.int32, (CH, tm), 0).astype(F32)
    hb = hn_ref[...]
    for g in range(N_SLOTS // SLOTS_PER_DOT):
        for s in range(g * SLOTS_PER_DOT, (g + 1) * SLOTS_PER_DOT):
            e = ce_ref[i * N_SLOTS + s]
            first = (cj_ref[i * N_SLOTS + s] * CH).astype(F32)
            sel_scr[s * CH:(s + 1) * CH, :] = jnp.where(rank_t_ref[0, e] == row + first, 1.0, 0.0).astype(BF16)
        rows = slice(g * SLOTS_PER_DOT * CH, (g + 1) * SLOTS_PER_DOT * CH)
        stage[par, rows, :] = _dot(sel_scr[rows, :], hb)

    def chunk_copy(s, tile, buf):
        return _chunk_copy(stage.at[buf], s * CH, xs_ref, _aligned(crow_ref[tile * N_SLOTS + s]), CH,
                           sems.at[buf])

    @pl.when(i > 0)
    def _():
        for s in range(N_SLOTS):
            @pl.when(s < tot_ref[i - 1])
            def _():
                chunk_copy(s, i - 1, 1 - par).wait()

    for s in range(N_SLOTS):
        @pl.when(s < tot_ref[i])
        def _():
            chunk_copy(s, i, par).start()

    @pl.when(i == pl.num_programs(0) - 1)
    def _():
        for s in range(N_SLOTS):
            @pl.when(s < tot_ref[i])
            def _():
                chunk_copy(s, i, par).wait()


def _sorted_rows(t):
    n_seg = (t // TM_MOE) * N_EXPERTS
    rows = TOP_K * t + n_seg * (SUBLANES - 1) + N_EXPERTS * (TMG - 1) + N_EXPERTS * TMG
    return -(-rows // TMG) * TMG


def _dispatch(hn, rank_t, plan):
    t = hn.shape[0]
    tm = TM_MOE
    scalars = (plan["ce"], plan["cj"], plan["crow"], plan["tot"], plan["zstart"], plan["zrem"], plan["tail"])
    grid_spec = pltpu.PrefetchScalarGridSpec(
        num_scalar_prefetch=len(scalars),
        grid=(t // tm,),
        in_specs=[pl.BlockSpec((1, N_EXPERTS, 1, tm), lambda i, *_: (i, 0, 0, 0)),
                  pl.BlockSpec((tm, D_MODEL), lambda i, *_: (i, 0))],
        out_specs=pl.BlockSpec(memory_space=pl.ANY),
        scratch_shapes=[pltpu.VMEM((N_SLOTS * CH, tm), BF16),
                        pltpu.VMEM((2, N_SLOTS * CH, D_MODEL), F32),
                        pltpu.VMEM((TMG, D_MODEL), F32),
                        pltpu.SemaphoreType.DMA((2,)), pltpu.SemaphoreType.DMA],
    )
    return pl.pallas_call(
        _dispatch_body,
        grid_spec=grid_spec,
        out_shape=jax.ShapeDtypeStruct((_sorted_rows(t), D_MODEL), F32),
        compiler_params=_cparams(("arbitrary",)),
        name="moe_dispatch",
    )(*scalars, rank_t, hn)


def _ffn_grouped_body(te_ref, live_ref, xs_ref, wg_ref, wu_ref, wd_ref, ys_ref):
    j = pl.program_id(0)

    @pl.when(live_ref[j] != 0)
    def _():
        ys_ref[...] = _swiglu(xs_ref[...].astype(BF16), wg_ref, wu_ref, wd_ref, (0,))

    @pl.when(live_ref[j] == 0)
    def _():
        ys_ref[...] = jnp.zeros_like(ys_ref)


def _ffn_grouped(xs, tile_expert, tile_live, wg, wu, wd):
    n_rows = xs.shape[0]
    row_tile = lambda j, te, live: (j, 0)
    expert = lambda j, te, live: (te[j], 0, 0)
    grid_spec = pltpu.PrefetchScalarGridSpec(
        num_scalar_prefetch=2,
        grid=(n_rows // TMG,),
        in_specs=[pl.BlockSpec((TMG, D_MODEL), row_tile),
                  pl.BlockSpec((1, D_MODEL, D_FF), expert),
                  pl.BlockSpec((1, D_MODEL, D_FF), expert),
                  pl.BlockSpec((1, D_FF, D_MODEL), expert)],
        out_specs=pl.BlockSpec((TMG, D_MODEL), row_tile),
    )
    return pl.pallas_call(
        _ffn_grouped_body,
        grid_spec=grid_spec,
        out_shape=jax.ShapeDtypeStruct((n_rows, D_MODEL), F32),
        compiler_params=_cparams(("arbitrary",)),
        name="moe_ffn",
    )(tile_expert, tile_live, xs, wg, wu, wd)


def _combine_body(ce_ref, cj_ref, crow_ref, tot_ref, tok_ref, h1_ref, gfin_ref, ys_ref,
                  yp_ref, ysm_ref, sel_scr, stage, sems, *, n_prompt_tiles):
    i = pl.program_id(0)
    tm = h1_ref.shape[0]
    par = i % 2

    def fetch(s, tile, buf):
        return _chunk_copy(ys_ref, _aligned(crow_ref[tile * N_SLOTS + s]), stage.at[buf], s * CH, CH,
                           sems.at[buf])

    def fetches(tile, buf, fn):
        for s in range(N_SLOTS):
            @pl.when(s < tot_ref[tile])
            def _():
                fn(fetch(s, tile, buf))

    @pl.when(i == 0)
    def _():
        stage[...] = jnp.zeros_like(stage)
        fetches(0, 0, lambda cp: cp.start())

    @pl.when(i + 1 < pl.num_programs(0))
    def _():
        fetches(i + 1, 1 - par, lambda cp: cp.start())

    tok = tok_ref[...]
    key1, key2 = tok[:, 0:1], tok[:, 1:2]
    gate1, gate2 = tok[:, 2:3], tok[:, 3:4]
    lane = lax.broadcasted_iota(jnp.int32, (1, LANES), 1)
    per_block = LANES // CH
    for blk in range(N_SLOTS // per_block):
        key = jnp.zeros((1, LANES), F32)
        for n in range(per_block):
            s = blk * per_block + n
            first = ce_ref[i * N_SLOTS + s] * KEY_MUL + cj_ref[i * N_SLOTS + s] * CH
            key = jnp.where(lane >= n * CH, (lane - n * CH + first).astype(F32), key)
        sel_scr[:, blk * LANES:(blk + 1) * LANES] = jnp.where(
            key1 == key, gate1, jnp.where(key2 == key, gate2, 0.0)).astype(BF16)

    fetches(i, par, lambda cp: cp.wait())
    ys_tile = stage[par].astype(BF16)
    halves = []
    for r0 in range(0, tm, tm // 2):
        rows = slice(r0, r0 + tm // 2)
        halves.append(_rms(h1_ref[rows, :] + _dot(sel_scr[rows, :], ys_tile), gfin_ref[...]))
    y = jnp.concatenate(halves, axis=0)

    @pl.when(i < n_prompt_tiles)
    def _():
        yp_ref[...] = y

    @pl.when(i >= n_prompt_tiles)
    def _():
        ysm_ref[...] = y


def _combine(ys, tok, h1, gfin, plan, t_prompt):
    t = h1.shape[0]
    tm = TM_MOE
    n_prompt_tiles = t_prompt // tm
    p_map, s_map = _split_rows(n_prompt_tiles)
    scalars = (plan["ce"], plan["cj"], plan["crow"], plan["tot"])
    grid_spec = pltpu.PrefetchScalarGridSpec(
        num_scalar_prefetch=len(scalars),
        grid=(t // tm,),
        in_specs=[pl.BlockSpec((tm, LANES), lambda i, *_: (i, 0)),
                  pl.BlockSpec((tm, D_MODEL), lambda i, *_: (i, 0)),
                  pl.BlockSpec((1, D_MODEL), lambda i, *_: (0, 0)),
                  pl.BlockSpec(memory_space=pl.ANY)],
        out_specs=[pl.BlockSpec((tm, D_MODEL), lambda i, *_: p_map(i)),
                   pl.BlockSpec((tm, D_MODEL), lambda i, *_: s_map(i))],
        scratch_shapes=[pltpu.VMEM((tm, N_SLOTS * CH), BF16),
                        pltpu.VMEM((2, N_SLOTS * CH, D_MODEL), F32),
                        pltpu.SemaphoreType.DMA((2,))],
    )
    return pl.pallas_call(
        functools.partial(_combine_body, n_prompt_tiles=n_prompt_tiles),
        grid_spec=grid_spec,
        out_shape=[jax.ShapeDtypeStruct((t_prompt, D_MODEL), F32),
                   jax.ShapeDtypeStruct((t - t_prompt, D_MODEL), F32)],
        compiler_params=_cparams(("arbitrary",)),
        name="moe_combine",
    )(*scalars, tok, h1, gfin, ys)


def _moe_plan(cnt, n_rows):
    i32 = jnp.int32
    seg = (cnt + SUBLANES - 1) // SUBLANES * SUBLANES
    g_len = jnp.sum(seg, axis=0)
    g_pad = (g_len + TMG - 1) // TMG * TMG
    g_span = g_pad + TMG
    g_end = jnp.cumsum(g_span)
    g_start = g_end - g_span
    seg_start = g_start[None, :] + jnp.cumsum(seg, axis=0) - seg
    n_chunks = (cnt + CH - 1) // CH
    c_end = jnp.cumsum(n_chunks, axis=1)
    tot = c_end[:, -1]
    slot = jnp.arange(N_SLOTS, dtype=i32)[None, :]
    live = slot < tot[:, None]
    ce = jnp.minimum(jnp.sum(slot[:, :, None] >= c_end[:, None, :], axis=-1), N_EXPERTS - 1)
    cj = slot - jnp.take_along_axis(c_end - n_chunks, ce, axis=1)
    cj = jnp.where(live, cj, CJ_NONE)
    crow = jnp.where(live, jnp.take_along_axis(seg_start, ce, axis=1) + CH * cj, 0)
    tile_row = jnp.arange(n_rows // TMG, dtype=i32) * TMG
    tile_expert = jnp.minimum(jnp.sum(tile_row[:, None] >= g_end[None, :], axis=1), N_EXPERTS - 1)
    tile_live = tile_row < (g_start + g_pad)[tile_expert]
    flat = lambda x: x.reshape(-1).astype(i32)
    return dict(ce=flat(ce), cj=flat(cj), crow=flat(crow), tot=flat(tot),
                zstart=flat(g_start + g_len), zrem=flat(g_pad - g_len),
                tail=jnp.stack([g_end[-1], (n_rows - g_end[-1]) // TMG]).astype(i32),
                tile_expert=flat(tile_expert), tile_live=flat(tile_live))


def _moe_routed(h1, hn, tok, rank_t, cnt, wg, wu, wd, gfin, t_prompt):
    plan = _moe_plan(cnt[:, 0, :N_EXPERTS], _sorted_rows(h1.shape[0]))
    xs = _dispatch(hn, rank_t, plan)
    ys = _ffn_grouped(xs, plan["tile_expert"], plan["tile_live"], wg, wu, wd)
    return _combine(ys, tok, h1, gfin, plan, t_prompt)


def _mix_consts(w_spatial, b_spatial, dec_seq):
    tril = jnp.tril(jnp.ones((CHUNK, CHUNK), F32))
    w_p = w_spatial * tril
    reps = CHUNK // dec_seq
    w_s = jnp.stack([jnp.kron(jnp.eye(reps, dtype=F32), w_p[g, :dec_seq, :dec_seq]) for g in range(G_A)])
    wmix = jnp.stack([w_p, w_s]).astype(BF16)
    b_p = jnp.repeat(b_spatial.T, CH_A, axis=1)
    b_s = jnp.tile(b_p[:dec_seq], (reps, 1))
    return wmix, jnp.stack([b_p, b_s])


def kernel(x_prompt, x_sample, state_hgrn, norm_mix_g, w_in, ln_v_g, ln_v_b, w_spatial, b_spatial,
           lower_bounds, hgrn_norm_g, w_branch_a, w_branch_b, w_out, norm_ffn_g, dense_w_gate,
           dense_w_up, dense_w_down, router_w, moe_w_gate, moe_w_up, moe_w_down, final_norm_g):
    n_seq, seq_len, _ = x_prompt.shape
    dec_batch, dec_seq, _ = x_sample.shape
    t_prompt = n_seq * seq_len
    t_sample = dec_batch * dec_seq
    t = t_prompt + t_sample

    lb_cum = jnp.cumsum(jax.nn.softmax(lower_bounds.astype(F32), axis=0), axis=0)
    lb_all = lb_cum - lb_cum[0:1]
    log_lb, log1m_lb, one_m_lb = jnp.log(lb_all), jnp.log1p(-lb_all), 1.0 - lb_all

    row = lambda p: p.reshape(1, -1)
    h_parts = (x_prompt.reshape(t_prompt, D_MODEL), x_sample.reshape(t_sample, D_MODEL))
    state_in = state_hgrn.astype(F32)
    states_p, states_s, v_rows = [], [], []
    for l in range(DEPTH):
        wmix, bmix = _mix_consts(w_spatial[l], b_spatial[l], dec_seq)
        a, v_s, q, lf, k, iv, gs, sa, sb = _in_proj(
            h_parts, row(norm_mix_g[l]), w_in[l].astype(BF16), row(ln_v_g[l]), row(ln_v_b[l]),
            row(log_lb[l]), row(log1m_lb[l]), row(one_m_lb[l]), wmix, bmix, t, t_prompt)
        gn = row(hgrn_norm_g[l])
        o_p, s_p = _hgrn_prompt(q, lf, k, iv, gn, n_seq, seq_len)
        o_s, s_s = _hgrn_sample(q, lf, k, iv, state_in, l, gn, t_prompt, dec_batch, dec_seq)
        mix_w = (w_branch_a[l].astype(BF16), w_branch_b[l].astype(BF16), w_out[l].astype(BF16),
                 row(norm_ffn_g[l]))
        i = l // 2
        if l % 2 == 0:
            ffn_w = (dense_w_gate[i].astype(BF16), dense_w_up[i].astype(BF16), dense_w_down[i].astype(BF16))
            moe_f32 = (moe_w_gate[i].reshape(-1, D_FF), moe_w_up[i].reshape(-1, D_FF),
                       moe_w_down[i].reshape(-1, D_MODEL))
            h_next, *moe_bf16 = _post_mix(h_parts, a, o_p, o_s, gs, sa, sb, *mix_w, ffn_w, t_prompt,
                                          "dense", moe_f32)
            h_parts = (h_next,)
        else:
            rw = jnp.pad(router_w[i], ((0, 0), (0, LANES - N_EXPERTS)))
            rw_hi = rw.astype(BF16)
            rw_lo = (rw - rw_hi.astype(F32)).astype(BF16)
            h1, hn, tok, rank_t, cnt = _post_mix(h_parts, a, o_p, o_s, gs, sa, sb, *mix_w,
                                                 (rw_hi, rw_lo), t_prompt, "router")
            h_parts = _moe_routed(h1, hn, tok, rank_t, cnt, moe_bf16[0].reshape(moe_w_gate[i].shape),
                                  moe_bf16[1].reshape(moe_w_up[i].shape),
                                  moe_bf16[2].reshape(moe_w_down[i].shape), row(final_norm_g), t_prompt)
        states_p.append(s_p)
        states_s.append(s_s)
        v_rows.append(v_s.reshape(dec_batch, dec_seq, D_A))

    y_prompt = h_parts[0].reshape(n_seq, seq_len, D_MODEL)
    y_sample = h_parts[1].reshape(dec_batch, dec_seq, D_MODEL)
    return (y_prompt, y_sample, jnp.stack(states_p).astype(x_prompt.dtype),
            jnp.stack(states_s).astype(state_hgrn.dtype), jnp.stack(v_rows))
```

```python
import functools

import numpy as np
import jax
import jax.numpy as jnp
from jax import lax
from jax.experimental import pallas as pl
from jax.experimental.pallas import tpu as pltpu

F32 = jnp.float32
BF16 = jnp.bfloat16

D_MODEL = 1024
DEPTH = 2
D_A = 512
G_A = 4
CH_A = 128
CHUNK = 128
D_B = 512
H_B = 4
DK = 128
IN_COLS = 5120
D_FF = 2816
N_EXPERTS = 8
TOP_K = 2
EPS = 1e-6

LANES = 128
SUBLANES = 8
VMEM_LIMIT = 52 * 1024 * 1024

TM_IN = 512
SUB_IN = 256
TM_PROJ = 256
CAST_STEPS = 64
TM_MOE = 512
TMG = 256
CH = 64
N_SLOTS = -(-(TOP_K * TM_MOE + N_EXPERTS * (CH - 1)) // CH)
SLOTS_PER_DOT = 8
KEY_MUL = 4096
CJ_NONE = N_SLOTS
ZERO_BITS = (128, 64, 32, 16, 8)
R_HGRN = 512
BLK = 128
LEVEL_HALVES = (64, 32, 16, 8)
SEQ_PER_STEP = 16


def _cparams(sem):
    return pltpu.CompilerParams(dimension_semantics=sem, vmem_limit_bytes=VMEM_LIMIT)


def _dot(a, b):
    return jnp.dot(a, b, preferred_element_type=F32)


def _dot_nt(a, b):
    return lax.dot_general(a, b, (((1,), (1,)), ((), ())), preferred_element_type=F32)


def _split(x, terms):
    out = []
    for _ in range(terms - 1):
        hi = x.astype(BF16)
        out.append(hi)
        x = x - hi.astype(F32)
    out.append(x.astype(BF16))
    return out


def _rms(x, g):
    return x * lax.rsqrt(jnp.mean(x * x, axis=-1, keepdims=True) + EPS) * g


def _gelu(x):
    return 0.5 * x * (1.0 + lax.erf(x * np.float32(2.0 ** -0.5)))


def _sigmoid(x):
    return jax.nn.sigmoid(x)


def _split_rows(n_prompt_tiles):
    return (lambda i: (jnp.minimum(i, n_prompt_tiles - 1), 0),
            lambda i: (jnp.maximum(i - n_prompt_tiles, 0), 0))


def _select_rows(parts, n_prompt_tiles):
    if len(parts) == 1:
        return parts[0][...]
    return jnp.where(pl.program_id(0) < n_prompt_tiles, parts[0][...], parts[1][...])


def _softplus_neg_abs(x):
    return jnp.log(1.0 + jnp.exp(-jnp.abs(x)))


def _in_proj_body(*refs, n_x, n_prompt_tiles):
    x_parts, refs = refs[:n_x], refs[n_x:]
    (g_ref, w_ref, lng_ref, lnb_ref, la_ref, l1m_ref, oml_ref, wmix_ref, bmix_ref,
     a_ref, vs_ref, q_ref, lf_ref, k_ref, iv_ref, gs_ref, sa_ref, sb_ref) = refs
    tm = a_ref.shape[0]
    gate_a, gate_b = 2 * D_A + 4 * D_B, 2 * D_A + 4 * D_B + D_MODEL
    half = D_MODEL // 2
    for r0 in range(0, tm, SUB_IN):
        rs = slice(r0, r0 + SUB_IN)
        xn = _rms(_select_rows([p.at[rs] for p in x_parts], n_prompt_tiles), g_ref[...]).astype(BF16)

        def seg(lo, n):
            return _dot(xn, w_ref[:, lo:lo + n])

        u = _gelu(seg(0, D_A))
        iv_ref[rs, :] = seg(2 * D_A + 2 * D_B, D_B)
        v = _gelu(seg(D_A, D_A))
        vc = v - jnp.mean(v, axis=-1, keepdims=True)
        v = vc * lax.rsqrt(jnp.mean(vc * vc, axis=-1, keepdims=True) + EPS) * lng_ref[...] + lnb_ref[...]
        vs_ref[rs, :] = v
        sa_ref[rs, :half] = _sigmoid(seg(gate_a, half)).astype(BF16)

        vb = v.astype(BF16)
        for c in range(SUB_IN // CHUNK):
            rows = slice(c * CHUNK, (c + 1) * CHUNK)
            out_rows = slice(r0 + c * CHUNK, r0 + (c + 1) * CHUNK)
            for g in range(G_A):
                cols = slice(g * CH_A, (g + 1) * CH_A)
                mixed = _dot(wmix_ref[0, g], vb[rows, cols]) + bmix_ref[0, :, cols]
                a_ref[out_rows, cols] = (u[rows, cols] * mixed).astype(BF16)

        fz = seg(2 * D_A + D_B, D_B)
        log_sig = jnp.minimum(fz, 0.0) - _softplus_neg_abs(fz)
        b = l1m_ref[...] + log_sig
        la = la_ref[...]
        lf_ref[rs, :] = jnp.maximum(la, b) + _softplus_neg_abs(la - b)
        k_ref[rs, :] = oml_ref[...] * _sigmoid(-fz)
        sa_ref[rs, half:] = _sigmoid(seg(gate_a + half, half)).astype(BF16)
        zq = seg(2 * D_A, D_B)
        q_ref[rs, :] = zq * _sigmoid(zq)
        sb_ref[rs, :half] = _sigmoid(seg(gate_b, half)).astype(BF16)
        gz = seg(2 * D_A + 3 * D_B, D_B)
        gs_ref[rs, :] = (gz * _sigmoid(gz)).astype(BF16)
        sb_ref[rs, half:] = _sigmoid(seg(gate_b + half, half)).astype(BF16)


def _in_proj(x_parts, g, w, lng, lnb, la, l1m, oml, wmix, bmix, t, t_prompt):
    tm = TM_IN
    n_prompt_tiles = t_prompt // tm
    row = lambda n: pl.BlockSpec((1, n), lambda i: (0, 0))
    tile = lambda n: pl.BlockSpec((tm, n), lambda i: (i, 0))
    sel = lambda i: jnp.minimum(i // n_prompt_tiles, 1)
    p_map, s_map = _split_rows(n_prompt_tiles)
    if len(x_parts) == 1:
        x_specs = [tile(D_MODEL)]
    else:
        x_specs = [pl.BlockSpec((tm, D_MODEL), p_map), pl.BlockSpec((tm, D_MODEL), s_map)]
    t_sample = t - n_prompt_tiles * tm
    shape = lambda rows, n, dt=F32: jax.ShapeDtypeStruct((rows, n), dt)
    out_shapes = ([shape(t, D_A, BF16), shape(t_sample, D_A)] + [shape(t, D_A)] * 4
                  + [shape(t, D_A, BF16)] + [shape(t, D_MODEL, BF16)] * 2)
    out_specs = ([tile(D_A), pl.BlockSpec((tm, D_A), s_map)] + [tile(D_A)] * 5 + [tile(D_MODEL)] * 2)
    return pl.pallas_call(
        functools.partial(_in_proj_body, n_x=len(x_parts), n_prompt_tiles=n_prompt_tiles),
        grid=(t // tm,),
        in_specs=x_specs + [row(D_MODEL),
                            pl.BlockSpec((D_MODEL, IN_COLS), lambda i: (0, 0), pipeline_mode=pl.Buffered(1)),
                            row(D_A), row(D_A), row(D_B), row(D_B), row(D_B),
                            pl.BlockSpec((1, G_A, CHUNK, CHUNK), lambda i: (sel(i), 0, 0, 0)),
                            pl.BlockSpec((1, CHUNK, D_A), lambda i: (sel(i), 0, 0))],
        out_specs=out_specs,
        out_shape=out_shapes,
        compiler_params=_cparams(("arbitrary",)),
        name="in_proj",
    )(*x_parts, g, w, lng, lnb, la, l1m, oml, wmix, bmix)


def _hgrn_block(q_ref, k_ref, v_ref, g_scr, masks_ref, st, r0, cols, lane, row):
    rows = slice(r0, r0 + BLK)
    qh, kh, vh = q_ref[rows, cols], k_ref[rows, cols], v_ref[rows, cols]
    gh = g_scr[rows, cols]
    g_tot = g_scr[r0 + BLK - 1:r0 + BLK, cols]

    o = _dot_nt((qh * jnp.exp(gh)).astype(BF16), st.astype(BF16))

    sc = None
    for l, m in enumerate(LEVEL_HALVES):
        refs = [jnp.broadcast_to(g_scr[r0 + b + m - 1:r0 + b + m, cols], (2 * m, DK))
                for b in range(0, BLK, 2 * m)]
        g_ref_rows = refs[0] if len(refs) == 1 else jnp.concatenate(refs, axis=0)
        right = (row & m) != 0
        x = jnp.where(right, gh - g_ref_rows, g_ref_rows - gh)
        z = (jnp.where(right, qh, kh) * jnp.exp(x)).astype(BF16)
        part = _dot_nt(z, z) * masks_ref[l]
        sc = part if sc is None else sc + part

    strips = []
    for g0 in range(0, BLK, SUBLANES):
        qg, gg = qh[g0:g0 + SUBLANES], gh[g0:g0 + SUBLANES]
        strip = jnp.zeros((SUBLANES, BLK), F32)
        for s in range(SUBLANES):
            src = r0 + g0 + s
            p = qg * jnp.exp(gg - g_scr[src:src + 1, cols]) * k_ref[src:src + 1, cols]
            strip = jnp.where(lane[:SUBLANES] == g0 + s, jnp.sum(p, axis=-1, keepdims=True), strip)
        strips.append(strip)
    diag = jnp.concatenate(strips, axis=0)
    sc = sc + jnp.where(lane <= row, diag, 0.0)

    o = o + _dot(sc.astype(BF16), vh.astype(BF16))
    kt = (kh * jnp.exp(g_tot - gh)).astype(BF16)
    st_new = st * jnp.exp(g_tot) + _dot(vh.T.astype(BF16), kt)
    return o, st_new


def _hgrn_prompt_body(q_ref, lf_ref, k_ref, v_ref, tri_ref, masks_ref, gn_ref, o_ref, sfin_ref,
                      st_ref, g_scr):
    j = pl.program_id(1)
    r = q_ref.shape[0]

    @pl.when(j == 0)
    def _():
        st_ref[...] = jnp.zeros_like(st_ref)

    tri = tri_ref[...]
    g_scr[...] = sum(_dot(tri, part) for part in _split(lf_ref[...], 2))
    lane = lax.broadcasted_iota(jnp.int32, (BLK, BLK), 1)
    row = lax.broadcasted_iota(jnp.int32, (BLK, BLK), 0)

    states = [st_ref[h] for h in range(H_B)]
    for r0 in range(0, r, BLK):
        for h in range(H_B):
            cols = slice(h * DK, (h + 1) * DK)
            o, states[h] = _hgrn_block(q_ref, k_ref, v_ref, g_scr, masks_ref, states[h], r0, cols, lane, row)
            o_ref[r0:r0 + BLK, cols] = _rms(o, gn_ref[:, cols])
    for h in range(H_B):
        st_ref[h] = states[h]

    @pl.when(j == pl.num_programs(1) - 1)
    def _():
        for h in range(H_B):
            sfin_ref[0, h] = st_ref[h].T


def _hgrn_consts():
    i = np.arange(R_HGRN)[:, None]
    j = np.arange(R_HGRN)[None, :]
    tri = ((i // BLK) == (j // BLK)) & (j <= i)
    i, j = np.arange(BLK)[:, None], np.arange(BLK)[None, :]
    masks = [((i // (2 * m)) == (j // (2 * m))) & ((i & m) != 0) & ((j & m) == 0) for m in LEVEL_HALVES]
    return (jnp.asarray(tri.astype(np.float32), dtype=BF16),
            jnp.asarray(np.stack(masks).astype(np.float32)))


def _hgrn_prompt(q, lf, k, iv, gn, n_seq, seq_len):
    r = R_HGRN
    nblk = seq_len // r
    tri, masks = _hgrn_consts()
    blk = pl.BlockSpec((r, D_B), lambda b, j: (b * nblk + j, 0))
    return pl.pallas_call(
        _hgrn_prompt_body,
        grid=(n_seq, nblk),
        in_specs=[blk, blk, blk, blk,
                  pl.BlockSpec(tri.shape, lambda b, j: (0, 0)),
                  pl.BlockSpec(masks.shape, lambda b, j: (0, 0, 0)),
                  pl.BlockSpec((1, D_B), lambda b, j: (0, 0))],
        out_specs=[blk, pl.BlockSpec((1, H_B, DK, DK), lambda b, j: (b, 0, 0, 0))],
        out_shape=[jax.ShapeDtypeStruct((n_seq * seq_len, D_B), F32),
                   jax.ShapeDtypeStruct((n_seq, H_B, DK, DK), F32)],
        scratch_shapes=[pltpu.VMEM((H_B, DK, DK), F32),
                        pltpu.VMEM((r, D_B), F32)],
        compiler_params=_cparams(("arbitrary", "arbitrary")),
        name="hgrn_prompt",
    )(q, lf, k, iv, tri, masks, gn)


def _sub_chunk_exact(q_ref, k_ref, v_ref, g_ref, r0, cols, sub):
    qg = q_ref[pl.ds(r0, sub), cols]
    gg = g_ref[pl.ds(r0, sub), cols]
    kg = k_ref[pl.ds(r0, sub), cols]
    vg = v_ref[pl.ds(r0, sub), cols]
    row = lax.broadcasted_iota(jnp.int32, (sub, 1), 0)
    acc = jnp.zeros((sub, DK), F32)
    for s in range(sub):
        kb = kg[s:s + 1, :]
        gb = gg[s:s + 1, :]
        vb = vg[s:s + 1, :]
        p = qg * jnp.exp(gg - gb) * kb
        rs = jnp.sum(p, axis=-1, keepdims=True)
        acc = acc + jnp.where(row >= s, rs, 0.0) * vb
    return acc


def _hgrn_sample_body(q_ref, lf_ref, k_ref, v_ref, s0_ref, cm_ref, gn_ref, *rest, dec_seq, out_layer):
    o_ref, s1_ref, g_scr, o_scr = rest[-4:]
    rows = q_ref.shape[0]
    n_seq = rows // dec_seq
    for later in range(s1_ref.shape[0]):
        if later != out_layer:
            s1_ref[later] = jnp.zeros(s1_ref.shape[1:], F32)
    parts = _split(lf_ref[...], 3)

    def cum(idx):
        m = cm_ref[idx]
        return _dot(m, parts[0]) + _dot(m, parts[1]) + _dot(m, parts[2])

    g_all = cum(0)
    g_scr[...] = g_all
    g_rev = cum(1)
    g_tot = cum(2)
    lane = lax.broadcasted_iota(jnp.int32, (DK, rows), 1)

    for h in range(H_B):
        cols = slice(h * DK, (h + 1) * DK)
        qt = (q_ref[:, cols] * jnp.exp(g_all[:, cols])).astype(BF16)
        kt_t = (k_ref[:, cols] * jnp.exp(g_rev[:, cols])).T
        dec_t = jnp.exp(g_tot[:, cols]).T
        vb = v_ref[:, cols].astype(BF16)
        for n in range(n_seq):
            r0 = n * dec_seq
            s0 = s0_ref[0, n, h]
            o_scr[r0:r0 + dec_seq, cols] = _dot(qt[r0:r0 + dec_seq, :], s0.astype(BF16))
            in_seq = (lane >= r0) & (lane < r0 + dec_seq)
            u = _dot(jnp.where(in_seq, kt_t, 0.0).astype(BF16), vb)
            s1_ref[out_layer, n, h] = s0 * dec_t[:, r0:r0 + 1] + u

    def group(n, carry):
        r0 = pl.multiple_of(n * dec_seq, dec_seq)
        for h in range(H_B):
            cols = slice(h * DK, (h + 1) * DK)
            o_scr[pl.ds(r0, dec_seq), cols] += _sub_chunk_exact(q_ref, k_ref, v_ref, g_scr, r0, cols,
                                                                 dec_seq)
        return carry

    lax.fori_loop(0, n_seq, group, 0)

    for h in range(H_B):
        cols = slice(h * DK, (h + 1) * DK)
        o_ref[:, cols] = _rms(o_scr[:, cols], gn_ref[:, cols])


def _hgrn_sample(q, lf, k, iv, state_in, layer, states_out, gn, row0, n_seq, dec_seq):
    rows = SEQ_PER_STEP * dec_seq
    i = np.arange(rows)[:, None]
    j = np.arange(rows)[None, :]
    same = (i // dec_seq) == (j // dec_seq)
    cm = jnp.asarray(np.stack([same & (j <= i), same & (j > i), same]).astype(np.float32), dtype=BF16)
    blk0 = row0 // rows
    blk = pl.BlockSpec((rows, D_B), lambda n: (blk0 + n, 0))
    oblk = pl.BlockSpec((rows, D_B), lambda n: (n, 0))
    depth = state_in.shape[0]
    in_specs = [blk, blk, blk, blk,
                pl.BlockSpec((1, SEQ_PER_STEP, H_B, DK, DK), lambda n: (layer, n, 0, 0, 0)),
                pl.BlockSpec(cm.shape, lambda n: (0, 0, 0)),
                pl.BlockSpec((1, D_B), lambda n: (0, 0))]
    operands = [q, lf, k, iv, state_in, cm, gn]
    if states_out is None:
        sblk = pl.BlockSpec((depth, SEQ_PER_STEP, H_B, DK, DK), lambda n: (0, n, 0, 0, 0))
        out_layer, aliases = layer, {}
    else:
        sblk = pl.BlockSpec((1, SEQ_PER_STEP, H_B, DK, DK), lambda n: (layer, n, 0, 0, 0))
        out_layer, aliases = 0, {len(operands): 1}
        in_specs.append(pl.BlockSpec(memory_space=pl.ANY))
        operands.append(states_out)
    return pl.pallas_call(
        functools.partial(_hgrn_sample_body, dec_seq=dec_seq, out_layer=out_layer),
        grid=(n_seq // SEQ_PER_STEP,),
        in_specs=in_specs,
        out_specs=[oblk, sblk],
        out_shape=[jax.ShapeDtypeStruct((n_seq * dec_seq, D_B), F32),
                   jax.ShapeDtypeStruct((depth, n_seq, H_B, DK, DK), F32)],
        scratch_shapes=[pltpu.VMEM((rows, D_B), F32), pltpu.VMEM((rows, D_B), F32)],
        input_output_aliases=aliases,
        compiler_params=_cparams(("arbitrary",)),
        name="hgrn_sample",
    )(*operands)


FF_SPLITS = ((0, 1536), (1536, 1280))


def _swiglu(hb, wg_ref, wu_ref, wd_ref, idx):
    acc = None
    for lo, n in FF_SPLITS:
        g = _dot(hb, wg_ref[idx + (slice(None), slice(lo, lo + n))])
        u = _dot(hb, wu_ref[idx + (slice(None), slice(lo, lo + n))])
        act = (g * _sigmoid(g) * u).astype(BF16)
        part = _dot(act, wd_ref[idx + (slice(lo, lo + n), slice(None))])
        acc = part if acc is None else acc + part
    return acc


def _route(hn, wh_ref, wl_ref, tok_ref, rank_t_ref, cnt_ref):
    tm = hn.shape[0]
    hh = hn.astype(BF16)
    hl = (hn - hh.astype(F32)).astype(BF16)
    logits = _dot(hh, wh_ref[...]) + _dot(hh, wl_ref[...]) + _dot(hl, wh_ref[...])
    lane = lax.broadcasted_iota(jnp.int32, logits.shape, 1).astype(F32)
    neg = np.float32(-np.inf)
    logits = jnp.where(lane < N_EXPERTS, logits, neg)
    m1 = jnp.max(logits, axis=-1, keepdims=True)
    i1 = jnp.min(jnp.where(logits == m1, lane, float(LANES)), axis=-1, keepdims=True)
    rest = jnp.where(lane == i1, neg, logits)
    m2 = jnp.max(rest, axis=-1, keepdims=True)
    i2 = jnp.min(jnp.where(rest == m2, lane, float(LANES)), axis=-1, keepdims=True)
    e2 = jnp.exp(m2 - m1)
    den = 1.0 + e2

    routed = (lane == i1) | (lane == i2)
    sel = jnp.where(routed, 1.0, 0.0)
    r = lax.broadcasted_iota(jnp.int32, (tm, tm), 0)
    c = lax.broadcasted_iota(jnp.int32, (tm, tm), 1)
    before = jnp.where(c < r, 1.0, 0.0).astype(BF16)
    rank = _dot(before, sel.astype(BF16))
    rank1 = jnp.sum(jnp.where(lane == i1, rank, 0.0), axis=-1, keepdims=True)
    rank2 = jnp.sum(jnp.where(lane == i2, rank, 0.0), axis=-1, keepdims=True)
    tok_ref[...] = jnp.where(lane == 0.0, i1 * KEY_MUL + rank1,
                             jnp.where(lane == 1.0, i2 * KEY_MUL + rank2,
                                       jnp.where(lane == 2.0, 1.0 / den,
                                                 jnp.where(lane == 3.0, e2 / den, 0.0))))
    rank_t = jnp.where(routed, rank, -1.0).T
    for e in range(N_EXPERTS):
        rank_t_ref[0, e] = rank_t[e:e + 1, :]
    cnt_ref[0] = jnp.sum(sel, axis=0, keepdims=True).astype(jnp.int32)


def _post_mix_body(*refs, n_h, n_prompt_tiles, mode):
    h_parts, refs = refs[:n_h], refs[n_h:]
    a_ref, op_ref, os_ref, gs_ref, sa_ref, sb_ref, wa_ref, wb_ref, wo_ref, gf_ref = refs[:10]
    tm = a_ref.shape[0]
    h1s, hns = [], []
    for r0 in range(0, tm, TM_PROJ):
        rows = slice(r0, r0 + TM_PROJ)
        o = _select_rows((op_ref.at[rows], os_ref.at[rows]), n_prompt_tiles)
        pa = _dot(a_ref[rows, :], wa_ref[...])
        pb = _dot((o * gs_ref[rows, :].astype(F32)).astype(BF16), wb_ref[...])
        merged = sa_ref[rows, :].astype(F32) * pa + sb_ref[rows, :].astype(F32) * pb
        h1s.append(_select_rows([p.at[rows] for p in h_parts], n_prompt_tiles)
                   + _dot(merged.astype(BF16), wo_ref[...]))
        hns.append(_rms(h1s[-1], gf_ref[...]))
    h1 = h1s[0] if len(h1s) == 1 else jnp.concatenate(h1s, axis=0)
    hn = hns[0] if len(hns) == 1 else jnp.concatenate(hns, axis=0)
    if mode == "dense":
        wg_ref, wu_ref, wd_ref = refs[10:13]
        n_cast = (len(refs) - 14) // 2
        cast_in, out_ref, cast_out = refs[13:13 + n_cast], refs[13 + n_cast], refs[14 + n_cast:]
        out_ref[...] = h1 + _swiglu(hn.astype(BF16), wg_ref, wu_ref, wd_ref, ())
        for src_ref, dst_ref in zip(cast_in, cast_out):
            dst_ref[...] = src_ref[...].astype(BF16)
    else:
        wh_ref, wl_ref, h1_ref, hn_ref, tok_ref, rank_t_ref, cnt_ref = refs[10:]
        h1_ref[...] = h1
        hn_ref[...] = hn.astype(BF16)
        _route(hn, wh_ref, wl_ref, tok_ref, rank_t_ref, cnt_ref)


def _post_mix(h_parts, a, o_p, o_s, gs, sa, sb, wa, wb, wo, gf, extra_w, t_prompt, mode, to_cast=()):
    t = a.shape[0]
    tm = TM_PROJ if mode == "dense" else TM_MOE
    n_prompt_tiles = t_prompt // tm
    assert not to_cast or t // tm >= CAST_STEPS
    tile = lambda n: pl.BlockSpec((tm, n), lambda i: (i, 0))
    cast_spec = lambda w: pl.BlockSpec((w.shape[0] // CAST_STEPS, w.shape[1]),
                                       lambda i: (jnp.minimum(i, CAST_STEPS - 1), 0))
    full = lambda w: pl.BlockSpec(w.shape, lambda i: (0,) * w.ndim, pipeline_mode=pl.Buffered(1))
    p_map, s_map = _split_rows(n_prompt_tiles)
    if len(h_parts) == 1:
        h_specs = [tile(D_MODEL)]
    else:
        h_specs = [pl.BlockSpec((tm, D_MODEL), p_map), pl.BlockSpec((tm, D_MODEL), s_map)]
    act = jax.ShapeDtypeStruct((t, D_MODEL), F32)
    if mode == "dense":
        out_specs = [tile(D_MODEL)] + [cast_spec(w) for w in to_cast]
        out_shape = [act] + [jax.ShapeDtypeStruct(w.shape, BF16) for w in to_cast]
        scratch = []
    else:
        out_specs = [tile(D_MODEL), tile(D_MODEL), tile(LANES),
                     pl.BlockSpec((1, N_EXPERTS, 1, tm), lambda i: (i, 0, 0, 0)),
                     pl.BlockSpec((1, 1, LANES), lambda i: (i, 0, 0))]
        out_shape = [act, jax.ShapeDtypeStruct((t, D_MODEL), BF16),
                     jax.ShapeDtypeStruct((t, LANES), F32),
                     jax.ShapeDtypeStruct((t // tm, N_EXPERTS, 1, tm), F32),
                     jax.ShapeDtypeStruct((t // tm, 1, LANES), jnp.int32)]
        scratch = []
    weights = (wa, wb, wo, gf) + tuple(extra_w)
    return pl.pallas_call(
        functools.partial(_post_mix_body, n_h=len(h_parts), n_prompt_tiles=n_prompt_tiles, mode=mode),
        grid=(t // tm,),
        in_specs=h_specs + [tile(D_A),
                            pl.BlockSpec((tm, D_B), p_map), pl.BlockSpec((tm, D_B), s_map),
                            tile(D_B), tile(D_MODEL), tile(D_MODEL)] + [full(w) for w in weights]
        + [cast_spec(w) for w in to_cast],
        out_specs=out_specs,
        out_shape=out_shape,
        scratch_shapes=scratch,
        compiler_params=_cparams(("arbitrary",)),
        name="post_mix_" + mode,
    )(*h_parts, a, o_p, o_s, gs, sa, sb, *weights, *to_cast)


def _chunk_copy(src_ref, src_row, dst_ref, dst_row, n, sem):
    return pltpu.make_async_copy(src_ref.at[pl.ds(src_row, n)], dst_ref.at[pl.ds(dst_row, n)], sem)


def _aligned(row):
    return pl.multiple_of(row, SUBLANES)


def _dispatch_body(ce_ref, cj_ref, crow_ref, tot_ref, zstart_ref, zrem_ref, tail_ref,
                   rank_t_ref, hn_ref, xs_ref, sel_scr, stage, zero_ref, sems, zsem):
    i = pl.program_id(0)
    tm = hn_ref.shape[0]
    par = i % 2

    def fills(fn):
        for e in range(N_EXPERTS):
            s = zstart_ref[e]
            fn(_chunk_copy(zero_ref, 0, xs_ref, _aligned(s), TMG, zsem))
            rem = zrem_ref[e]
            for b in ZERO_BITS:
                @pl.when((rem & b) != 0)
                def _():
                    fn(_chunk_copy(zero_ref, 0, xs_ref, _aligned(s + TMG + (rem & ~(2 * b - 1))), b, zsem))

        def tail(n, carry):
            fn(_chunk_copy(zero_ref, 0, xs_ref, _aligned(tail_ref[0] + n * TMG), TMG, zsem))
            return carry

        lax.fori_loop(0, tail_ref[1], tail, 0)

    @pl.when(i == 0)
    def _():
        zero_ref[...] = jnp.zeros_like(zero_ref)
        fills(lambda cp: cp.start())
        fills(lambda cp: cp.wait())

    row = lax.broadcasted_iota(jnp.int32, (CH, tm), 0).astype(F32)
    hb = hn_ref[...]
    for g in range(N_SLOTS // SLOTS_PER_DOT):
        for s in range(g * SLOTS_PER_DOT, (g + 1) * SLOTS_PER_DOT):
            e = ce_ref[i * N_SLOTS + s]
            first = (cj_ref[i * N_SLOTS + s] * CH).astype(F32)
            sel_scr[s * CH:(s + 1) * CH, :] = jnp.where(rank_t_ref[0, e] == row + first, 1.0, 0.0).astype(BF16)
        rows = slice(g * SLOTS_PER_DOT * CH, (g + 1) * SLOTS_PER_DOT * CH)
        stage[par, rows, :] = _dot(sel_scr[rows, :], hb)

    def chunk_copy(s, tile, buf):
        return _chunk_copy(stage.at[buf], s * CH, xs_ref, _aligned(crow_ref[tile * N_SLOTS + s]), CH,
                           sems.at[buf])

    @pl.when(i > 0)
    def _():
        for s in range(N_SLOTS):
            @pl.when(s < tot_ref[i - 1])
            def _():
                chunk_copy(s, i - 1, 1 - par).wait()

    for s in range(N_SLOTS):
        @pl.when(s < tot_ref[i])
        def _():
            chunk_copy(s, i, par).start()

    @pl.when(i == pl.num_programs(0) - 1)
    def _():
        for s in range(N_SLOTS):
            @pl.when(s < tot_ref[i])
            def _():
                chunk_copy(s, i, par).wait()


def _sorted_rows(t):
    n_seg = (t // TM_MOE) * N_EXPERTS
    rows = TOP_K * t + n_seg * (SUBLANES - 1) + N_EXPERTS * (TMG - 1) + N_EXPERTS * TMG
    return -(-rows // TMG) * TMG


def _dispatch(hn, rank_t, plan):
    t = hn.shape[0]
    tm = TM_MOE
    scalars = (plan["ce"], plan["cj"], plan["crow"], plan["tot"], plan["zstart"], plan["zrem"], plan["tail"])
    grid_spec = pltpu.PrefetchScalarGridSpec(
        num_scalar_prefetch=len(scalars),
        grid=(t // tm,),
        in_specs=[pl.BlockSpec((1, N_EXPERTS, 1, tm), lambda i, *_: (i, 0, 0, 0)),
                  pl.BlockSpec((tm, D_MODEL), lambda i, *_: (i, 0))],
        out_specs=pl.BlockSpec(memory_space=pl.ANY),
        scratch_shapes=[pltpu.VMEM((N_SLOTS * CH, tm), BF16),
                        pltpu.VMEM((2, N_SLOTS * CH, D_MODEL), F32),
                        pltpu.VMEM((TMG, D_MODEL), F32),
                        pltpu.SemaphoreType.DMA((2,)), pltpu.SemaphoreType.DMA],
    )
    return pl.pallas_call(
        _dispatch_body,
        grid_spec=grid_spec,
        out_shape=jax.ShapeDtypeStruct((_sorted_rows(t), D_MODEL), F32),
        compiler_params=_cparams(("arbitrary",)),
        name="moe_dispatch",
    )(*scalars, rank_t, hn)


def _ffn_grouped_body(te_ref, live_ref, xs_ref, wg_ref, wu_ref, wd_ref, ys_ref):
    j = pl.program_id(0)

    @pl.when(live_ref[j] != 0)
    def _():
        ys_ref[...] = _swiglu(xs_ref[...].astype(BF16), wg_ref, wu_ref, wd_ref, (0,))

    @pl.when(live_ref[j] == 0)
    def _():
        ys_ref[...] = jnp.zeros_like(ys_ref)


def _ffn_grouped(xs, tile_expert, tile_live, wg, wu, wd):
    n_rows = xs.shape[0]
    row_tile = lambda j, te, live: (j, 0)
    expert = lambda j, te, live: (te[j], 0, 0)
    grid_spec = pltpu.PrefetchScalarGridSpec(
        num_scalar_prefetch=2,
        grid=(n_rows // TMG,),
        in_specs=[pl.BlockSpec((TMG, D_MODEL), row_tile),
                  pl.BlockSpec((1, D_MODEL, D_FF), expert),
                  pl.BlockSpec((1, D_MODEL, D_FF), expert),
                  pl.BlockSpec((1, D_FF, D_MODEL), expert)],
        out_specs=pl.BlockSpec((TMG, D_MODEL), row_tile),
    )
    return pl.pallas_call(
        _ffn_grouped_body,
        grid_spec=grid_spec,
        out_shape=jax.ShapeDtypeStruct((n_rows, D_MODEL), F32),
        compiler_params=_cparams(("arbitrary",)),
        name="moe_ffn",
    )(tile_expert, tile_live, xs, wg, wu, wd)


def _combine_body(ce_ref, cj_ref, crow_ref, tot_ref, tok_ref, h1_ref, gfin_ref, ys_ref,
                  yp_ref, ysm_ref, sel_scr, stage, sems, *, n_prompt_tiles):
    i = pl.program_id(0)
    tm = h1_ref.shape[0]
    par = i % 2

    def fetch(s, tile, buf):
        return _chunk_copy(ys_ref, _aligned(crow_ref[tile * N_SLOTS + s]), stage.at[buf], s * CH, CH,
                           sems.at[buf])

    def fetches(tile, buf, fn):
        for s in range(N_SLOTS):
            @pl.when(s < tot_ref[tile])
            def _():
                fn(fetch(s, tile, buf))

    @pl.when(i == 0)
    def _():
        stage[...] = jnp.zeros_like(stage)
        fetches(0, 0, lambda cp: cp.start())

    @pl.when(i + 1 < pl.num_programs(0))
    def _():
        fetches(i + 1, 1 - par, lambda cp: cp.start())

    tok = tok_ref[...]
    key1, key2 = tok[:, 0:1], tok[:, 1:2]
    gate1, gate2 = tok[:, 2:3], tok[:, 3:4]
    lane = lax.broadcasted_iota(jnp.int32, (1, LANES), 1)
    per_block = LANES // CH
    for blk in range(N_SLOTS // per_block):
        key = jnp.zeros((1, LANES), F32)
        for n in range(per_block):
            s = blk * per_block + n
            first = ce_ref[i * N_SLOTS + s] * KEY_MUL + cj_ref[i * N_SLOTS + s] * CH
            key = jnp.where(lane >= n * CH, (lane - n * CH + first).astype(F32), key)
        sel_scr[:, blk * LANES:(blk + 1) * LANES] = jnp.where(
            key1 == key, gate1, jnp.where(key2 == key, gate2, 0.0)).astype(BF16)

    fetches(i, par, lambda cp: cp.wait())
    ys_tile = stage[par].astype(BF16)
    halves = []
    for r0 in range(0, tm, tm // 2):
        rows = slice(r0, r0 + tm // 2)
        halves.append(_rms(h1_ref[rows, :] + _dot(sel_scr[rows, :], ys_tile), gfin_ref[...]))
    y = jnp.concatenate(halves, axis=0)

    @pl.when(i < n_prompt_tiles)
    def _():
        yp_ref[...] = y

    @pl.when(i >= n_prompt_tiles)
    def _():
        ysm_ref[...] = y


def _combine(ys, tok, h1, gfin, plan, t_prompt):
    t = h1.shape[0]
    tm = TM_MOE
    n_prompt_tiles = t_prompt // tm
    p_map, s_map = _split_rows(n_prompt_tiles)
    scalars = (plan["ce"], plan["cj"], plan["crow"], plan["tot"])
    grid_spec = pltpu.PrefetchScalarGridSpec(
        num_scalar_prefetch=len(scalars),
        grid=(t // tm,),
        in_specs=[pl.BlockSpec((tm, LANES), lambda i, *_: (i, 0)),
                  pl.BlockSpec((tm, D_MODEL), lambda i, *_: (i, 0)),
                  pl.BlockSpec((1, D_MODEL), lambda i, *_: (0, 0)),
                  pl.BlockSpec(memory_space=pl.ANY)],
        out_specs=[pl.BlockSpec((tm, D_MODEL), lambda i, *_: p_map(i)),
                   pl.BlockSpec((tm, D_MODEL), lambda i, *_: s_map(i))],
        scratch_shapes=[pltpu.VMEM((tm, N_SLOTS * CH), BF16),
                        pltpu.VMEM((2, N_SLOTS * CH, D_MODEL), F32),
                        pltpu.SemaphoreType.DMA((2,))],
    )
    return pl.pallas_call(
        functools.partial(_combine_body, n_prompt_tiles=n_prompt_tiles),
        grid_spec=grid_spec,
        out_shape=[jax.ShapeDtypeStruct((t_prompt, D_MODEL), F32),
                   jax.ShapeDtypeStruct((t - t_prompt, D_MODEL), F32)],
        compiler_params=_cparams(("arbitrary",)),
        name="moe_combine",
    )(*scalars, tok, h1, gfin, ys)


def _moe_plan(cnt, n_rows):
    i32 = jnp.int32
    seg = (cnt + SUBLANES - 1) // SUBLANES * SUBLANES
    g_len = jnp.sum(seg, axis=0)
    g_pad = (g_len + TMG - 1) // TMG * TMG
    g_span = g_pad + TMG
    g_end = jnp.cumsum(g_span)
    g_start = g_end - g_span
    seg_start = g_start[None, :] + jnp.cumsum(seg, axis=0) - seg
    n_chunks = (cnt + CH - 1) // CH
    c_end = jnp.cumsum(n_chunks, axis=1)
    tot = c_end[:, -1]
    slot = jnp.arange(N_SLOTS, dtype=i32)[None, :]
    live = slot < tot[:, None]
    ce = jnp.minimum(jnp.sum(slot[:, :, None] >= c_end[:, None, :], axis=-1), N_EXPERTS - 1)
    cj = slot - jnp.take_along_axis(c_end - n_chunks, ce, axis=1)
    cj = jnp.where(live, cj, CJ_NONE)
    crow = jnp.where(live, jnp.take_along_axis(seg_start, ce, axis=1) + CH * cj, 0)
    tile_row = jnp.arange(n_rows // TMG, dtype=i32) * TMG
    tile_expert = jnp.minimum(jnp.sum(tile_row[:, None] >= g_end[None, :], axis=1), N_EXPERTS - 1)
    tile_live = tile_row < (g_start + g_pad)[tile_expert]
    flat = lambda x: x.reshape(-1).astype(i32)
    return dict(ce=flat(ce), cj=flat(cj), crow=flat(crow), tot=flat(tot),
                zstart=flat(g_start + g_len), zrem=flat(g_pad - g_len),
                tail=jnp.stack([g_end[-1], (n_rows - g_end[-1]) // TMG]).astype(i32),
                tile_expert=flat(tile_expert), tile_live=flat(tile_live))


def _moe_routed(h1, hn, tok, rank_t, cnt, wg, wu, wd, gfin, t_prompt):
    plan = _moe_plan(cnt[:, 0, :N_EXPERTS], _sorted_rows(h1.shape[0]))
    xs = _dispatch(hn, rank_t, plan)
    ys = _ffn_grouped(xs, plan["tile_expert"], plan["tile_live"], wg, wu, wd)
    return _combine(ys, tok, h1, gfin, plan, t_prompt)


def _mix_consts(w_spatial, b_spatial, dec_seq):
    tril = jnp.tril(jnp.ones((CHUNK, CHUNK), F32))
    w_p = w_spatial * tril
    reps = CHUNK // dec_seq
    w_s = jnp.stack([jnp.kron(jnp.eye(reps, dtype=F32), w_p[g, :dec_seq, :dec_seq]) for g in range(G_A)])
    wmix = jnp.stack([w_p, w_s]).astype(BF16)
    b_p = jnp.repeat(b_spatial.T, CH_A, axis=1)
    b_s = jnp.tile(b_p[:dec_seq], (reps, 1))
    return wmix, jnp.stack([b_p, b_s])


def kernel(x_prompt, x_sample, state_hgrn, norm_mix_g, w_in, ln_v_g, ln_v_b, w_spatial, b_spatial,
           lower_bounds, hgrn_norm_g, w_branch_a, w_branch_b, w_out, norm_ffn_g, dense_w_gate,
           dense_w_up, dense_w_down, router_w, moe_w_gate, moe_w_up, moe_w_down, final_norm_g):
    n_seq, seq_len, _ = x_prompt.shape
    dec_batch, dec_seq, _ = x_sample.shape
    t_prompt = n_seq * seq_len
    t_sample = dec_batch * dec_seq
    t = t_prompt + t_sample

    lb_cum = jnp.cumsum(jax.nn.softmax(lower_bounds.astype(F32), axis=0), axis=0)
    lb_all = lb_cum - lb_cum[0:1]
    log_lb, log1m_lb, one_m_lb = jnp.log(lb_all), jnp.log1p(-lb_all), 1.0 - lb_all

    row = lambda p: p.reshape(1, -1)
    h_parts = (x_prompt.reshape(t_prompt, D_MODEL), x_sample.reshape(t_sample, D_MODEL))
    state_in = state_hgrn.astype(F32)
    states_p, states_s, v_rows = [], None, []
    for l in range(DEPTH):
        wmix, bmix = _mix_consts(w_spatial[l], b_spatial[l], dec_seq)
        a, v_s, q, lf, k, iv, gs, sa, sb = _in_proj(
            h_parts, row(norm_mix_g[l]), w_in[l].astype(BF16), row(ln_v_g[l]), row(ln_v_b[l]),
            row(log_lb[l]), row(log1m_lb[l]), row(one_m_lb[l]), wmix, bmix, t, t_prompt)
        gn = row(hgrn_norm_g[l])
        o_p, s_p = _hgrn_prompt(q, lf, k, iv, gn, n_seq, seq_len)
        o_s, states_s = _hgrn_sample(q, lf, k, iv, state_in, l, states_s, gn, t_prompt, dec_batch, dec_seq)
        mix_w = (w_branch_a[l].astype(BF16), w_branch_b[l].astype(BF16), w_out[l].astype(BF16),
                 row(norm_ffn_g[l]))
        i = l // 2
        if l % 2 == 0:
            ffn_w = (dense_w_gate[i].astype(BF16), dense_w_up[i].astype(BF16), dense_w_down[i].astype(BF16))
            moe_f32 = (moe_w_gate[i].reshape(-1, D_FF), moe_w_up[i].reshape(-1, D_FF),
                       moe_w_down[i].reshape(-1, D_MODEL))
            h_next, *moe_bf16 = _post_mix(h_parts, a, o_p, o_s, gs, sa, sb, *mix_w, ffn_w, t_prompt,
                                          "dense", moe_f32)
            h_parts = (h_next,)
        else:
            rw = jnp.pad(router_w[i], ((0, 0), (0, LANES - N_EXPERTS)))
            rw_hi = rw.astype(BF16)
            rw_lo = (rw - rw_hi.astype(F32)).astype(BF16)
            h1, hn, tok, rank_t, cnt = _post_mix(h_parts, a, o_p, o_s, gs, sa, sb, *mix_w,
                                                 (rw_hi, rw_lo), t_prompt, "router")
            h_parts = _moe_routed(h1, hn, tok, rank_t, cnt, moe_bf16[0].reshape(moe_w_gate[i].shape),
                                  moe_bf16[1].reshape(moe_w_up[i].shape),
                                  moe_bf16[2].reshape(moe_w_down[i].shape), row(final_norm_g), t_prompt)
        states_p.append(s_p)
        v_rows.append(v_s.reshape(dec_batch, dec_seq, D_A))

    y_prompt = h_parts[0].reshape(n_seq, seq_len, D_MODEL)
    y_sample = h_parts[1].reshape(dec_batch, dec_seq, D_MODEL)
    return (y_prompt, y_sample, jnp.stack(states_p).astype(x_prompt.dtype),
            states_s.astype(state_hgrn.dtype), jnp.stack(v_rows))
```

```python
import functools

import numpy as np
import jax
import jax.numpy as jnp
from jax import lax
from jax.experimental import pallas as pl
from jax.experimental.pallas import tpu as pltpu

F32 = jnp.float32
BF16 = jnp.bfloat16

D_MODEL = 1024
DEPTH = 2
D_A = 512
G_A = 4
CH_A = 128
CHUNK = 128
D_B = 512
H_B = 4
DK = 128
IN_COLS = 5120
D_FF = 2816
N_EXPERTS = 8
TOP_K = 2
EPS = 1e-6

LANES = 128
SUBLANES = 8
VMEM_LIMIT = 52 * 1024 * 1024

TM_IN = 512
SUB_IN = 256
TM_PROJ = 256
CAST_STEPS = 64
TM_MOE = 512
TMG = 256
CH = 64
N_SLOTS = -(-(TOP_K * TM_MOE + N_EXPERTS * (CH - 1)) // CH)
SLOTS_PER_DOT = 8
KEY_MUL = 4096
CJ_NONE = N_SLOTS
ZERO_BITS = (128, 64, 32, 16, 8)
R_HGRN = 512
BLK = 128
LEVEL_HALVES = (64, 32, 16, 8)
SEQ_PER_STEP = 8


def _cparams(sem):
    return pltpu.CompilerParams(dimension_semantics=sem, vmem_limit_bytes=VMEM_LIMIT)


def _dot(a, b):
    return jnp.dot(a, b, preferred_element_type=F32)


def _dot_nt(a, b):
    return lax.dot_general(a, b, (((1,), (1,)), ((), ())), preferred_element_type=F32)


def _split(x, terms):
    out = []
    for _ in range(terms - 1):
        hi = x.astype(BF16)
        out.append(hi)
        x = x - hi.astype(F32)
    out.append(x.astype(BF16))
    return out


def _rms(x, g):
    return x * lax.rsqrt(jnp.mean(x * x, axis=-1, keepdims=True) + EPS) * g


def _gelu(x):
    return 0.5 * x * (1.0 + lax.erf(x * np.float32(2.0 ** -0.5)))


def _sigmoid(x):
    return jax.nn.sigmoid(x)


def _split_rows(n_prompt_tiles):
    return (lambda i: (jnp.minimum(i, n_prompt_tiles - 1), 0),
            lambda i: (jnp.maximum(i - n_prompt_tiles, 0), 0))


def _select_rows(parts, n_prompt_tiles):
    if len(parts) == 1:
        return parts[0][...]
    return jnp.where(pl.program_id(0) < n_prompt_tiles, parts[0][...], parts[1][...])


def _softplus_neg_abs(x):
    return jnp.log(1.0 + jnp.exp(-jnp.abs(x)))


def _in_proj_body(*refs, n_x, n_prompt_tiles):
    x_parts, refs = refs[:n_x], refs[n_x:]
    (g_ref, w_ref, lng_ref, lnb_ref, la_ref, l1m_ref, oml_ref, wmix_ref, bmix_ref,
     a_ref, vs_ref, q_ref, lf_ref, k_ref, iv_ref, gs_ref, sa_ref, sb_ref) = refs
    tm = a_ref.shape[0]
    gate_a, gate_b = 2 * D_A + 4 * D_B, 2 * D_A + 4 * D_B + D_MODEL
    half = D_MODEL // 2
    for r0 in range(0, tm, SUB_IN):
        rs = slice(r0, r0 + SUB_IN)
        xn = _rms(_select_rows([p.at[rs] for p in x_parts], n_prompt_tiles), g_ref[...]).astype(BF16)

        def seg(lo, n):
            return _dot(xn, w_ref[:, lo:lo + n])

        u = _gelu(seg(0, D_A))
        iv_ref[rs, :] = seg(2 * D_A + 2 * D_B, D_B)
        v = _gelu(seg(D_A, D_A))
        vc = v - jnp.mean(v, axis=-1, keepdims=True)
        v = vc * lax.rsqrt(jnp.mean(vc * vc, axis=-1, keepdims=True) + EPS) * lng_ref[...] + lnb_ref[...]
        vs_ref[rs, :] = v
        sa_ref[rs, :half] = _sigmoid(seg(gate_a, half)).astype(BF16)

        vb = v.astype(BF16)
        for c in range(SUB_IN // CHUNK):
            rows = slice(c * CHUNK, (c + 1) * CHUNK)
            out_rows = slice(r0 + c * CHUNK, r0 + (c + 1) * CHUNK)
            for g in range(G_A):
                cols = slice(g * CH_A, (g + 1) * CH_A)
                mixed = _dot(wmix_ref[0, g], vb[rows, cols]) + bmix_ref[0, :, cols]
                a_ref[out_rows, cols] = (u[rows, cols] * mixed).astype(BF16)

        fz = seg(2 * D_A + D_B, D_B)
        log_sig = jnp.minimum(fz, 0.0) - _softplus_neg_abs(fz)
        b = l1m_ref[...] + log_sig
        la = la_ref[...]
        lf_ref[rs, :] = jnp.maximum(la, b) + _softplus_neg_abs(la - b)
        k_ref[rs, :] = oml_ref[...] * _sigmoid(-fz)
        sa_ref[rs, half:] = _sigmoid(seg(gate_a + half, half)).astype(BF16)
        zq = seg(2 * D_A, D_B)
        q_ref[rs, :] = zq * _sigmoid(zq)
        sb_ref[rs, :half] = _sigmoid(seg(gate_b, half)).astype(BF16)
        gz = seg(2 * D_A + 3 * D_B, D_B)
        gs_ref[rs, :] = (gz * _sigmoid(gz)).astype(BF16)
        sb_ref[rs, half:] = _sigmoid(seg(gate_b + half, half)).astype(BF16)


def _in_proj(x_parts, g, w, lng, lnb, la, l1m, oml, wmix, bmix, t, t_prompt):
    tm = TM_IN
    n_prompt_tiles = t_prompt // tm
    row = lambda n: pl.BlockSpec((1, n), lambda i: (0, 0))
    tile = lambda n: pl.BlockSpec((tm, n), lambda i: (i, 0))
    sel = lambda i: jnp.minimum(i // n_prompt_tiles, 1)
    p_map, s_map = _split_rows(n_prompt_tiles)
    if len(x_parts) == 1:
        x_specs = [tile(D_MODEL)]
    else:
        x_specs = [pl.BlockSpec((tm, D_MODEL), p_map), pl.BlockSpec((tm, D_MODEL), s_map)]
    t_sample = t - n_prompt_tiles * tm
    shape = lambda rows, n, dt=F32: jax.ShapeDtypeStruct((rows, n), dt)
    out_shapes = ([shape(t, D_A, BF16), shape(t_sample, D_A)] + [shape(t, D_A)] * 4
                  + [shape(t, D_A, BF16)] + [shape(t, D_MODEL, BF16)] * 2)
    out_specs = ([tile(D_A), pl.BlockSpec((tm, D_A), s_map)] + [tile(D_A)] * 5 + [tile(D_MODEL)] * 2)
    return pl.pallas_call(
        functools.partial(_in_proj_body, n_x=len(x_parts), n_prompt_tiles=n_prompt_tiles),
        grid=(t // tm,),
        in_specs=x_specs + [row(D_MODEL),
                            pl.BlockSpec((D_MODEL, IN_COLS), lambda i: (0, 0), pipeline_mode=pl.Buffered(1)),
                            row(D_A), row(D_A), row(D_B), row(D_B), row(D_B),
                            pl.BlockSpec((1, G_A, CHUNK, CHUNK), lambda i: (sel(i), 0, 0, 0)),
                            pl.BlockSpec((1, CHUNK, D_A), lambda i: (sel(i), 0, 0))],
        out_specs=out_specs,
        out_shape=out_shapes,
        compiler_params=_cparams(("arbitrary",)),
        name="in_proj",
    )(*x_parts, g, w, lng, lnb, la, l1m, oml, wmix, bmix)


def _hgrn_block(q_ref, k_ref, v_ref, g_scr, masks_ref, st, r0, cols, lane, row):
    rows = slice(r0, r0 + BLK)
    qh, kh, vh = q_ref[rows, cols], k_ref[rows, cols], v_ref[rows, cols]
    gh = g_scr[rows, cols]
    g_tot = g_scr[r0 + BLK - 1:r0 + BLK, cols]

    o = _dot_nt((qh * jnp.exp(gh)).astype(BF16), st.astype(BF16))

    sc = None
    for l, m in enumerate(LEVEL_HALVES):
        refs = [jnp.broadcast_to(g_scr[r0 + b + m - 1:r0 + b + m, cols], (2 * m, DK))
                for b in range(0, BLK, 2 * m)]
        g_ref_rows = refs[0] if len(refs) == 1 else jnp.concatenate(refs, axis=0)
        right = (row & m) != 0
        x = jnp.where(right, gh - g_ref_rows, g_ref_rows - gh)
        z = (jnp.where(right, qh, kh) * jnp.exp(x)).astype(BF16)
        part = _dot_nt(z, z) * masks_ref[l]
        sc = part if sc is None else sc + part

    strips = []
    for g0 in range(0, BLK, SUBLANES):
        qg, gg = qh[g0:g0 + SUBLANES], gh[g0:g0 + SUBLANES]
        strip = jnp.zeros((SUBLANES, BLK), F32)
        for s in range(SUBLANES):
            src = r0 + g0 + s
            p = qg * jnp.exp(gg - g_scr[src:src + 1, cols]) * k_ref[src:src + 1, cols]
            strip = jnp.where(lane[:SUBLANES] == g0 + s, jnp.sum(p, axis=-1, keepdims=True), strip)
        strips.append(strip)
    diag = jnp.concatenate(strips, axis=0)
    sc = sc + jnp.where(lane <= row, diag, 0.0)

    o = o + _dot(sc.astype(BF16), vh.astype(BF16))
    kt = (kh * jnp.exp(g_tot - gh)).astype(BF16)
    st_new = st * jnp.exp(g_tot) + _dot(vh.T.astype(BF16), kt)
    return o, st_new


def _hgrn_prompt_body(q_ref, lf_ref, k_ref, v_ref, tri_ref, masks_ref, gn_ref, o_ref, sfin_ref,
                      st_ref, g_scr):
    j = pl.program_id(1)
    r = q_ref.shape[0]

    @pl.when(j == 0)
    def _():
        st_ref[...] = jnp.zeros_like(st_ref)

    tri = tri_ref[...]
    g_scr[...] = sum(_dot(tri, part) for part in _split(lf_ref[...], 2))
    lane = lax.broadcasted_iota(jnp.int32, (BLK, BLK), 1)
    row = lax.broadcasted_iota(jnp.int32, (BLK, BLK), 0)

    states = [st_ref[h] for h in range(H_B)]
    for r0 in range(0, r, BLK):
        for h in range(H_B):
            cols = slice(h * DK, (h + 1) * DK)
            o, states[h] = _hgrn_block(q_ref, k_ref, v_ref, g_scr, masks_ref, states[h], r0, cols, lane, row)
            o_ref[r0:r0 + BLK, cols] = _rms(o, gn_ref[:, cols])
    for h in range(H_B):
        st_ref[h] = states[h]

    @pl.when(j == pl.num_programs(1) - 1)
    def _():
        for h in range(H_B):
            sfin_ref[0, h] = st_ref[h].T


def _hgrn_consts():
    i = np.arange(R_HGRN)[:, None]
    j = np.arange(R_HGRN)[None, :]
    tri = ((i // BLK) == (j // BLK)) & (j <= i)
    i, j = np.arange(BLK)[:, None], np.arange(BLK)[None, :]
    masks = [((i // (2 * m)) == (j // (2 * m))) & ((i & m) != 0) & ((j & m) == 0) for m in LEVEL_HALVES]
    return (jnp.asarray(tri.astype(np.float32), dtype=BF16),
            jnp.asarray(np.stack(masks).astype(np.float32)))


def _hgrn_prompt(q, lf, k, iv, gn, n_seq, seq_len):
    r = R_HGRN
    nblk = seq_len // r
    tri, masks = _hgrn_consts()
    blk = pl.BlockSpec((r, D_B), lambda b, j: (b * nblk + j, 0))
    return pl.pallas_call(
        _hgrn_prompt_body,
        grid=(n_seq, nblk),
        in_specs=[blk, blk, blk, blk,
                  pl.BlockSpec(tri.shape, lambda b, j: (0, 0)),
                  pl.BlockSpec(masks.shape, lambda b, j: (0, 0, 0)),
                  pl.BlockSpec((1, D_B), lambda b, j: (0, 0))],
        out_specs=[blk, pl.BlockSpec((1, H_B, DK, DK), lambda b, j: (b, 0, 0, 0))],
        out_shape=[jax.ShapeDtypeStruct((n_seq * seq_len, D_B), F32),
                   jax.ShapeDtypeStruct((n_seq, H_B, DK, DK), F32)],
        scratch_shapes=[pltpu.VMEM((H_B, DK, DK), F32),
                        pltpu.VMEM((r, D_B), F32)],
        compiler_params=_cparams(("arbitrary", "arbitrary")),
        name="hgrn_prompt",
    )(q, lf, k, iv, tri, masks, gn)


def _sub_chunk_exact(q_ref, k_ref, v_ref, g_ref, r0, cols, sub):
    qg = q_ref[pl.ds(r0, sub), cols]
    gg = g_ref[pl.ds(r0, sub), cols]
    kg = k_ref[pl.ds(r0, sub), cols]
    vg = v_ref[pl.ds(r0, sub), cols]
    row = lax.broadcasted_iota(jnp.int32, (sub, 1), 0)
    acc = jnp.zeros((sub, DK), F32)
    for s in range(sub):
        kb = kg[s:s + 1, :]
        gb = gg[s:s + 1, :]
        vb = vg[s:s + 1, :]
        p = qg * jnp.exp(gg - gb) * kb
        rs = jnp.sum(p, axis=-1, keepdims=True)
        acc = acc + jnp.where(row >= s, rs, 0.0) * vb
    return acc


def _hgrn_sample_body(q_ref, lf_ref, k_ref, v_ref, s0_ref, cm_ref, gn_ref, *rest, dec_seq, out_layer):
    o_ref, s1_ref, g_scr, o_scr = rest[-4:]
    rows = q_ref.shape[0]
    n_seq = rows // dec_seq
    for later in range(s1_ref.shape[0]):
        if later != out_layer:
            s1_ref[later] = jnp.zeros(s1_ref.shape[1:], F32)
    parts = _split(lf_ref[...], 3)

    def cum(idx):
        m = cm_ref[idx]
        return _dot(m, parts[0]) + _dot(m, parts[1]) + _dot(m, parts[2])

    g_all = cum(0)
    g_scr[...] = g_all
    g_rev = cum(1)
    g_tot = cum(2)
    lane = lax.broadcasted_iota(jnp.int32, (DK, rows), 1)

    for h in range(H_B):
        cols = slice(h * DK, (h + 1) * DK)
        qt = (q_ref[:, cols] * jnp.exp(g_all[:, cols])).astype(BF16)
        kt_t = (k_ref[:, cols] * jnp.exp(g_rev[:, cols])).T
        dec_t = jnp.exp(g_tot[:, cols]).T
        vb = v_ref[:, cols].astype(BF16)
        for n in range(n_seq):
            r0 = n * dec_seq
            s0 = s0_ref[0, n, h]
            o_scr[r0:r0 + dec_seq, cols] = _dot(qt[r0:r0 + dec_seq, :], s0.astype(BF16))
            in_seq = (lane >= r0) & (lane < r0 + dec_seq)
            u = _dot(jnp.where(in_seq, kt_t, 0.0).astype(BF16), vb)
            s1_ref[out_layer, n, h] = s0 * dec_t[:, r0:r0 + 1] + u

    def group(n, carry):
        r0 = pl.multiple_of(n * dec_seq, dec_seq)
        for h in range(H_B):
            cols = slice(h * DK, (h + 1) * DK)
            o_scr[pl.ds(r0, dec_seq), cols] += _sub_chunk_exact(q_ref, k_ref, v_ref, g_scr, r0, cols,
                                                                 dec_seq)
        return carry

    lax.fori_loop(0, n_seq, group, 0)

    for h in range(H_B):
        cols = slice(h * DK, (h + 1) * DK)
        o_ref[:, cols] = _rms(o_scr[:, cols], gn_ref[:, cols])


def _hgrn_sample(q, lf, k, iv, state_in, layer, states_out, gn, row0, n_seq, dec_seq):
    rows = SEQ_PER_STEP * dec_seq
    i = np.arange(rows)[:, None]
    j = np.arange(rows)[None, :]
    same = (i // dec_seq) == (j // dec_seq)
    cm = jnp.asarray(np.stack([same & (j <= i), same & (j > i), same]).astype(np.float32), dtype=BF16)
    blk0 = row0 // rows
    blk = pl.BlockSpec((rows, D_B), lambda n: (blk0 + n, 0))
    oblk = pl.BlockSpec((rows, D_B), lambda n: (n, 0))
    depth = state_in.shape[0]
    in_specs = [blk, blk, blk, blk,
                pl.BlockSpec((1, SEQ_PER_STEP, H_B, DK, DK), lambda n: (layer, n, 0, 0, 0)),
                pl.BlockSpec(cm.shape, lambda n: (0, 0, 0)),
                pl.BlockSpec((1, D_B), lambda n: (0, 0))]
    operands = [q, lf, k, iv, state_in, cm, gn]
    if states_out is None:
        sblk = pl.BlockSpec((depth, SEQ_PER_STEP, H_B, DK, DK), lambda n: (0, n, 0, 0, 0))
        out_layer, aliases = layer, {}
    else:
        sblk = pl.BlockSpec((1, SEQ_PER_STEP, H_B, DK, DK), lambda n: (layer, n, 0, 0, 0))
        out_layer, aliases = 0, {len(operands): 1}
        in_specs.append(pl.BlockSpec(memory_space=pl.ANY))
        operands.append(states_out)
    return pl.pallas_call(
        functools.partial(_hgrn_sample_body, dec_seq=dec_seq, out_layer=out_layer),
        grid=(n_seq // SEQ_PER_STEP,),
        in_specs=in_specs,
        out_specs=[oblk, sblk],
        out_shape=[jax.ShapeDtypeStruct((n_seq * dec_seq, D_B), F32),
                   jax.ShapeDtypeStruct((depth, n_seq, H_B, DK, DK), F32)],
        scratch_shapes=[pltpu.VMEM((rows, D_B), F32), pltpu.VMEM((rows, D_B), F32)],
        input_output_aliases=aliases,
        compiler_params=_cparams(("arbitrary",)),
        name="hgrn_sample",
    )(*operands)


FF_SPLITS = ((0, 1536), (1536, 1280))


def _swiglu(hb, wg_ref, wu_ref, wd_ref, idx):
    acc = None
    for lo, n in FF_SPLITS:
        g = _dot(hb, wg_ref[idx + (slice(None), slice(lo, lo + n))])
        u = _dot(hb, wu_ref[idx + (slice(None), slice(lo, lo + n))])
        act = (g * _sigmoid(g) * u).astype(BF16)
        part = _dot(act, wd_ref[idx + (slice(lo, lo + n), slice(None))])
        acc = part if acc is None else acc + part
    return acc


def _route(hn, wh_ref, wl_ref, tok_ref, rank_t_ref, cnt_ref):
    tm = hn.shape[0]
    hh = hn.astype(BF16)
    hl = (hn - hh.astype(F32)).astype(BF16)
    logits = _dot(hh, wh_ref[...]) + _dot(hh, wl_ref[...]) + _dot(hl, wh_ref[...])
    lane = lax.broadcasted_iota(jnp.int32, logits.shape, 1).astype(F32)
    neg = np.float32(-np.inf)
    logits = jnp.where(lane < N_EXPERTS, logits, neg)
    m1 = jnp.max(logits, axis=-1, keepdims=True)
    i1 = jnp.min(jnp.where(logits == m1, lane, float(LANES)), axis=-1, keepdims=True)
    rest = jnp.where(lane == i1, neg, logits)
    m2 = jnp.max(rest, axis=-1, keepdims=True)
    i2 = jnp.min(jnp.where(rest == m2, lane, float(LANES)), axis=-1, keepdims=True)
    e2 = jnp.exp(m2 - m1)
    den = 1.0 + e2

    routed = (lane == i1) | (lane == i2)
    sel = jnp.where(routed, 1.0, 0.0)
    r = lax.broadcasted_iota(jnp.int32, (tm, tm), 0)
    c = lax.broadcasted_iota(jnp.int32, (tm, tm), 1)
    before = jnp.where(c < r, 1.0, 0.0).astype(BF16)
    rank = _dot(before, sel.astype(BF16))
    rank1 = jnp.sum(jnp.where(lane == i1, rank, 0.0), axis=-1, keepdims=True)
    rank2 = jnp.sum(jnp.where(lane == i2, rank, 0.0), axis=-1, keepdims=True)
    tok_ref[...] = jnp.where(lane == 0.0, i1 * KEY_MUL + rank1,
                             jnp.where(lane == 1.0, i2 * KEY_MUL + rank2,
                                       jnp.where(lane == 2.0, 1.0 / den,
                                                 jnp.where(lane == 3.0, e2 / den, 0.0))))
    rank_t = jnp.where(routed, rank, -1.0).T
    for e in range(N_EXPERTS):
        rank_t_ref[0, e] = rank_t[e:e + 1, :]
    cnt_ref[0] = jnp.sum(sel, axis=0, keepdims=True).astype(jnp.int32)


def _post_mix_body(*refs, n_h, n_prompt_tiles, mode):
    h_parts, refs = refs[:n_h], refs[n_h:]
    a_ref, op_ref, os_ref, gs_ref, sa_ref, sb_ref, wa_ref, wb_ref, wo_ref, gf_ref = refs[:10]
    tm = a_ref.shape[0]
    h1s, hns = [], []
    for r0 in range(0, tm, TM_PROJ):
        rows = slice(r0, r0 + TM_PROJ)
        o = _select_rows((op_ref.at[rows], os_ref.at[rows]), n_prompt_tiles)
        pa = _dot(a_ref[rows, :], wa_ref[...])
        pb = _dot((o * gs_ref[rows, :].astype(F32)).astype(BF16), wb_ref[...])
        merged = sa_ref[rows, :].astype(F32) * pa + sb_ref[rows, :].astype(F32) * pb
        h1s.append(_select_rows([p.at[rows] for p in h_parts], n_prompt_tiles)
                   + _dot(merged.astype(BF16), wo_ref[...]))
        hns.append(_rms(h1s[-1], gf_ref[...]))
    h1 = h1s[0] if len(h1s) == 1 else jnp.concatenate(h1s, axis=0)
    hn = hns[0] if len(hns) == 1 else jnp.concatenate(hns, axis=0)
    if mode == "dense":
        wg_ref, wu_ref, wd_ref = refs[10:13]
        n_cast = (len(refs) - 14) // 2
        cast_in, out_ref, cast_out = refs[13:13 + n_cast], refs[13 + n_cast], refs[14 + n_cast:]
        out_ref[...] = h1 + _swiglu(hn.astype(BF16), wg_ref, wu_ref, wd_ref, ())
        for src_ref, dst_ref in zip(cast_in, cast_out):
            dst_ref[...] = src_ref[...].astype(BF16)
    else:
        wh_ref, wl_ref, h1_ref, hn_ref, tok_ref, rank_t_ref, cnt_ref = refs[10:]
        h1_ref[...] = h1
        hn_ref[...] = hn.astype(BF16)
        _route(hn, wh_ref, wl_ref, tok_ref, rank_t_ref, cnt_ref)


def _post_mix(h_parts, a, o_p, o_s, gs, sa, sb, wa, wb, wo, gf, extra_w, t_prompt, mode, to_cast=()):
    t = a.shape[0]
    tm = TM_PROJ if mode == "dense" else TM_MOE
    n_prompt_tiles = t_prompt // tm
    assert not to_cast or t // tm >= CAST_STEPS
    tile = lambda n: pl.BlockSpec((tm, n), lambda i: (i, 0))
    cast_spec = lambda w: pl.BlockSpec((w.shape[0] // CAST_STEPS, w.shape[1]),
                                       lambda i: (jnp.minimum(i, CAST_STEPS - 1), 0))
    full = lambda w: pl.BlockSpec(w.shape, lambda i: (0,) * w.ndim, pipeline_mode=pl.Buffered(1))
    p_map, s_map = _split_rows(n_prompt_tiles)
    if len(h_parts) == 1:
        h_specs = [tile(D_MODEL)]
    else:
        h_specs = [pl.BlockSpec((tm, D_MODEL), p_map), pl.BlockSpec((tm, D_MODEL), s_map)]
    act = jax.ShapeDtypeStruct((t, D_MODEL), F32)
    if mode == "dense":
        out_specs = [tile(D_MODEL)] + [cast_spec(w) for w in to_cast]
        out_shape = [act] + [jax.ShapeDtypeStruct(w.shape, BF16) for w in to_cast]
        scratch = []
    else:
        out_specs = [tile(D_MODEL), tile(D_MODEL), tile(LANES),
                     pl.BlockSpec((1, N_EXPERTS, 1, tm), lambda i: (i, 0, 0, 0)),
                     pl.BlockSpec((1, 1, LANES), lambda i: (i, 0, 0))]
        out_shape = [act, jax.ShapeDtypeStruct((t, D_MODEL), BF16),
                     jax.ShapeDtypeStruct((t, LANES), F32),
                     jax.ShapeDtypeStruct((t // tm, N_EXPERTS, 1, tm), F32),
                     jax.ShapeDtypeStruct((t // tm, 1, LANES), jnp.int32)]
        scratch = []
    weights = (wa, wb, wo, gf) + tuple(extra_w)
    return pl.pallas_call(
        functools.partial(_post_mix_body, n_h=len(h_parts), n_prompt_tiles=n_prompt_tiles, mode=mode),
        grid=(t // tm,),
        in_specs=h_specs + [tile(D_A),
                            pl.BlockSpec((tm, D_B), p_map), pl.BlockSpec((tm, D_B), s_map),
                            tile(D_B), tile(D_MODEL), tile(D_MODEL)] + [full(w) for w in weights]
        + [cast_spec(w) for w in to_cast],
        out_specs=out_specs,
        out_shape=out_shape,
        scratch_shapes=scratch,
        compiler_params=_cparams(("arbitrary",)),
        name="post_mix_" + mode,
    )(*h_parts, a, o_p, o_s, gs, sa, sb, *weights, *to_cast)


def _chunk_copy(src_ref, src_row, dst_ref, dst_row, n, sem):
    return pltpu.make_async_copy(src_ref.at[pl.ds(src_row, n)], dst_ref.at[pl.ds(dst_row, n)], sem)


def _aligned(row):
    return pl.multiple_of(row, SUBLANES)


def _dispatch_body(ce_ref, cj_ref, crow_ref, tot_ref, zstart_ref, zrem_ref, tail_ref,
                   rank_t_ref, hn_ref, xs_ref, sel_scr, stage, zero_ref, sems, zsem):
    i = pl.program_id(0)
    tm = hn_ref.shape[0]
    par = i % 2

    def fills(fn):
        for e in range(N_EXPERTS):
            s = zstart_ref[e]
            fn(_chunk_copy(zero_ref, 0, xs_ref, _aligned(s), TMG, zsem))
            rem = zrem_ref[e]
            for b in ZERO_BITS:
                @pl.when((rem & b) != 0)
                def _():
                    fn(_chunk_copy(zero_ref, 0, xs_ref, _aligned(s + TMG + (rem & ~(2 * b - 1))), b, zsem))

        def tail(n, carry):
            fn(_chunk_copy(zero_ref, 0, xs_ref, _aligned(tail_ref[0] + n * TMG), TMG, zsem))
            return carry

        lax.fori_loop(0, tail_ref[1], tail, 0)

    @pl.when(i == 0)
    def _():
        zero_ref[...] = jnp.zeros_like(zero_ref)
        fills(lambda cp: cp.start())
        fills(lambda cp: cp.wait())

    row = lax.broadcasted_iota(jnp.int32, (CH, tm), 0).astype(F32)
    hb = hn_ref[...]
    for g in range(N_SLOTS // SLOTS_PER_DOT):
        for s in range(g * SLOTS_PER_DOT, (g + 1) * SLOTS_PER_DOT):
            e = ce_ref[i * N_SLOTS + s]
            first = (cj_ref[i * N_SLOTS + s] * CH).astype(F32)
            sel_scr[s * CH:(s + 1) * CH, :] = jnp.where(rank_t_ref[0, e] == row + first, 1.0, 0.0).astype(BF16)
        rows = slice(g * SLOTS_PER_DOT * CH, (g + 1) * SLOTS_PER_DOT * CH)
        stage[par, rows, :] = _dot(sel_scr[rows, :], hb)

    def chunk_copy(s, tile, buf):
        return _chunk_copy(stage.at[buf], s * CH, xs_ref, _aligned(crow_ref[tile * N_SLOTS + s]), CH,
                           sems.at[buf])

    @pl.when(i > 0)
    def _():
        for s in range(N_SLOTS):
            @pl.when(s < tot_ref[i - 1])
            def _():
                chunk_copy(s, i - 1, 1 - par).wait()

    for s in range(N_SLOTS):
        @pl.when(s < tot_ref[i])
        def _():
            chunk_copy(s, i, par).start()

    @pl.when(i == pl.num_programs(0) - 1)
    def _():
        for s in range(N_SLOTS):
            @pl.when(s < tot_ref[i])
            def _():
                chunk_copy(s, i, par).wait()


def _sorted_rows(t):
    n_seg = (t // TM_MOE) * N_EXPERTS
    rows = TOP_K * t + n_seg * (SUBLANES - 1) + N_EXPERTS * (TMG - 1) + N_EXPERTS * TMG
    return -(-rows // TMG) * TMG


def _dispatch(hn, rank_t, plan):
    t = hn.shape[0]
    tm = TM_MOE
    scalars = (plan["ce"], plan["cj"], plan["crow"], plan["tot"], plan["zstart"], plan["zrem"], plan["tail"])
    grid_spec = pltpu.PrefetchScalarGridSpec(
        num_scalar_prefetch=len(scalars),
        grid=(t // tm,),
        in_specs=[pl.BlockSpec((1, N_EXPERTS, 1, tm), lambda i, *_: (i, 0, 0, 0)),
                  pl.BlockSpec((tm, D_MODEL), lambda i, *_: (i, 0))],
        out_specs=pl.BlockSpec(memory_space=pl.ANY),
        scratch_shapes=[pltpu.VMEM((N_SLOTS * CH, tm), BF16),
                        pltpu.VMEM((2, N_SLOTS * CH, D_MODEL), F32),
                        pltpu.VMEM((TMG, D_MODEL), F32),
                        pltpu.SemaphoreType.DMA((2,)), pltpu.SemaphoreType.DMA],
    )
    return pl.pallas_call(
        _dispatch_body,
        grid_spec=grid_spec,
        out_shape=jax.ShapeDtypeStruct((_sorted_rows(t), D_MODEL), F32),
        compiler_params=_cparams(("arbitrary",)),
        name="moe_dispatch",
    )(*scalars, rank_t, hn)


def _ffn_grouped_body(te_ref, live_ref, xs_ref, wg_ref, wu_ref, wd_ref, ys_ref):
    j = pl.program_id(0)

    @pl.when(live_ref[j] != 0)
    def _():
        ys_ref[...] = _swiglu(xs_ref[...].astype(BF16), wg_ref, wu_ref, wd_ref, (0,))

    @pl.when(live_ref[j] == 0)
    def _():
        ys_ref[...] = jnp.zeros_like(ys_ref)


def _ffn_grouped(xs, tile_expert, tile_live, wg, wu, wd):
    n_rows = xs.shape[0]
    row_tile = lambda j, te, live: (j, 0)
    expert = lambda j, te, live: (te[j], 0, 0)
    grid_spec = pltpu.PrefetchScalarGridSpec(
        num_scalar_prefetch=2,
        grid=(n_rows // TMG,),
        in_specs=[pl.BlockSpec((TMG, D_MODEL), row_tile),
                  pl.BlockSpec((1, D_MODEL, D_FF), expert),
                  pl.BlockSpec((1, D_MODEL, D_FF), expert),
                  pl.BlockSpec((1, D_FF, D_MODEL), expert)],
        out_specs=pl.BlockSpec((TMG, D_MODEL), row_tile),
    )
    return pl.pallas_call(
        _ffn_grouped_body,
        grid_spec=grid_spec,
        out_shape=jax.ShapeDtypeStruct((n_rows, D_MODEL), F32),
        compiler_params=_cparams(("arbitrary",)),
        name="moe_ffn",
    )(tile_expert, tile_live, xs, wg, wu, wd)


def _combine_body(ce_ref, cj_ref, crow_ref, tot_ref, tok_ref, h1_ref, gfin_ref, ys_ref,
                  yp_ref, ysm_ref, sel_scr, stage, sems, *, n_prompt_tiles):
    i = pl.program_id(0)
    tm = h1_ref.shape[0]
    par = i % 2

    def fetch(s, tile, buf):
        return _chunk_copy(ys_ref, _aligned(crow_ref[tile * N_SLOTS + s]), stage.at[buf], s * CH, CH,
                           sems.at[buf])

    def fetches(tile, buf, fn):
        for s in range(N_SLOTS):
            @pl.when(s < tot_ref[tile])
            def _():
                fn(fetch(s, tile, buf))

    @pl.when(i == 0)
    def _():
        stage[...] = jnp.zeros_like(stage)
        fetches(0, 0, lambda cp: cp.start())

    @pl.when(i + 1 < pl.num_programs(0))
    def _():
        fetches(i + 1, 1 - par, lambda cp: cp.start())

    tok = tok_ref[...]
    key1, key2 = tok[:, 0:1], tok[:, 1:2]
    gate1, gate2 = tok[:, 2:3], tok[:, 3:4]
    lane = lax.broadcasted_iota(jnp.int32, (1, LANES), 1)
    per_block = LANES // CH
    for blk in range(N_SLOTS // per_block):
        key = jnp.zeros((1, LANES), F32)
        for n in range(per_block):
            s = blk * per_block + n
            first = ce_ref[i * N_SLOTS + s] * KEY_MUL + cj_ref[i * N_SLOTS + s] * CH
            key = jnp.where(lane >= n * CH, (lane - n * CH + first).astype(F32), key)
        sel_scr[:, blk * LANES:(blk + 1) * LANES] = jnp.where(
            key1 == key, gate1, jnp.where(key2 == key, gate2, 0.0)).astype(BF16)

    fetches(i, par, lambda cp: cp.wait())
    ys_tile = stage[par].astype(BF16)
    halves = []
    for r0 in range(0, tm, tm // 2):
        rows = slice(r0, r0 + tm // 2)
        halves.append(_rms(h1_ref[rows, :] + _dot(sel_scr[rows, :], ys_tile), gfin_ref[...]))
    y = jnp.concatenate(halves, axis=0)

    @pl.when(i < n_prompt_tiles)
    def _():
        yp_ref[...] = y

    @pl.when(i >= n_prompt_tiles)
    def _():
        ysm_ref[...] = y


def _combine(ys, tok, h1, gfin, plan, t_prompt):
    t = h1.shape[0]
    tm = TM_MOE
    n_prompt_tiles = t_prompt // tm
    p_map, s_map = _split_rows(n_prompt_tiles)
    scalars = (plan["ce"], plan["cj"], plan["crow"], plan["tot"])
    grid_spec = pltpu.PrefetchScalarGridSpec(
        num_scalar_prefetch=len(scalars),
        grid=(t // tm,),
        in_specs=[pl.BlockSpec((tm, LANES), lambda i, *_: (i, 0)),
                  pl.BlockSpec((tm, D_MODEL), lambda i, *_: (i, 0)),
                  pl.BlockSpec((1, D_MODEL), lambda i, *_: (0, 0)),
                  pl.BlockSpec(memory_space=pl.ANY)],
        out_specs=[pl.BlockSpec((tm, D_MODEL), lambda i, *_: p_map(i)),
                   pl.BlockSpec((tm, D_MODEL), lambda i, *_: s_map(i))],
        scratch_shapes=[pltpu.VMEM((tm, N_SLOTS * CH), BF16),
                        pltpu.VMEM((2, N_SLOTS * CH, D_MODEL), F32),
                        pltpu.SemaphoreType.DMA((2,))],
    )
    return pl.pallas_call(
        functools.partial(_combine_body, n_prompt_tiles=n_prompt_tiles),
        grid_spec=grid_spec,
        out_shape=[jax.ShapeDtypeStruct((t_prompt, D_MODEL), F32),
                   jax.ShapeDtypeStruct((t - t_prompt, D_MODEL), F32)],
        compiler_params=_cparams(("arbitrary",)),
        name="moe_combine",
    )(*scalars, tok, h1, gfin, ys)


def _moe_plan(cnt, n_rows):
    i32 = jnp.int32
    seg = (cnt + SUBLANES - 1) // SUBLANES * SUBLANES
    g_len = jnp.sum(seg, axis=0)
    g_pad = (g_len + TMG - 1) // TMG * TMG
    g_span = g_pad + TMG
    g_end = jnp.cumsum(g_span)
    g_start = g_end - g_span
    seg_start = g_start[None, :] + jnp.cumsum(seg, axis=0) - seg
    n_chunks = (cnt + CH - 1) // CH
    c_end = jnp.cumsum(n_chunks, axis=1)
    tot = c_end[:, -1]
    slot = jnp.arange(N_SLOTS, dtype=i32)[None, :]
    live = slot < tot[:, None]
    ce = jnp.minimum(jnp.sum(slot[:, :, None] >= c_end[:, None, :], axis=-1), N_EXPERTS - 1)
    of_slot = ce[:, :, None] == jnp.arange(N_EXPERTS, dtype=i32)[None, None, :]
    pick = lambda per_expert: jnp.sum(jnp.where(of_slot, per_expert[:, None, :], 0), axis=-1)
    cj = jnp.where(live, slot - pick(c_end - n_chunks), CJ_NONE)
    crow = jnp.where(live, pick(seg_start) + CH * cj, 0)
    tile_row = jnp.arange(n_rows // TMG, dtype=i32) * TMG
    tile_expert = jnp.minimum(jnp.sum(tile_row[:, None] >= g_end[None, :], axis=1), N_EXPERTS - 1)
    live_end = jnp.sum(jnp.where(tile_expert[:, None] == jnp.arange(N_EXPERTS, dtype=i32)[None, :],
                                 (g_start + g_pad)[None, :], 0), axis=-1)
    tile_live = tile_row < live_end
    flat = lambda x: x.reshape(-1).astype(i32)
    return dict(ce=flat(ce), cj=flat(cj), crow=flat(crow), tot=flat(tot),
                zstart=flat(g_start + g_len), zrem=flat(g_pad - g_len),
                tail=jnp.stack([g_end[-1], (n_rows - g_end[-1]) // TMG]).astype(i32),
                tile_expert=flat(tile_expert), tile_live=flat(tile_live))


def _moe_routed(h1, hn, tok, rank_t, cnt, wg, wu, wd, gfin, t_prompt):
    plan = _moe_plan(cnt[:, 0, :N_EXPERTS], _sorted_rows(h1.shape[0]))
    xs = _dispatch(hn, rank_t, plan)
    ys = _ffn_grouped(xs, plan["tile_expert"], plan["tile_live"], wg, wu, wd)
    return _combine(ys, tok, h1, gfin, plan, t_prompt)


def _mix_consts(w_spatial, b_spatial, dec_seq):
    tril = jnp.tril(jnp.ones((CHUNK, CHUNK), F32))
    w_p = w_spatial * tril
    reps = CHUNK // dec_seq
    w_s = jnp.stack([jnp.kron(jnp.eye(reps, dtype=F32), w_p[g, :dec_seq, :dec_seq]) for g in range(G_A)])
    wmix = jnp.stack([w_p, w_s]).astype(BF16)
    b_p = jnp.repeat(b_spatial.T, CH_A, axis=1)
    b_s = jnp.tile(b_p[:dec_seq], (reps, 1))
    return wmix, jnp.stack([b_p, b_s])


def kernel(x_prompt, x_sample, state_hgrn, norm_mix_g, w_in, ln_v_g, ln_v_b, w_spatial, b_spatial,
           lower_bounds, hgrn_norm_g, w_branch_a, w_branch_b, w_out, norm_ffn_g, dense_w_gate,
           dense_w_up, dense_w_down, router_w, moe_w_gate, moe_w_up, moe_w_down, final_norm_g):
    n_seq, seq_len, _ = x_prompt.shape
    dec_batch, dec_seq, _ = x_sample.shape
    t_prompt = n_seq * seq_len
    t_sample = dec_batch * dec_seq
    t = t_prompt + t_sample

    lb_cum = jnp.cumsum(jax.nn.softmax(lower_bounds.astype(F32), axis=0), axis=0)
    lb_all = lb_cum - lb_cum[0:1]
    log_lb, log1m_lb, one_m_lb = jnp.log(lb_all), jnp.log1p(-lb_all), 1.0 - lb_all

    row = lambda p: p.reshape(1, -1)
    h_parts = (x_prompt.reshape(t_prompt, D_MODEL), x_sample.reshape(t_sample, D_MODEL))
    state_in = state_hgrn.astype(F32)
    states_p, states_s, v_rows = [], None, []
    w_in_bf16, w_out_bf16 = {}, {}
    for l in range(DEPTH):
        wmix, bmix = _mix_consts(w_spatial[l], b_spatial[l], dec_seq)
        a, v_s, q, lf, k, iv, gs, sa, sb = _in_proj(
            h_parts, row(norm_mix_g[l]), w_in_bf16.get(l, w_in[l].astype(BF16)), row(ln_v_g[l]),
            row(ln_v_b[l]), row(log_lb[l]), row(log1m_lb[l]), row(one_m_lb[l]), wmix, bmix, t, t_prompt)
        gn = row(hgrn_norm_g[l])
        o_p, s_p = _hgrn_prompt(q, lf, k, iv, gn, n_seq, seq_len)
        o_s, states_s = _hgrn_sample(q, lf, k, iv, state_in, l, states_s, gn, t_prompt, dec_batch, dec_seq)
        mix_w = (w_branch_a[l].astype(BF16), w_branch_b[l].astype(BF16),
                 w_out_bf16.get(l, w_out[l].astype(BF16)), row(norm_ffn_g[l]))
        i = l // 2
        if l % 2 == 0:
            ffn_w = (dense_w_gate[i].astype(BF16), dense_w_up[i].astype(BF16), dense_w_down[i].astype(BF16))
            later_f32 = (moe_w_gate[i].reshape(-1, D_FF), moe_w_up[i].reshape(-1, D_FF),
                         moe_w_down[i].reshape(-1, D_MODEL), w_in[l + 1], w_out[l + 1])
            h_next, *later_bf16 = _post_mix(h_parts, a, o_p, o_s, gs, sa, sb, *mix_w, ffn_w, t_prompt,
                                            "dense", later_f32)
            moe_bf16 = later_bf16[:3]
            w_in_bf16[l + 1], w_out_bf16[l + 1] = later_bf16[3:]
            h_parts = (h_next,)
        else:
            rw = jnp.pad(router_w[i], ((0, 0), (0, LANES - N_EXPERTS)))
            rw_hi = rw.astype(BF16)
            rw_lo = (rw - rw_hi.astype(F32)).astype(BF16)
            h1, hn, tok, rank_t, cnt = _post_mix(h_parts, a, o_p, o_s, gs, sa, sb, *mix_w,
                                                 (rw_hi, rw_lo), t_prompt, "router")
            h_parts = _moe_routed(h1, hn, tok, rank_t, cnt, moe_bf16[0].reshape(moe_w_gate[i].shape),
                                  moe_bf16[1].reshape(moe_w_up[i].shape),
                                  moe_bf16[2].reshape(moe_w_down[i].shape), row(final_norm_g), t_prompt)
        states_p.append(s_p)
        v_rows.append(v_s.reshape(dec_batch, dec_seq, D_A))

    y_prompt = h_parts[0].reshape(n_seq, seq_len, D_MODEL)
    y_sample = h_parts[1].reshape(dec_batch, dec_seq, D_MODEL)
    return (y_prompt, y_sample, jnp.stack(states_p).astype(x_prompt.dtype),
            states_s.astype(state_hgrn.dtype), jnp.stack(v_rows))
```

```python
import functools

import numpy as np
import jax
import jax.numpy as jnp
from jax import lax
from jax.experimental import pallas as pl
from jax.experimental.pallas import tpu as pltpu

F32 = jnp.float32
BF16 = jnp.bfloat16

D_MODEL = 1024
DEPTH = 2
D_A = 512
G_A = 4
CH_A = 128
CHUNK = 128
D_B = 512
H_B = 4
DK = 128
IN_COLS = 5120
D_FF = 2816
N_EXPERTS = 8
TOP_K = 2
EPS = 1e-6

LANES = 128
SUBLANES = 8
VMEM_LIMIT = 52 * 1024 * 1024

TM_IN = 512
SUB_IN = 256
TM_PROJ = 256
CAST_STEPS = 64
TM_MOE = 512
TMG = 256
CH = 64
N_SLOTS = -(-(TOP_K * TM_MOE + N_EXPERTS * (CH - 1)) // CH)
SLOTS_PER_DOT = 8
KEY_MUL = 4096
CJ_NONE = N_SLOTS
ZERO_BITS = (128, 64, 32, 16, 8)
R_HGRN = 512
BLK = 128
LEVEL_HALVES = (64, 32, 16, 8)
SEQ_PER_STEP = 16


def _cparams(sem):
    return pltpu.CompilerParams(dimension_semantics=sem, vmem_limit_bytes=VMEM_LIMIT)


def _dot(a, b):
    return jnp.dot(a, b, preferred_element_type=F32)


def _dot_nt(a, b):
    return lax.dot_general(a, b, (((1,), (1,)), ((), ())), preferred_element_type=F32)


def _split(x, terms):
    out = []
    for _ in range(terms - 1):
        hi = x.astype(BF16)
        out.append(hi)
        x = x - hi.astype(F32)
    out.append(x.astype(BF16))
    return out


def _rms(x, g):
    return x * lax.rsqrt(jnp.mean(x * x, axis=-1, keepdims=True) + EPS) * g


def _gelu(x):
    return 0.5 * x * (1.0 + lax.erf(x * np.float32(2.0 ** -0.5)))


def _sigmoid(x):
    return jax.nn.sigmoid(x)


def _split_rows(n_prompt_tiles):
    return (lambda i: (jnp.minimum(i, n_prompt_tiles - 1), 0),
            lambda i: (jnp.maximum(i - n_prompt_tiles, 0), 0))


def _select_rows(parts, n_prompt_tiles):
    if len(parts) == 1:
        return parts[0][...]
    return jnp.where(pl.program_id(0) < n_prompt_tiles, parts[0][...], parts[1][...])


def _softplus_neg_abs(x):
    return jnp.log(1.0 + jnp.exp(-jnp.abs(x)))


def _in_proj_body(*refs, n_x, n_prompt_tiles):
    x_parts, refs = refs[:n_x], refs[n_x:]
    (g_ref, w_ref, lng_ref, lnb_ref, la_ref, l1m_ref, oml_ref, wmix_ref, bmix_ref,
     a_ref, vs_ref, q_ref, lf_ref, k_ref, iv_ref, gs_ref, sa_ref, sb_ref) = refs
    tm = a_ref.shape[0]
    gate_a, gate_b = 2 * D_A + 4 * D_B, 2 * D_A + 4 * D_B + D_MODEL
    half = D_MODEL // 2
    for r0 in range(0, tm, SUB_IN):
        rs = slice(r0, r0 + SUB_IN)
        xn = _rms(_select_rows([p.at[rs] for p in x_parts], n_prompt_tiles), g_ref[...]).astype(BF16)

        def seg(lo, n):
            return _dot(xn, w_ref[:, lo:lo + n])

        u = _gelu(seg(0, D_A))
        iv_ref[rs, :] = seg(2 * D_A + 2 * D_B, D_B)
        v = _gelu(seg(D_A, D_A))
        vc = v - jnp.mean(v, axis=-1, keepdims=True)
        v = vc * lax.rsqrt(jnp.mean(vc * vc, axis=-1, keepdims=True) + EPS) * lng_ref[...] + lnb_ref[...]
        vs_ref[rs, :] = v
        sa_ref[rs, :half] = _sigmoid(seg(gate_a, half)).astype(BF16)

        vb = v.astype(BF16)
        for c in range(SUB_IN // CHUNK):
            rows = slice(c * CHUNK, (c + 1) * CHUNK)
            out_rows = slice(r0 + c * CHUNK, r0 + (c + 1) * CHUNK)
            for g in range(G_A):
                cols = slice(g * CH_A, (g + 1) * CH_A)
                mixed = _dot(wmix_ref[0, g], vb[rows, cols]) + bmix_ref[0, :, cols]
                a_ref[out_rows, cols] = (u[rows, cols] * mixed).astype(BF16)

        fz = seg(2 * D_A + D_B, D_B)
        log_sig = jnp.minimum(fz, 0.0) - _softplus_neg_abs(fz)
        b = l1m_ref[...] + log_sig
        la = la_ref[...]
        lf_ref[rs, :] = jnp.maximum(la, b) + _softplus_neg_abs(la - b)
        k_ref[rs, :] = oml_ref[...] * _sigmoid(-fz)
        sa_ref[rs, half:] = _sigmoid(seg(gate_a + half, half)).astype(BF16)
        zq = seg(2 * D_A, D_B)
        q_ref[rs, :] = zq * _sigmoid(zq)
        sb_ref[rs, :half] = _sigmoid(seg(gate_b, half)).astype(BF16)
        gz = seg(2 * D_A + 3 * D_B, D_B)
        gs_ref[rs, :] = (gz * _sigmoid(gz)).astype(BF16)
        sb_ref[rs, half:] = _sigmoid(seg(gate_b + half, half)).astype(BF16)


def _in_proj(x_parts, g, w, lng, lnb, la, l1m, oml, wmix, bmix, t, t_prompt):
    tm = TM_IN
    n_prompt_tiles = t_prompt // tm
    row = lambda n: pl.BlockSpec((1, n), lambda i: (0, 0))
    tile = lambda n: pl.BlockSpec((tm, n), lambda i: (i, 0))
    sel = lambda i: jnp.minimum(i // n_prompt_tiles, 1)
    p_map, s_map = _split_rows(n_prompt_tiles)
    if len(x_parts) == 1:
        x_specs = [tile(D_MODEL)]
    else:
        x_specs = [pl.BlockSpec((tm, D_MODEL), p_map), pl.BlockSpec((tm, D_MODEL), s_map)]
    t_sample = t - n_prompt_tiles * tm
    shape = lambda rows, n, dt=F32: jax.ShapeDtypeStruct((rows, n), dt)
    out_shapes = ([shape(t, D_A, BF16), shape(t_sample, D_A)] + [shape(t, D_A)] * 4
                  + [shape(t, D_A, BF16)] + [shape(t, D_MODEL, BF16)] * 2)
    out_specs = ([tile(D_A), pl.BlockSpec((tm, D_A), s_map)] + [tile(D_A)] * 5 + [tile(D_MODEL)] * 2)
    return pl.pallas_call(
        functools.partial(_in_proj_body, n_x=len(x_parts), n_prompt_tiles=n_prompt_tiles),
        grid=(t // tm,),
        in_specs=x_specs + [row(D_MODEL),
                            pl.BlockSpec((D_MODEL, IN_COLS), lambda i: (0, 0), pipeline_mode=pl.Buffered(1)),
                            row(D_A), row(D_A), row(D_B), row(D_B), row(D_B),
                            pl.BlockSpec((1, G_A, CHUNK, CHUNK), lambda i: (sel(i), 0, 0, 0)),
                            pl.BlockSpec((1, CHUNK, D_A), lambda i: (sel(i), 0, 0))],
        out_specs=out_specs,
        out_shape=out_shapes,
        compiler_params=_cparams(("arbitrary",)),
        name="in_proj",
    )(*x_parts, g, w, lng, lnb, la, l1m, oml, wmix, bmix)


def _hgrn_block(q_ref, k_ref, v_ref, g_scr, masks_ref, st, r0, cols, lane, row):
    rows = slice(r0, r0 + BLK)
    qh, kh, vh = q_ref[rows, cols], k_ref[rows, cols], v_ref[rows, cols]
    gh = g_scr[rows, cols]
    g_tot = g_scr[r0 + BLK - 1:r0 + BLK, cols]

    o = _dot_nt((qh * jnp.exp(gh)).astype(BF16), st.astype(BF16))

    sc = None
    for l, m in enumerate(LEVEL_HALVES):
        refs = [jnp.broadcast_to(g_scr[r0 + b + m - 1:r0 + b + m, cols], (2 * m, DK))
                for b in range(0, BLK, 2 * m)]
        g_ref_rows = refs[0] if len(refs) == 1 else jnp.concatenate(refs, axis=0)
        right = (row & m) != 0
        x = jnp.where(right, gh - g_ref_rows, g_ref_rows - gh)
        z = (jnp.where(right, qh, kh) * jnp.exp(x)).astype(BF16)
        part = _dot_nt(z, z) * masks_ref[l]
        sc = part if sc is None else sc + part

    strips = []
    for g0 in range(0, BLK, SUBLANES):
        qg, gg = qh[g0:g0 + SUBLANES], gh[g0:g0 + SUBLANES]
        strip = jnp.zeros((SUBLANES, BLK), F32)
        for s in range(SUBLANES):
            src = r0 + g0 + s
            p = qg * jnp.exp(gg - g_scr[src:src + 1, cols]) * k_ref[src:src + 1, cols]
            strip = jnp.where(lane[:SUBLANES] == g0 + s, jnp.sum(p, axis=-1, keepdims=True), strip)
        strips.append(strip)
    diag = jnp.concatenate(strips, axis=0)
    sc = sc + jnp.where(lane <= row, diag, 0.0)

    o = o + _dot(sc.astype(BF16), vh.astype(BF16))
    kt = (kh * jnp.exp(g_tot - gh)).astype(BF16)
    st_new = st * jnp.exp(g_tot) + _dot(vh.T.astype(BF16), kt)
    return o, st_new


def _hgrn_prompt_body(q_ref, lf_ref, k_ref, v_ref, tri_ref, masks_ref, gn_ref, o_ref, sfin_ref,
                      st_ref, g_scr):
    j = pl.program_id(1)
    r = q_ref.shape[0]

    @pl.when(j == 0)
    def _():
        st_ref[...] = jnp.zeros_like(st_ref)

    tri = tri_ref[...]
    g_scr[...] = sum(_dot(tri, part) for part in _split(lf_ref[...], 2))
    lane = lax.broadcasted_iota(jnp.int32, (BLK, BLK), 1)
    row = lax.broadcasted_iota(jnp.int32, (BLK, BLK), 0)

    states = [st_ref[h] for h in range(H_B)]
    for r0 in range(0, r, BLK):
        for h in range(H_B):
            cols = slice(h * DK, (h + 1) * DK)
            o, states[h] = _hgrn_block(q_ref, k_ref, v_ref, g_scr, masks_ref, states[h], r0, cols, lane, row)
            o_ref[r0:r0 + BLK, cols] = _rms(o, gn_ref[:, cols])
    for h in range(H_B):
        st_ref[h] = states[h]

    @pl.when(j == pl.num_programs(1) - 1)
    def _():
        for h in range(H_B):
            sfin_ref[0, h] = st_ref[h].T


def _hgrn_consts():
    i = np.arange(R_HGRN)[:, None]
    j = np.arange(R_HGRN)[None, :]
    tri = ((i // BLK) == (j // BLK)) & (j <= i)
    i, j = np.arange(BLK)[:, None], np.arange(BLK)[None, :]
    masks = [((i // (2 * m)) == (j // (2 * m))) & ((i & m) != 0) & ((j & m) == 0) for m in LEVEL_HALVES]
    return (jnp.asarray(tri.astype(np.float32), dtype=BF16),
            jnp.asarray(np.stack(masks).astype(np.float32)))


def _hgrn_prompt(q, lf, k, iv, gn, n_seq, seq_len):
    r = R_HGRN
    nblk = seq_len // r
    tri, masks = _hgrn_consts()
    blk = pl.BlockSpec((r, D_B), lambda b, j: (b * nblk + j, 0))
    return pl.pallas_call(
        _hgrn_prompt_body,
        grid=(n_seq, nblk),
        in_specs=[blk, blk, blk, blk,
                  pl.BlockSpec(tri.shape, lambda b, j: (0, 0)),
                  pl.BlockSpec(masks.shape, lambda b, j: (0, 0, 0)),
                  pl.BlockSpec((1, D_B), lambda b, j: (0, 0))],
        out_specs=[blk, pl.BlockSpec((1, H_B, DK, DK), lambda b, j: (b, 0, 0, 0))],
        out_shape=[jax.ShapeDtypeStruct((n_seq * seq_len, D_B), F32),
                   jax.ShapeDtypeStruct((n_seq, H_B, DK, DK), F32)],
        scratch_shapes=[pltpu.VMEM((H_B, DK, DK), F32),
                        pltpu.VMEM((r, D_B), F32)],
        compiler_params=_cparams(("arbitrary", "arbitrary")),
        name="hgrn_prompt",
    )(q, lf, k, iv, tri, masks, gn)


def _sub_chunk_exact(q_ref, k_ref, v_ref, g_ref, r0, cols, sub):
    qg = q_ref[pl.ds(r0, sub), cols]
    gg = g_ref[pl.ds(r0, sub), cols]
    kg = k_ref[pl.ds(r0, sub), cols]
    vg = v_ref[pl.ds(r0, sub), cols]
    row = lax.broadcasted_iota(jnp.int32, (sub, 1), 0)
    acc = jnp.zeros((sub, DK), F32)
    for s in range(sub):
        kb = kg[s:s + 1, :]
        gb = gg[s:s + 1, :]
        vb = vg[s:s + 1, :]
        p = qg * jnp.exp(gg - gb) * kb
        rs = jnp.sum(p, axis=-1, keepdims=True)
        acc = acc + jnp.where(row >= s, rs, 0.0) * vb
    return acc


def _hgrn_sample_body(q_ref, lf_ref, k_ref, v_ref, s0_ref, cm_ref, gn_ref, *rest, dec_seq, out_layer):
    o_ref, s1_ref, g_scr, o_scr = rest[-4:]
    rows = q_ref.shape[0]
    n_seq = rows // dec_seq
    for later in range(s1_ref.shape[0]):
        if later != out_layer:
            s1_ref[later] = jnp.zeros(s1_ref.shape[1:], F32)
    parts = _split(lf_ref[...], 3)

    def cum(idx):
        m = cm_ref[idx]
        return _dot(m, parts[0]) + _dot(m, parts[1]) + _dot(m, parts[2])

    g_all = cum(0)
    g_scr[...] = g_all
    g_rev = cum(1)
    g_tot = cum(2)
    lane = lax.broadcasted_iota(jnp.int32, (DK, rows), 1)

    for h in range(H_B):
        cols = slice(h * DK, (h + 1) * DK)
        qt = (q_ref[:, cols] * jnp.exp(g_all[:, cols])).astype(BF16)
        kt_t = (k_ref[:, cols] * jnp.exp(g_rev[:, cols])).T
        dec_t = jnp.exp(g_tot[:, cols]).T
        vb = v_ref[:, cols].astype(BF16)
        for n in range(n_seq):
            r0 = n * dec_seq
            s0 = s0_ref[0, n, h]
            o_scr[r0:r0 + dec_seq, cols] = _dot(qt[r0:r0 + dec_seq, :], s0.astype(BF16))
            in_seq = (lane >= r0) & (lane < r0 + dec_seq)
            u = _dot(jnp.where(in_seq, kt_t, 0.0).astype(BF16), vb)
            s1_ref[out_layer, n, h] = s0 * dec_t[:, r0:r0 + 1] + u

    def group(n, carry):
        r0 = pl.multiple_of(n * dec_seq, dec_seq)
        for h in range(H_B):
            cols = slice(h * DK, (h + 1) * DK)
            o_scr[pl.ds(r0, dec_seq), cols] += _sub_chunk_exact(q_ref, k_ref, v_ref, g_scr, r0, cols,
                                                                 dec_seq)
        return carry

    lax.fori_loop(0, n_seq, group, 0, unroll=True)

    for h in range(H_B):
        cols = slice(h * DK, (h + 1) * DK)
        o_ref[:, cols] = _rms(o_scr[:, cols], gn_ref[:, cols])


def _hgrn_sample(q, lf, k, iv, state_in, layer, states_out, gn, row0, n_seq, dec_seq):
    rows = SEQ_PER_STEP * dec_seq
    i = np.arange(rows)[:, None]
    j = np.arange(rows)[None, :]
    same = (i // dec_seq) == (j // dec_seq)
    cm = jnp.asarray(np.stack([same & (j <= i), same & (j > i), same]).astype(np.float32), dtype=BF16)
    blk0 = row0 // rows
    blk = pl.BlockSpec((rows, D_B), lambda n: (blk0 + n, 0))
    oblk = pl.BlockSpec((rows, D_B), lambda n: (n, 0))
    depth = state_in.shape[0]
    in_specs = [blk, blk, blk, blk,
                pl.BlockSpec((1, SEQ_PER_STEP, H_B, DK, DK), lambda n: (layer, n, 0, 0, 0)),
                pl.BlockSpec(cm.shape, lambda n: (0, 0, 0)),
                pl.BlockSpec((1, D_B), lambda n: (0, 0))]
    operands = [q, lf, k, iv, state_in, cm, gn]
    if states_out is None:
        sblk = pl.BlockSpec((depth, SEQ_PER_STEP, H_B, DK, DK), lambda n: (0, n, 0, 0, 0))
        out_layer, aliases = layer, {}
    else:
        sblk = pl.BlockSpec((1, SEQ_PER_STEP, H_B, DK, DK), lambda n: (layer, n, 0, 0, 0))
        out_layer, aliases = 0, {len(operands): 1}
        in_specs.append(pl.BlockSpec(memory_space=pl.ANY))
        operands.append(states_out)
    return pl.pallas_call(
        functools.partial(_hgrn_sample_body, dec_seq=dec_seq, out_layer=out_layer),
        grid=(n_seq // SEQ_PER_STEP,),
        in_specs=in_specs,
        out_specs=[oblk, sblk],
        out_shape=[jax.ShapeDtypeStruct((n_seq * dec_seq, D_B), F32),
                   jax.ShapeDtypeStruct((depth, n_seq, H_B, DK, DK), F32)],
        scratch_shapes=[pltpu.VMEM((rows, D_B), F32), pltpu.VMEM((rows, D_B), F32)],
        input_output_aliases=aliases,
        compiler_params=_cparams(("arbitrary",)),
        name="hgrn_sample",
    )(*operands)


FF_SPLITS = ((0, 1536), (1536, 1280))


def _swiglu(hb, wg_ref, wu_ref, wd_ref, idx):
    acc = None
    for lo, n in FF_SPLITS:
        g = _dot(hb, wg_ref[idx + (slice(None), slice(lo, lo + n))])
        u = _dot(hb, wu_ref[idx + (slice(None), slice(lo, lo + n))])
        act = (g * _sigmoid(g) * u).astype(BF16)
        part = _dot(act, wd_ref[idx + (slice(lo, lo + n), slice(None))])
        acc = part if acc is None else acc + part
    return acc


def _route(hn, wh_ref, wl_ref, tok_ref, rank_t_ref, cnt_ref):
    tm = hn.shape[0]
    hh = hn.astype(BF16)
    hl = (hn - hh.astype(F32)).astype(BF16)
    logits = _dot(hh, wh_ref[...]) + _dot(hh, wl_ref[...]) + _dot(hl, wh_ref[...])
    lane = lax.broadcasted_iota(jnp.int32, logits.shape, 1).astype(F32)
    neg = np.float32(-np.inf)
    logits = jnp.where(lane < N_EXPERTS, logits, neg)
    m1 = jnp.max(logits, axis=-1, keepdims=True)
    i1 = jnp.min(jnp.where(logits == m1, lane, float(LANES)), axis=-1, keepdims=True)
    rest = jnp.where(lane == i1, neg, logits)
    m2 = jnp.max(rest, axis=-1, keepdims=True)
    i2 = jnp.min(jnp.where(rest == m2, lane, float(LANES)), axis=-1, keepdims=True)
    e2 = jnp.exp(m2 - m1)
    den = 1.0 + e2

    routed = (lane == i1) | (lane == i2)
    sel = jnp.where(routed, 1.0, 0.0)
    r = lax.broadcasted_iota(jnp.int32, (tm, tm), 0)
    c = lax.broadcasted_iota(jnp.int32, (tm, tm), 1)
    before = jnp.where(c < r, 1.0, 0.0).astype(BF16)
    rank = _dot(before, sel.astype(BF16))
    rank1 = jnp.sum(jnp.where(lane == i1, rank, 0.0), axis=-1, keepdims=True)
    rank2 = jnp.sum(jnp.where(lane == i2, rank, 0.0), axis=-1, keepdims=True)
    tok_ref[...] = jnp.where(lane == 0.0, i1 * KEY_MUL + rank1,
                             jnp.where(lane == 1.0, i2 * KEY_MUL + rank2,
                                       jnp.where(lane == 2.0, 1.0 / den,
                                                 jnp.where(lane == 3.0, e2 / den, 0.0))))
    rank_t = jnp.where(routed, rank, -1.0).T
    for e in range(N_EXPERTS):
        rank_t_ref[0, e] = rank_t[e:e + 1, :]
    cnt_ref[0] = jnp.sum(sel, axis=0, keepdims=True).astype(jnp.int32)


def _post_mix_body(*refs, n_h, n_prompt_tiles, mode):
    h_parts, refs = refs[:n_h], refs[n_h:]
    a_ref, op_ref, os_ref, gs_ref, sa_ref, sb_ref, wa_ref, wb_ref, wo_ref, gf_ref = refs[:10]
    tm = a_ref.shape[0]
    h1s, hns = [], []
    for r0 in range(0, tm, TM_PROJ):
        rows = slice(r0, r0 + TM_PROJ)
        o = _select_rows((op_ref.at[rows], os_ref.at[rows]), n_prompt_tiles)
        pa = _dot(a_ref[rows, :], wa_ref[...])
        pb = _dot((o * gs_ref[rows, :].astype(F32)).astype(BF16), wb_ref[...])
        merged = sa_ref[rows, :].astype(F32) * pa + sb_ref[rows, :].astype(F32) * pb
        h1s.append(_select_rows([p.at[rows] for p in h_parts], n_prompt_tiles)
                   + _dot(merged.astype(BF16), wo_ref[...]))
        hns.append(_rms(h1s[-1], gf_ref[...]))
    h1 = h1s[0] if len(h1s) == 1 else jnp.concatenate(h1s, axis=0)
    hn = hns[0] if len(hns) == 1 else jnp.concatenate(hns, axis=0)
    if mode == "dense":
        wg_ref, wu_ref, wd_ref = refs[10:13]
        n_cast = (len(refs) - 14) // 2
        cast_in, out_ref, cast_out = refs[13:13 + n_cast], refs[13 + n_cast], refs[14 + n_cast:]
        out_ref[...] = h1 + _swiglu(hn.astype(BF16), wg_ref, wu_ref, wd_ref, ())
        for src_ref, dst_ref in zip(cast_in, cast_out):
            dst_ref[...] = src_ref[...].astype(BF16)
    else:
        wh_ref, wl_ref, h1_ref, hn_ref, tok_ref, rank_t_ref, cnt_ref = refs[10:]
        h1_ref[...] = h1
        hn_ref[...] = hn.astype(BF16)
        _route(hn, wh_ref, wl_ref, tok_ref, rank_t_ref, cnt_ref)


def _post_mix(h_parts, a, o_p, o_s, gs, sa, sb, wa, wb, wo, gf, extra_w, t_prompt, mode, to_cast=()):
    t = a.shape[0]
    tm = TM_PROJ if mode == "dense" else TM_MOE
    n_prompt_tiles = t_prompt // tm
    assert not to_cast or t // tm >= CAST_STEPS
    tile = lambda n: pl.BlockSpec((tm, n), lambda i: (i, 0))
    cast_spec = lambda w: pl.BlockSpec((w.shape[0] // CAST_STEPS, w.shape[1]),
                                       lambda i: (jnp.minimum(i, CAST_STEPS - 1), 0))
    full = lambda w: pl.BlockSpec(w.shape, lambda i: (0,) * w.ndim, pipeline_mode=pl.Buffered(1))
    p_map, s_map = _split_rows(n_prompt_tiles)
    if len(h_parts) == 1:
        h_specs = [tile(D_MODEL)]
    else:
        h_specs = [pl.BlockSpec((tm, D_MODEL), p_map), pl.BlockSpec((tm, D_MODEL), s_map)]
    act = jax.ShapeDtypeStruct((t, D_MODEL), F32)
    if mode == "dense":
        out_specs = [tile(D_MODEL)] + [cast_spec(w) for w in to_cast]
        out_shape = [act] + [jax.ShapeDtypeStruct(w.shape, BF16) for w in to_cast]
        scratch = []
    else:
        out_specs = [tile(D_MODEL), tile(D_MODEL), tile(LANES),
                     pl.BlockSpec((1, N_EXPERTS, 1, tm), lambda i: (i, 0, 0, 0)),
                     pl.BlockSpec((1, 1, LANES), lambda i: (i, 0, 0))]
        out_shape = [act, jax.ShapeDtypeStruct((t, D_MODEL), BF16),
                     jax.ShapeDtypeStruct((t, LANES), F32),
                     jax.ShapeDtypeStruct((t // tm, N_EXPERTS, 1, tm), F32),
                     jax.ShapeDtypeStruct((t // tm, 1, LANES), jnp.int32)]
        scratch = []
    weights = (wa, wb, wo, gf) + tuple(extra_w)
    return pl.pallas_call(
        functools.partial(_post_mix_body, n_h=len(h_parts), n_prompt_tiles=n_prompt_tiles, mode=mode),
        grid=(t // tm,),
        in_specs=h_specs + [tile(D_A),
                            pl.BlockSpec((tm, D_B), p_map), pl.BlockSpec((tm, D_B), s_map),
                            tile(D_B), tile(D_MODEL), tile(D_MODEL)] + [full(w) for w in weights]
        + [cast_spec(w) for w in to_cast],
        out_specs=out_specs,
        out_shape=out_shape,
        scratch_shapes=scratch,
        compiler_params=_cparams(("arbitrary",)),
        name="post_mix_" + mode,
    )(*h_parts, a, o_p, o_s, gs, sa, sb, *weights, *to_cast)


def _chunk_copy(src_ref, src_row, dst_ref, dst_row, n, sem):
    return pltpu.make_async_copy(src_ref.at[pl.ds(src_row, n)], dst_ref.at[pl.ds(dst_row, n)], sem)


def _aligned(row):
    return pl.multiple_of(row, SUBLANES)


def _dispatch_body(ce_ref, cj_ref, crow_ref, tot_ref, zstart_ref, zrem_ref, tail_ref,
                   rank_t_ref, hn_ref, xs_ref, sel_scr, stage, zero_ref, sems, zsem):
    i = pl.program_id(0)
    tm = hn_ref.shape[0]
    par = i % 2

    def fills(fn):
        for e in range(N_EXPERTS):
            s = zstart_ref[e]
            fn(_chunk_copy(zero_ref, 0, xs_ref, _aligned(s), TMG, zsem))
            rem = zrem_ref[e]
            for b in ZERO_BITS:
                @pl.when((rem & b) != 0)
                def _():
                    fn(_chunk_copy(zero_ref, 0, xs_ref, _aligned(s + TMG + (rem & ~(2 * b - 1))), b, zsem))

        def tail(n, carry):
            fn(_chunk_copy(zero_ref, 0, xs_ref, _aligned(tail_ref[0] + n * TMG), TMG, zsem))
            return carry

        lax.fori_loop(0, tail_ref[1], tail, 0)

    @pl.when(i == 0)
    def _():
        zero_ref[...] = jnp.zeros_like(zero_ref)
        fills(lambda cp: cp.start())
        fills(lambda cp: cp.wait())

    row = lax.broadcasted_iota(jnp.int32, (CH, tm), 0).astype(F32)
    hb = hn_ref[...]
    for g in range(N_SLOTS // SLOTS_PER_DOT):
        for s in range(g * SLOTS_PER_DOT, (g + 1) * SLOTS_PER_DOT):
            e = ce_ref[i * N_SLOTS + s]
            first = (cj_ref[i * N_SLOTS + s] * CH).astype(F32)
            sel_scr[s * CH:(s + 1) * CH, :] = jnp.where(rank_t_ref[0, e] == row + first, 1.0, 0.0).astype(BF16)
        rows = slice(g * SLOTS_PER_DOT * CH, (g + 1) * SLOTS_PER_DOT * CH)
        stage[par, rows, :] = _dot(sel_scr[rows, :], hb)

    def chunk_copy(s, tile, buf):
        return _chunk_copy(stage.at[buf], s * CH, xs_ref, _aligned(crow_ref[tile * N_SLOTS + s]), CH,
                           sems.at[buf])

    @pl.when(i > 0)
    def _():
        for s in range(N_SLOTS):
            @pl.when(s < tot_ref[i - 1])
            def _():
                chunk_copy(s, i - 1, 1 - par).wait()

    for s in range(N_SLOTS):
        @pl.when(s < tot_ref[i])
        def _():
            chunk_copy(s, i, par).start()

    @pl.when(i == pl.num_programs(0) - 1)
    def _():
        for s in range(N_SLOTS):
            @pl.when(s < tot_ref[i])
            def _():
                chunk_copy(s, i, par).wait()


def _sorted_rows(t):
    n_seg = (t // TM_MOE) * N_EXPERTS
    rows = TOP_K * t + n_seg * (SUBLANES - 1) + N_EXPERTS * (TMG - 1) + N_EXPERTS * TMG
    return -(-rows // TMG) * TMG


def _dispatch(hn, rank_t, plan):
    t = hn.shape[0]
    tm = TM_MOE
    scalars = (plan["ce"], plan["cj"], plan["crow"], plan["tot"], plan["zstart"], plan["zrem"], plan["tail"])
    grid_spec = pltpu.PrefetchScalarGridSpec(
        num_scalar_prefetch=len(scalars),
        grid=(t // tm,),
        in_specs=[pl.BlockSpec((1, N_EXPERTS, 1, tm), lambda i, *_: (i, 0, 0, 0)),
                  pl.BlockSpec((tm, D_MODEL), lambda i, *_: (i, 0))],
        out_specs=pl.BlockSpec(memory_space=pl.ANY),
        scratch_shapes=[pltpu.VMEM((N_SLOTS * CH, tm), BF16),
                        pltpu.VMEM((2, N_SLOTS * CH, D_MODEL), F32),
                        pltpu.VMEM((TMG, D_MODEL), F32),
                        pltpu.SemaphoreType.DMA((2,)), pltpu.SemaphoreType.DMA],
    )
    return pl.pallas_call(
        _dispatch_body,
        grid_spec=grid_spec,
        out_shape=jax.ShapeDtypeStruct((_sorted_rows(t), D_MODEL), F32),
        compiler_params=_cparams(("arbitrary",)),
        name="moe_dispatch",
    )(*scalars, rank_t, hn)


def _ffn_grouped_body(te_ref, live_ref, xs_ref, wg_ref, wu_ref, wd_ref, ys_ref):
    j = pl.program_id(0)

    @pl.when(live_ref[j] != 0)
    def _():
        ys_ref[...] = _swiglu(xs_ref[...].astype(BF16), wg_ref, wu_ref, wd_ref, (0,))

    @pl.when(live_ref[j] == 0)
    def _():
        ys_ref[...] = jnp.zeros_like(ys_ref)


def _ffn_grouped(xs, tile_expert, tile_live, wg, wu, wd):
    n_rows = xs.shape[0]
    row_tile = lambda j, te, live: (j, 0)
    expert = lambda j, te, live: (te[j], 0, 0)
    grid_spec = pltpu.PrefetchScalarGridSpec(
        num_scalar_prefetch=2,
        grid=(n_rows // TMG,),
        in_specs=[pl.BlockSpec((TMG, D_MODEL), row_tile),
                  pl.BlockSpec((1, D_MODEL, D_FF), expert),
                  pl.BlockSpec((1, D_MODEL, D_FF), expert),
                  pl.BlockSpec((1, D_FF, D_MODEL), expert)],
        out_specs=pl.BlockSpec((TMG, D_MODEL), row_tile),
    )
    return pl.pallas_call(
        _ffn_grouped_body,
        grid_spec=grid_spec,
        out_shape=jax.ShapeDtypeStruct((n_rows, D_MODEL), F32),
        compiler_params=_cparams(("arbitrary",)),
        name="moe_ffn",
    )(tile_expert, tile_live, xs, wg, wu, wd)


def _combine_body(ce_ref, cj_ref, crow_ref, tot_ref, tok_ref, h1_ref, gfin_ref, ys_ref,
                  yp_ref, ysm_ref, sel_scr, stage, sems, *, n_prompt_tiles):
    i = pl.program_id(0)
    tm = h1_ref.shape[0]
    par = i % 2

    def fetch(s, tile, buf):
        return _chunk_copy(ys_ref, _aligned(crow_ref[tile * N_SLOTS + s]), stage.at[buf], s * CH, CH,
                           sems.at[buf])

    def fetches(tile, buf, fn):
        for s in range(N_SLOTS):
            @pl.when(s < tot_ref[tile])
            def _():
                fn(fetch(s, tile, buf))

    @pl.when(i == 0)
    def _():
        stage[...] = jnp.zeros_like(stage)
        fetches(0, 0, lambda cp: cp.start())

    @pl.when(i + 1 < pl.num_programs(0))
    def _():
        fetches(i + 1, 1 - par, lambda cp: cp.start())

    tok = tok_ref[...]
    key1, key2 = tok[:, 0:1], tok[:, 1:2]
    gate1, gate2 = tok[:, 2:3], tok[:, 3:4]
    lane = lax.broadcasted_iota(jnp.int32, (1, LANES), 1)
    per_block = LANES // CH
    for blk in range(N_SLOTS // per_block):
        key = jnp.zeros((1, LANES), F32)
        for n in range(per_block):
            s = blk * per_block + n
            first = ce_ref[i * N_SLOTS + s] * KEY_MUL + cj_ref[i * N_SLOTS + s] * CH
            key = jnp.where(lane >= n * CH, (lane - n * CH + first).astype(F32), key)
        sel_scr[:, blk * LANES:(blk + 1) * LANES] = jnp.where(
            key1 == key, gate1, jnp.where(key2 == key, gate2, 0.0)).astype(BF16)

    fetches(i, par, lambda cp: cp.wait())
    ys_tile = stage[par].astype(BF16)
    halves = []
    for r0 in range(0, tm, tm // 2):
        rows = slice(r0, r0 + tm // 2)
        halves.append(_rms(h1_ref[rows, :] + _dot(sel_scr[rows, :], ys_tile), gfin_ref[...]))
    y = jnp.concatenate(halves, axis=0)

    @pl.when(i < n_prompt_tiles)
    def _():
        yp_ref[...] = y

    @pl.when(i >= n_prompt_tiles)
    def _():
        ysm_ref[...] = y


def _combine(ys, tok, h1, gfin, plan, t_prompt):
    t = h1.shape[0]
    tm = TM_MOE
    n_prompt_tiles = t_prompt // tm
    p_map, s_map = _split_rows(n_prompt_tiles)
    scalars = (plan["ce"], plan["cj"], plan["crow"], plan["tot"])
    grid_spec = pltpu.PrefetchScalarGridSpec(
        num_scalar_prefetch=len(scalars),
        grid=(t // tm,),
        in_specs=[pl.BlockSpec((tm, LANES), lambda i, *_: (i, 0)),
                  pl.BlockSpec((tm, D_MODEL), lambda i, *_: (i, 0)),
                  pl.BlockSpec((1, D_MODEL), lambda i, *_: (0, 0)),
                  pl.BlockSpec(memory_space=pl.ANY)],
        out_specs=[pl.BlockSpec((tm, D_MODEL), lambda i, *_: p_map(i)),
                   pl.BlockSpec((tm, D_MODEL), lambda i, *_: s_map(i))],
        scratch_shapes=[pltpu.VMEM((tm, N_SLOTS * CH), BF16),
                        pltpu.VMEM((2, N_SLOTS * CH, D_MODEL), F32),
                        pltpu.SemaphoreType.DMA((2,))],
    )
    return pl.pallas_call(
        functools.partial(_combine_body, n_prompt_tiles=n_prompt_tiles),
        grid_spec=grid_spec,
        out_shape=[jax.ShapeDtypeStruct((t_prompt, D_MODEL), F32),
                   jax.ShapeDtypeStruct((t - t_prompt, D_MODEL), F32)],
        compiler_params=_cparams(("arbitrary",)),
        name="moe_combine",
    )(*scalars, tok, h1, gfin, ys)


def _moe_plan(cnt, n_rows):
    i32 = jnp.int32
    seg = (cnt + SUBLANES - 1) // SUBLANES * SUBLANES
    g_len = jnp.sum(seg, axis=0)
    g_pad = (g_len + TMG - 1) // TMG * TMG
    g_span = g_pad + TMG
    g_end = jnp.cumsum(g_span)
    g_start = g_end - g_span
    seg_start = g_start[None, :] + jnp.cumsum(seg, axis=0) - seg
    n_chunks = (cnt + CH - 1) // CH
    c_end = jnp.cumsum(n_chunks, axis=1)
    tot = c_end[:, -1]
    slot = jnp.arange(N_SLOTS, dtype=i32)[None, :]
    live = slot < tot[:, None]
    ce = jnp.minimum(jnp.sum(slot[:, :, None] >= c_end[:, None, :], axis=-1), N_EXPERTS - 1)
    cj = slot - jnp.take_along_axis(c_end - n_chunks, ce, axis=1)
    cj = jnp.where(live, cj, CJ_NONE)
    crow = jnp.where(live, jnp.take_along_axis(seg_start, ce, axis=1) + CH * cj, 0)
    tile_row = jnp.arange(n_rows // TMG, dtype=i32) * TMG
    tile_expert = jnp.minimum(jnp.sum(tile_row[:, None] >= g_end[None, :], axis=1), N_EXPERTS - 1)
    tile_live = tile_row < (g_start + g_pad)[tile_expert]
    flat = lambda x: x.reshape(-1).astype(i32)
    return dict(ce=flat(ce), cj=flat(cj), crow=flat(crow), tot=flat(tot),
                zstart=flat(g_start + g_len), zrem=flat(g_pad - g_len),
                tail=jnp.stack([g_end[-1], (n_rows - g_end[-1]) // TMG]).astype(i32),
                tile_expert=flat(tile_expert), tile_live=flat(tile_live))


def _moe_routed(h1, hn, tok, rank_t, cnt, wg, wu, wd, gfin, t_prompt):
    plan = _moe_plan(cnt[:, 0, :N_EXPERTS], _sorted_rows(h1.shape[0]))
    xs = _dispatch(hn, rank_t, plan)
    ys = _ffn_grouped(xs, plan["tile_expert"], plan["tile_live"], wg, wu, wd)
    return _combine(ys, tok, h1, gfin, plan, t_prompt)


def _mix_consts(w_spatial, b_spatial, dec_seq):
    tril = jnp.tril(jnp.ones((CHUNK, CHUNK), F32))
    w_p = w_spatial * tril
    reps = CHUNK // dec_seq
    w_s = jnp.stack([jnp.kron(jnp.eye(reps, dtype=F32), w_p[g, :dec_seq, :dec_seq]) for g in range(G_A)])
    wmix = jnp.stack([w_p, w_s]).astype(BF16)
    b_p = jnp.repeat(b_spatial.T, CH_A, axis=1)
    b_s = jnp.tile(b_p[:dec_seq], (reps, 1))
    return wmix, jnp.stack([b_p, b_s])


def kernel(x_prompt, x_sample, state_hgrn, norm_mix_g, w_in, ln_v_g, ln_v_b, w_spatial, b_spatial,
           lower_bounds, hgrn_norm_g, w_branch_a, w_branch_b, w_out, norm_ffn_g, dense_w_gate,
           dense_w_up, dense_w_down, router_w, moe_w_gate, moe_w_up, moe_w_down, final_norm_g):
    n_seq, seq_len, _ = x_prompt.shape
    dec_batch, dec_seq, _ = x_sample.shape
    t_prompt = n_seq * seq_len
    t_sample = dec_batch * dec_seq
    t = t_prompt + t_sample

    lb_cum = jnp.cumsum(jax.nn.softmax(lower_bounds.astype(F32), axis=0), axis=0)
    lb_all = lb_cum - lb_cum[0:1]
    log_lb, log1m_lb, one_m_lb = jnp.log(lb_all), jnp.log1p(-lb_all), 1.0 - lb_all

    row = lambda p: p.reshape(1, -1)
    h_parts = (x_prompt.reshape(t_prompt, D_MODEL), x_sample.reshape(t_sample, D_MODEL))
    state_in = state_hgrn.astype(F32)
    states_p, states_s, v_rows = [], None, []
    for l in range(DEPTH):
        wmix, bmix = _mix_consts(w_spatial[l], b_spatial[l], dec_seq)
        a, v_s, q, lf, k, iv, gs, sa, sb = _in_proj(
            h_parts, row(norm_mix_g[l]), w_in[l].astype(BF16), row(ln_v_g[l]), row(ln_v_b[l]),
            row(log_lb[l]), row(log1m_lb[l]), row(one_m_lb[l]), wmix, bmix, t, t_prompt)
        gn = row(hgrn_norm_g[l])
        o_p, s_p = _hgrn_prompt(q, lf, k, iv, gn, n_seq, seq_len)
        o_s, states_s = _hgrn_sample(q, lf, k, iv, state_in, l, states_s, gn, t_prompt, dec_batch, dec_seq)
        mix_w = (w_branch_a[l].astype(BF16), w_branch_b[l].astype(BF16), w_out[l].astype(BF16),
                 row(norm_ffn_g[l]))
        i = l // 2
        if l % 2 == 0:
            ffn_w = (dense_w_gate[i].astype(BF16), dense_w_up[i].astype(BF16), dense_w_down[i].astype(BF16))
            moe_f32 = (moe_w_gate[i].reshape(-1, D_FF), moe_w_up[i].reshape(-1, D_FF),
                       moe_w_down[i].reshape(-1, D_MODEL))
            h_next, *moe_bf16 = _post_mix(h_parts, a, o_p, o_s, gs, sa, sb, *mix_w, ffn_w, t_prompt,
                                          "dense", moe_f32)
            h_parts = (h_next,)
        else:
            rw = jnp.pad(router_w[i], ((0, 0), (0, LANES - N_EXPERTS)))
            rw_hi = rw.astype(BF16)
            rw_lo = (rw - rw_hi.astype(F32)).astype(BF16)
            h1, hn, tok, rank_t, cnt = _post_mix(h_parts, a, o_p, o_s, gs, sa, sb, *mix_w,
                                                 (rw_hi, rw_lo), t_prompt, "router")
            h_parts = _moe_routed(h1, hn, tok, rank_t, cnt, moe_bf16[0].reshape(moe_w_gate[i].shape),
                                  moe_bf16[1].reshape(moe_w_up[i].shape),
                                  moe_bf16[2].reshape(moe_w_down[i].shape), row(final_norm_g), t_prompt)
        states_p.append(s_p)
        v_rows.append(v_s.reshape(dec_batch, dec_seq, D_A))

    y_prompt = h_parts[0].reshape(n_seq, seq_len, D_MODEL)
    y_sample = h_parts[1].reshape(dec_batch, dec_seq, D_MODEL)
    return (y_prompt, y_sample, jnp.stack(states_p).astype(x_prompt.dtype),
            states_s.astype(state_hgrn.dtype), jnp.stack(v_rows))
```

```python
import functools

import numpy as np
import jax
import jax.numpy as jnp
from jax import lax
from jax.experimental import pallas as pl
from jax.experimental.pallas import tpu as pltpu

F32 = jnp.float32
BF16 = jnp.bfloat16

D_MODEL = 1024
DEPTH = 2
D_A = 512
G_A = 4
CH_A = 128
CHUNK = 128
D_B = 512
H_B = 4
DK = 128
IN_COLS = 5120
D_FF = 2816
N_EXPERTS = 8
TOP_K = 2
EPS = 1e-6

LANES = 128
SUBLANES = 8
VMEM_LIMIT = 52 * 1024 * 1024

TM_IN = 512
SUB_IN = 256
TM_PROJ = 256
CAST_STEPS = 64
TM_MOE = 512
TMG = 256
CH = 64
N_SLOTS = -(-(TOP_K * TM_MOE + N_EXPERTS * (CH - 1)) // CH)
SLOTS_PER_DOT = 8
KEY_MUL = 4096
CJ_NONE = N_SLOTS
SEG_ALIGN = 16
ZERO_BITS = (128, 64, 32, 16)
R_HGRN = 512
BLK = 128
LEVEL_HALVES = (64, 32, 16, 8)
SEQ_PER_STEP = 16


def _cparams(sem):
    return pltpu.CompilerParams(dimension_semantics=sem, vmem_limit_bytes=VMEM_LIMIT)


def _dot(a, b):
    return jnp.dot(a, b, preferred_element_type=F32)


def _dot_nt(a, b):
    return lax.dot_general(a, b, (((1,), (1,)), ((), ())), preferred_element_type=F32)


def _split(x, terms):
    out = []
    for _ in range(terms - 1):
        hi = x.astype(BF16)
        out.append(hi)
        x = x - hi.astype(F32)
    out.append(x.astype(BF16))
    return out


def _rms(x, g):
    return x * lax.rsqrt(jnp.mean(x * x, axis=-1, keepdims=True) + EPS) * g


def _gelu(x):
    return 0.5 * x * (1.0 + lax.erf(x * np.float32(2.0 ** -0.5)))


def _sigmoid(x):
    return jax.nn.sigmoid(x)


def _split_rows(n_prompt_tiles):
    return (lambda i: (jnp.minimum(i, n_prompt_tiles - 1), 0),
            lambda i: (jnp.maximum(i - n_prompt_tiles, 0), 0))


def _select_rows(parts, n_prompt_tiles):
    if len(parts) == 1:
        return parts[0][...]
    return jnp.where(pl.program_id(0) < n_prompt_tiles, parts[0][...], parts[1][...])


def _softplus_neg_abs(x):
    return jnp.log(1.0 + jnp.exp(-jnp.abs(x)))


def _in_proj_body(*refs, n_x, n_prompt_tiles):
    x_parts, refs = refs[:n_x], refs[n_x:]
    (g_ref, w_ref, lng_ref, lnb_ref, la_ref, l1m_ref, oml_ref, wmix_ref, bmix_ref,
     a_ref, vs_ref, q_ref, lf_ref, k_ref, iv_ref, gs_ref, sa_ref, sb_ref) = refs
    tm = a_ref.shape[0]
    gate_a, gate_b = 2 * D_A + 4 * D_B, 2 * D_A + 4 * D_B + D_MODEL
    half = D_MODEL // 2
    for r0 in range(0, tm, SUB_IN):
        rs = slice(r0, r0 + SUB_IN)
        xn = _rms(_select_rows([p.at[rs] for p in x_parts], n_prompt_tiles), g_ref[...]).astype(BF16)

        def seg(lo, n):
            return _dot(xn, w_ref[:, lo:lo + n])

        u = _gelu(seg(0, D_A))
        iv_ref[rs, :] = seg(2 * D_A + 2 * D_B, D_B)
        v = _gelu(seg(D_A, D_A))
        vc = v - jnp.mean(v, axis=-1, keepdims=True)
        v = vc * lax.rsqrt(jnp.mean(vc * vc, axis=-1, keepdims=True) + EPS) * lng_ref[...] + lnb_ref[...]
        vs_ref[rs, :] = v
        sa_ref[rs, :half] = _sigmoid(seg(gate_a, half)).astype(BF16)

        vb = v.astype(BF16)
        for c in range(SUB_IN // CHUNK):
            rows = slice(c * CHUNK, (c + 1) * CHUNK)
            out_rows = slice(r0 + c * CHUNK, r0 + (c + 1) * CHUNK)
            for g in range(G_A):
                cols = slice(g * CH_A, (g + 1) * CH_A)
                mixed = _dot(wmix_ref[0, g], vb[rows, cols]) + bmix_ref[0, :, cols]
                a_ref[out_rows, cols] = (u[rows, cols] * mixed).astype(BF16)

        fz = seg(2 * D_A + D_B, D_B)
        log_sig = jnp.minimum(fz, 0.0) - _softplus_neg_abs(fz)
        b = l1m_ref[...] + log_sig
        la = la_ref[...]
        lf_ref[rs, :] = jnp.maximum(la, b) + _softplus_neg_abs(la - b)
        k_ref[rs, :] = oml_ref[...] * _sigmoid(-fz)
        sa_ref[rs, half:] = _sigmoid(seg(gate_a + half, half)).astype(BF16)
        zq = seg(2 * D_A, D_B)
        q_ref[rs, :] = zq * _sigmoid(zq)
        sb_ref[rs, :half] = _sigmoid(seg(gate_b, half)).astype(BF16)
        gz = seg(2 * D_A + 3 * D_B, D_B)
        gs_ref[rs, :] = (gz * _sigmoid(gz)).astype(BF16)
        sb_ref[rs, half:] = _sigmoid(seg(gate_b + half, half)).astype(BF16)


def _in_proj(x_parts, g, w, lng, lnb, la, l1m, oml, wmix, bmix, t, t_prompt):
    tm = TM_IN
    n_prompt_tiles = t_prompt // tm
    row = lambda n: pl.BlockSpec((1, n), lambda i: (0, 0))
    tile = lambda n: pl.BlockSpec((tm, n), lambda i: (i, 0))
    sel = lambda i: jnp.minimum(i // n_prompt_tiles, 1)
    p_map, s_map = _split_rows(n_prompt_tiles)
    if len(x_parts) == 1:
        x_specs = [tile(D_MODEL)]
    else:
        x_specs = [pl.BlockSpec((tm, D_MODEL), p_map), pl.BlockSpec((tm, D_MODEL), s_map)]
    t_sample = t - n_prompt_tiles * tm
    shape = lambda rows, n, dt=F32: jax.ShapeDtypeStruct((rows, n), dt)
    out_shapes = ([shape(t, D_A, BF16), shape(t_sample, D_A)] + [shape(t, D_A)] * 4
                  + [shape(t, D_A, BF16)] + [shape(t, D_MODEL, BF16)] * 2)
    out_specs = ([tile(D_A), pl.BlockSpec((tm, D_A), s_map)] + [tile(D_A)] * 5 + [tile(D_MODEL)] * 2)
    return pl.pallas_call(
        functools.partial(_in_proj_body, n_x=len(x_parts), n_prompt_tiles=n_prompt_tiles),
        grid=(t // tm,),
        in_specs=x_specs + [row(D_MODEL),
                            pl.BlockSpec((D_MODEL, IN_COLS), lambda i: (0, 0), pipeline_mode=pl.Buffered(1)),
                            row(D_A), row(D_A), row(D_B), row(D_B), row(D_B),
                            pl.BlockSpec((1, G_A, CHUNK, CHUNK), lambda i: (sel(i), 0, 0, 0)),
                            pl.BlockSpec((1, CHUNK, D_A), lambda i: (sel(i), 0, 0))],
        out_specs=out_specs,
        out_shape=out_shapes,
        compiler_params=_cparams(("arbitrary",)),
        name="in_proj",
    )(*x_parts, g, w, lng, lnb, la, l1m, oml, wmix, bmix)


def _hgrn_block(q_ref, k_ref, v_ref, g_scr, masks_ref, st, r0, cols, lane, row):
    rows = slice(r0, r0 + BLK)
    qh, kh, vh = q_ref[rows, cols], k_ref[rows, cols], v_ref[rows, cols]
    gh = g_scr[rows, cols]
    g_tot = g_scr[r0 + BLK - 1:r0 + BLK, cols]

    o = _dot_nt((qh * jnp.exp(gh)).astype(BF16), st.astype(BF16))

    sc = None
    for l, m in enumerate(LEVEL_HALVES):
        refs = [jnp.broadcast_to(g_scr[r0 + b + m - 1:r0 + b + m, cols], (2 * m, DK))
                for b in range(0, BLK, 2 * m)]
        g_ref_rows = refs[0] if len(refs) == 1 else jnp.concatenate(refs, axis=0)
        right = (row & m) != 0
        x = jnp.where(right, gh - g_ref_rows, g_ref_rows - gh)
        z = (jnp.where(right, qh, kh) * jnp.exp(x)).astype(BF16)
        part = _dot_nt(z, z) * masks_ref[l]
        sc = part if sc is None else sc + part

    strips = []
    for g0 in range(0, BLK, SUBLANES):
        qg, gg = qh[g0:g0 + SUBLANES], gh[g0:g0 + SUBLANES]
        strip = jnp.zeros((SUBLANES, BLK), F32)
        for s in range(SUBLANES):
            src = r0 + g0 + s
            p = qg * jnp.exp(gg - g_scr[src:src + 1, cols]) * k_ref[src:src + 1, cols]
            strip = jnp.where(lane[:SUBLANES] == g0 + s, jnp.sum(p, axis=-1, keepdims=True), strip)
        strips.append(strip)
    diag = jnp.concatenate(strips, axis=0)
    sc = sc + jnp.where(lane <= row, diag, 0.0)

    o = o + _dot(sc.astype(BF16), vh.astype(BF16))
    kt = (kh * jnp.exp(g_tot - gh)).astype(BF16)
    st_new = st * jnp.exp(g_tot) + _dot(vh.T.astype(BF16), kt)
    return o, st_new


def _hgrn_prompt_body(q_ref, lf_ref, k_ref, v_ref, tri_ref, masks_ref, gn_ref, o_ref, sfin_ref,
                      st_ref, g_scr):
    j = pl.program_id(1)
    r = q_ref.shape[0]

    @pl.when(j == 0)
    def _():
        st_ref[...] = jnp.zeros_like(st_ref)

    tri = tri_ref[...]
    g_scr[...] = sum(_dot(tri, part) for part in _split(lf_ref[...], 2))
    lane = lax.broadcasted_iota(jnp.int32, (BLK, BLK), 1)
    row = lax.broadcasted_iota(jnp.int32, (BLK, BLK), 0)

    states = [st_ref[h] for h in range(H_B)]
    for r0 in range(0, r, BLK):
        for h in range(H_B):
            cols = slice(h * DK, (h + 1) * DK)
            o, states[h] = _hgrn_block(q_ref, k_ref, v_ref, g_scr, masks_ref, states[h], r0, cols, lane, row)
            o_ref[r0:r0 + BLK, cols] = _rms(o, gn_ref[:, cols])
    for h in range(H_B):
        st_ref[h] = states[h]

    @pl.when(j == pl.num_programs(1) - 1)
    def _():
        for h in range(H_B):
            sfin_ref[0, h] = st_ref[h].T


def _hgrn_consts():
    i = np.arange(R_HGRN)[:, None]
    j = np.arange(R_HGRN)[None, :]
    tri = ((i // BLK) == (j // BLK)) & (j <= i)
    i, j = np.arange(BLK)[:, None], np.arange(BLK)[None, :]
    masks = [((i // (2 * m)) == (j // (2 * m))) & ((i & m) != 0) & ((j & m) == 0) for m in LEVEL_HALVES]
    return (jnp.asarray(tri.astype(np.float32), dtype=BF16),
            jnp.asarray(np.stack(masks).astype(np.float32)))


def _hgrn_prompt(q, lf, k, iv, gn, n_seq, seq_len):
    r = R_HGRN
    nblk = seq_len // r
    tri, masks = _hgrn_consts()
    blk = pl.BlockSpec((r, D_B), lambda b, j: (b * nblk + j, 0))
    return pl.pallas_call(
        _hgrn_prompt_body,
        grid=(n_seq, nblk),
        in_specs=[blk, blk, blk, blk,
                  pl.BlockSpec(tri.shape, lambda b, j: (0, 0)),
                  pl.BlockSpec(masks.shape, lambda b, j: (0, 0, 0)),
                  pl.BlockSpec((1, D_B), lambda b, j: (0, 0))],
        out_specs=[blk, pl.BlockSpec((1, H_B, DK, DK), lambda b, j: (b, 0, 0, 0))],
        out_shape=[jax.ShapeDtypeStruct((n_seq * seq_len, D_B), F32),
                   jax.ShapeDtypeStruct((n_seq, H_B, DK, DK), F32)],
        scratch_shapes=[pltpu.VMEM((H_B, DK, DK), F32),
                        pltpu.VMEM((r, D_B), F32)],
        compiler_params=_cparams(("arbitrary", "arbitrary")),
        name="hgrn_prompt",
    )(q, lf, k, iv, tri, masks, gn)


def _sub_chunk_exact(q_ref, k_ref, v_ref, g_ref, r0, cols, sub):
    qg = q_ref[pl.ds(r0, sub), cols]
    gg = g_ref[pl.ds(r0, sub), cols]
    kg = k_ref[pl.ds(r0, sub), cols]
    vg = v_ref[pl.ds(r0, sub), cols]
    row = lax.broadcasted_iota(jnp.int32, (sub, 1), 0)
    acc = jnp.zeros((sub, DK), F32)
    for s in range(sub):
        kb = kg[s:s + 1, :]
        gb = gg[s:s + 1, :]
        vb = vg[s:s + 1, :]
        p = qg * jnp.exp(gg - gb) * kb
        rs = jnp.sum(p, axis=-1, keepdims=True)
        acc = acc + jnp.where(row >= s, rs, 0.0) * vb
    return acc


def _hgrn_sample_body(q_ref, lf_ref, k_ref, v_ref, s0_ref, cm_ref, gn_ref, *rest, dec_seq, out_layer):
    o_ref, s1_ref, g_scr, o_scr = rest[-4:]
    rows = q_ref.shape[0]
    n_seq = rows // dec_seq
    for later in range(s1_ref.shape[0]):
        if later != out_layer:
            s1_ref[later] = jnp.zeros(s1_ref.shape[1:], F32)
    parts = _split(lf_ref[...], 3)

    def cum(idx):
        m = cm_ref[idx]
        return _dot(m, parts[0]) + _dot(m, parts[1]) + _dot(m, parts[2])

    g_all = cum(0)
    g_scr[...] = g_all
    g_rev = cum(1)
    g_tot = cum(2)
    lane = lax.broadcasted_iota(jnp.int32, (DK, rows), 1)

    for h in range(H_B):
        cols = slice(h * DK, (h + 1) * DK)
        qt = (q_ref[:, cols] * jnp.exp(g_all[:, cols])).astype(BF16)
        kt_t = (k_ref[:, cols] * jnp.exp(g_rev[:, cols])).T
        dec_t = jnp.exp(g_tot[:, cols]).T
        vb = v_ref[:, cols].astype(BF16)
        for n in range(n_seq):
            r0 = n * dec_seq
            s0 = s0_ref[0, n, h]
            o_scr[r0:r0 + dec_seq, cols] = _dot(qt[r0:r0 + dec_seq, :], s0.astype(BF16))
            in_seq = (lane >= r0) & (lane < r0 + dec_seq)
            u = _dot(jnp.where(in_seq, kt_t, 0.0).astype(BF16), vb)
            s1_ref[out_layer, n, h] = s0 * dec_t[:, r0:r0 + 1] + u

    def group(n, carry):
        r0 = pl.multiple_of(n * dec_seq, dec_seq)
        for h in range(H_B):
            cols = slice(h * DK, (h + 1) * DK)
            o_scr[pl.ds(r0, dec_seq), cols] += _sub_chunk_exact(q_ref, k_ref, v_ref, g_scr, r0, cols,
                                                                 dec_seq)
        return carry

    lax.fori_loop(0, n_seq, group, 0, unroll=True)

    for h in range(H_B):
        cols = slice(h * DK, (h + 1) * DK)
        o_ref[:, cols] = _rms(o_scr[:, cols], gn_ref[:, cols])


def _hgrn_sample(q, lf, k, iv, state_in, layer, states_out, gn, row0, n_seq, dec_seq):
    rows = SEQ_PER_STEP * dec_seq
    i = np.arange(rows)[:, None]
    j = np.arange(rows)[None, :]
    same = (i // dec_seq) == (j // dec_seq)
    cm = jnp.asarray(np.stack([same & (j <= i), same & (j > i), same]).astype(np.float32), dtype=BF16)
    blk0 = row0 // rows
    blk = pl.BlockSpec((rows, D_B), lambda n: (blk0 + n, 0))
    oblk = pl.BlockSpec((rows, D_B), lambda n: (n, 0))
    depth = state_in.shape[0]
    in_specs = [blk, blk, blk, blk,
                pl.BlockSpec((1, SEQ_PER_STEP, H_B, DK, DK), lambda n: (layer, n, 0, 0, 0)),
                pl.BlockSpec(cm.shape, lambda n: (0, 0, 0)),
                pl.BlockSpec((1, D_B), lambda n: (0, 0))]
    operands = [q, lf, k, iv, state_in, cm, gn]
    if states_out is None:
        sblk = pl.BlockSpec((depth, SEQ_PER_STEP, H_B, DK, DK), lambda n: (0, n, 0, 0, 0))
        out_layer, aliases = layer, {}
    else:
        sblk = pl.BlockSpec((1, SEQ_PER_STEP, H_B, DK, DK), lambda n: (layer, n, 0, 0, 0))
        out_layer, aliases = 0, {len(operands): 1}
        in_specs.append(pl.BlockSpec(memory_space=pl.ANY))
        operands.append(states_out)
    return pl.pallas_call(
        functools.partial(_hgrn_sample_body, dec_seq=dec_seq, out_layer=out_layer),
        grid=(n_seq // SEQ_PER_STEP,),
        in_specs=in_specs,
        out_specs=[oblk, sblk],
        out_shape=[jax.ShapeDtypeStruct((n_seq * dec_seq, D_B), F32),
                   jax.ShapeDtypeStruct((depth, n_seq, H_B, DK, DK), F32)],
        scratch_shapes=[pltpu.VMEM((rows, D_B), F32), pltpu.VMEM((rows, D_B), F32)],
        input_output_aliases=aliases,
        compiler_params=_cparams(("arbitrary",)),
        name="hgrn_sample",
    )(*operands)


FF_SPLITS = ((0, 1536), (1536, 1280))


def _swiglu(hb, wg_ref, wu_ref, wd_ref, idx):
    acc = None
    for lo, n in FF_SPLITS:
        g = _dot(hb, wg_ref[idx + (slice(None), slice(lo, lo + n))])
        u = _dot(hb, wu_ref[idx + (slice(None), slice(lo, lo + n))])
        act = (g * _sigmoid(g) * u).astype(BF16)
        part = _dot(act, wd_ref[idx + (slice(lo, lo + n), slice(None))])
        acc = part if acc is None else acc + part
    return acc


def _route(hn, wh_ref, wl_ref, tok_ref, rank_t_ref, cnt_ref):
    tm = hn.shape[0]
    hh = hn.astype(BF16)
    hl = (hn - hh.astype(F32)).astype(BF16)
    logits = _dot(hh, wh_ref[...]) + _dot(hh, wl_ref[...]) + _dot(hl, wh_ref[...])
    lane = lax.broadcasted_iota(jnp.int32, logits.shape, 1).astype(F32)
    neg = np.float32(-np.inf)
    logits = jnp.where(lane < N_EXPERTS, logits, neg)
    m1 = jnp.max(logits, axis=-1, keepdims=True)
    i1 = jnp.min(jnp.where(logits == m1, lane, float(LANES)), axis=-1, keepdims=True)
    rest = jnp.where(lane == i1, neg, logits)
    m2 = jnp.max(rest, axis=-1, keepdims=True)
    i2 = jnp.min(jnp.where(rest == m2, lane, float(LANES)), axis=-1, keepdims=True)
    e2 = jnp.exp(m2 - m1)
    den = 1.0 + e2

    routed = (lane == i1) | (lane == i2)
    sel = jnp.where(routed, 1.0, 0.0)
    r = lax.broadcasted_iota(jnp.int32, (tm, tm), 0)
    c = lax.broadcasted_iota(jnp.int32, (tm, tm), 1)
    before = jnp.where(c < r, 1.0, 0.0).astype(BF16)
    rank = _dot(before, sel.astype(BF16))
    rank1 = jnp.sum(jnp.where(lane == i1, rank, 0.0), axis=-1, keepdims=True)
    rank2 = jnp.sum(jnp.where(lane == i2, rank, 0.0), axis=-1, keepdims=True)
    tok_ref[...] = jnp.where(lane == 0.0, i1 * KEY_MUL + rank1,
                             jnp.where(lane == 1.0, i2 * KEY_MUL + rank2,
                                       jnp.where(lane == 2.0, 1.0 / den,
                                                 jnp.where(lane == 3.0, e2 / den, 0.0))))
    rank_t = jnp.where(routed, rank, -1.0).T
    for e in range(N_EXPERTS):
        rank_t_ref[0, e] = rank_t[e:e + 1, :]
    cnt_ref[0] = jnp.sum(sel, axis=0, keepdims=True).astype(jnp.int32)


def _post_mix_body(*refs, n_h, n_prompt_tiles, mode):
    h_parts, refs = refs[:n_h], refs[n_h:]
    a_ref, op_ref, os_ref, gs_ref, sa_ref, sb_ref, wa_ref, wb_ref, wo_ref, gf_ref = refs[:10]
    tm = a_ref.shape[0]
    h1s, hns = [], []
    for r0 in range(0, tm, TM_PROJ):
        rows = slice(r0, r0 + TM_PROJ)
        o = _select_rows((op_ref.at[rows], os_ref.at[rows]), n_prompt_tiles)
        pa = _dot(a_ref[rows, :], wa_ref[...])
        pb = _dot((o * gs_ref[rows, :].astype(F32)).astype(BF16), wb_ref[...])
        merged = sa_ref[rows, :].astype(F32) * pa + sb_ref[rows, :].astype(F32) * pb
        h1s.append(_select_rows([p.at[rows] for p in h_parts], n_prompt_tiles)
                   + _dot(merged.astype(BF16), wo_ref[...]))
        hns.append(_rms(h1s[-1], gf_ref[...]))
    h1 = h1s[0] if len(h1s) == 1 else jnp.concatenate(h1s, axis=0)
    hn = hns[0] if len(hns) == 1 else jnp.concatenate(hns, axis=0)
    if mode == "dense":
        wg_ref, wu_ref, wd_ref = refs[10:13]
        n_cast = (len(refs) - 14) // 2
        cast_in, out_ref, cast_out = refs[13:13 + n_cast], refs[13 + n_cast], refs[14 + n_cast:]
        out_ref[...] = h1 + _swiglu(hn.astype(BF16), wg_ref, wu_ref, wd_ref, ())
        for src_ref, dst_ref in zip(cast_in, cast_out):
            dst_ref[...] = src_ref[...].astype(BF16)
    else:
        wh_ref, wl_ref, h1_ref, hn_ref, tok_ref, rank_t_ref, cnt_ref = refs[10:]
        h1_ref[...] = h1
        hn_ref[...] = hn.astype(BF16)
        _route(hn, wh_ref, wl_ref, tok_ref, rank_t_ref, cnt_ref)


def _post_mix(h_parts, a, o_p, o_s, gs, sa, sb, wa, wb, wo, gf, extra_w, t_prompt, mode, to_cast=()):
    t = a.shape[0]
    tm = TM_PROJ if mode == "dense" else TM_MOE
    n_prompt_tiles = t_prompt // tm
    assert not to_cast or t // tm >= CAST_STEPS
    tile = lambda n: pl.BlockSpec((tm, n), lambda i: (i, 0))
    cast_spec = lambda w: pl.BlockSpec((w.shape[0] // CAST_STEPS, w.shape[1]),
                                       lambda i: (jnp.minimum(i, CAST_STEPS - 1), 0))
    full = lambda w: pl.BlockSpec(w.shape, lambda i: (0,) * w.ndim, pipeline_mode=pl.Buffered(1))
    p_map, s_map = _split_rows(n_prompt_tiles)
    if len(h_parts) == 1:
        h_specs = [tile(D_MODEL)]
    else:
        h_specs = [pl.BlockSpec((tm, D_MODEL), p_map), pl.BlockSpec((tm, D_MODEL), s_map)]
    act = jax.ShapeDtypeStruct((t, D_MODEL), F32)
    if mode == "dense":
        out_specs = [tile(D_MODEL)] + [cast_spec(w) for w in to_cast]
        out_shape = [act] + [jax.ShapeDtypeStruct(w.shape, BF16) for w in to_cast]
        scratch = []
    else:
        out_specs = [tile(D_MODEL), tile(D_MODEL), tile(LANES),
                     pl.BlockSpec((1, N_EXPERTS, 1, tm), lambda i: (i, 0, 0, 0)),
                     pl.BlockSpec((1, 1, LANES), lambda i: (i, 0, 0))]
        out_shape = [act, jax.ShapeDtypeStruct((t, D_MODEL), BF16),
                     jax.ShapeDtypeStruct((t, LANES), F32),
                     jax.ShapeDtypeStruct((t // tm, N_EXPERTS, 1, tm), F32),
                     jax.ShapeDtypeStruct((t // tm, 1, LANES), jnp.int32)]
        scratch = []
    weights = (wa, wb, wo, gf) + tuple(extra_w)
    return pl.pallas_call(
        functools.partial(_post_mix_body, n_h=len(h_parts), n_prompt_tiles=n_prompt_tiles, mode=mode),
        grid=(t // tm,),
        in_specs=h_specs + [tile(D_A),
                            pl.BlockSpec((tm, D_B), p_map), pl.BlockSpec((tm, D_B), s_map),
                            tile(D_B), tile(D_MODEL), tile(D_MODEL)] + [full(w) for w in weights]
        + [cast_spec(w) for w in to_cast],
        out_specs=out_specs,
        out_shape=out_shape,
        scratch_shapes=scratch,
        compiler_params=_cparams(("arbitrary",)),
        name="post_mix_" + mode,
    )(*h_parts, a, o_p, o_s, gs, sa, sb, *weights, *to_cast)


def _chunk_copy(src_ref, src_row, dst_ref, dst_row, n, sem):
    return pltpu.make_async_copy(src_ref.at[pl.ds(src_row, n)], dst_ref.at[pl.ds(dst_row, n)], sem)


def _aligned(row):
    return pl.multiple_of(row, SEG_ALIGN)


def _dispatch_body(ce_ref, cj_ref, crow_ref, tot_ref, zstart_ref, zrem_ref, tail_ref,
                   rank_t_ref, hn_ref, xs_ref, sel_scr, stage, zero_ref, sems, zsem):
    i = pl.program_id(0)
    tm = hn_ref.shape[0]
    par = i % 2

    def fills(fn):
        for e in range(N_EXPERTS):
            s = zstart_ref[e]
            fn(_chunk_copy(zero_ref, 0, xs_ref, _aligned(s), TMG, zsem))
            rem = zrem_ref[e]
            for b in ZERO_BITS:
                @pl.when((rem & b) != 0)
                def _():
                    fn(_chunk_copy(zero_ref, 0, xs_ref, _aligned(s + TMG + (rem & ~(2 * b - 1))), b, zsem))

        def tail(n, carry):
            fn(_chunk_copy(zero_ref, 0, xs_ref, _aligned(tail_ref[0] + n * TMG), TMG, zsem))
            return carry

        lax.fori_loop(0, tail_ref[1], tail, 0)

    @pl.when(i == 0)
    def _():
        zero_ref[...] = jnp.zeros_like(zero_ref)
        fills(lambda cp: cp.start())
        fills(lambda cp: cp.wait())

    row = lax.broadcasted_iota(jnp.int32, (CH, tm), 0).astype(F32)
    hb = hn_ref[...]
    for g in range(N_SLOTS // SLOTS_PER_DOT):
        for s in range(g * SLOTS_PER_DOT, (g + 1) * SLOTS_PER_DOT):
            e = ce_ref[i * N_SLOTS + s]
            first = (cj_ref[i * N_SLOTS + s] * CH).astype(F32)
            sel_scr[s * CH:(s + 1) * CH, :] = jnp.where(rank_t_ref[0, e] == row + first, 1.0, 0.0).astype(BF16)
        rows = slice(g * SLOTS_PER_DOT * CH, (g + 1) * SLOTS_PER_DOT * CH)
        stage[par, rows, :] = _dot(sel_scr[rows, :], hb).astype(BF16)

    def chunk_copy(s, tile, buf):
        return _chunk_copy(stage.at[buf], s * CH, xs_ref, _aligned(crow_ref[tile * N_SLOTS + s]), CH,
                           sems.at[buf])

    @pl.when(i > 0)
    def _():
        for s in range(N_SLOTS):
            @pl.when(s < tot_ref[i - 1])
            def _():
                chunk_copy(s, i - 1, 1 - par).wait()

    for s in range(N_SLOTS):
        @pl.when(s < tot_ref[i])
        def _():
            chunk_copy(s, i, par).start()

    @pl.when(i == pl.num_programs(0) - 1)
    def _():
        for s in range(N_SLOTS):
            @pl.when(s < tot_ref[i])
            def _():
                chunk_copy(s, i, par).wait()


def _sorted_rows(t):
    n_seg = (t // TM_MOE) * N_EXPERTS
    rows = TOP_K * t + n_seg * (SEG_ALIGN - 1) + N_EXPERTS * (TMG - 1) + N_EXPERTS * TMG
    return -(-rows // TMG) * TMG


def _dispatch(hn, rank_t, plan):
    t = hn.shape[0]
    tm = TM_MOE
    scalars = (plan["ce"], plan["cj"], plan["crow"], plan["tot"], plan["zstart"], plan["zrem"], plan["tail"])
    grid_spec = pltpu.PrefetchScalarGridSpec(
        num_scalar_prefetch=len(scalars),
        grid=(t // tm,),
        in_specs=[pl.BlockSpec((1, N_EXPERTS, 1, tm), lambda i, *_: (i, 0, 0, 0)),
                  pl.BlockSpec((tm, D_MODEL), lambda i, *_: (i, 0))],
        out_specs=pl.BlockSpec(memory_space=pl.ANY),
        scratch_shapes=[pltpu.VMEM((N_SLOTS * CH, tm), BF16),
                        pltpu.VMEM((2, N_SLOTS * CH, D_MODEL), BF16),
                        pltpu.VMEM((TMG, D_MODEL), BF16),
                        pltpu.SemaphoreType.DMA((2,)), pltpu.SemaphoreType.DMA],
    )
    return pl.pallas_call(
        _dispatch_body,
        grid_spec=grid_spec,
        out_shape=jax.ShapeDtypeStruct((_sorted_rows(t), D_MODEL), BF16),
        compiler_params=_cparams(("arbitrary",)),
        name="moe_dispatch",
    )(*scalars, rank_t, hn)


def _ffn_grouped_body(te_ref, live_ref, xs_ref, wg_ref, wu_ref, wd_ref, ys_ref):
    j = pl.program_id(0)

    @pl.when(live_ref[j] != 0)
    def _():
        ys_ref[...] = _swiglu(xs_ref[...], wg_ref, wu_ref, wd_ref, (0,)).astype(BF16)

    @pl.when(live_ref[j] == 0)
    def _():
        ys_ref[...] = jnp.zeros_like(ys_ref)


def _ffn_grouped(xs, tile_expert, tile_live, wg, wu, wd):
    n_rows = xs.shape[0]
    row_tile = lambda j, te, live: (j, 0)
    expert = lambda j, te, live: (te[j], 0, 0)
    grid_spec = pltpu.PrefetchScalarGridSpec(
        num_scalar_prefetch=2,
        grid=(n_rows // TMG,),
        in_specs=[pl.BlockSpec((TMG, D_MODEL), row_tile),
                  pl.BlockSpec((1, D_MODEL, D_FF), expert),
                  pl.BlockSpec((1, D_MODEL, D_FF), expert),
                  pl.BlockSpec((1, D_FF, D_MODEL), expert)],
        out_specs=pl.BlockSpec((TMG, D_MODEL), row_tile),
    )
    return pl.pallas_call(
        _ffn_grouped_body,
        grid_spec=grid_spec,
        out_shape=jax.ShapeDtypeStruct((n_rows, D_MODEL), BF16),
        compiler_params=_cparams(("arbitrary",)),
        name="moe_ffn",
    )(tile_expert, tile_live, xs, wg, wu, wd)


def _combine_body(ce_ref, cj_ref, crow_ref, tot_ref, tok_ref, h1_ref, gfin_ref, ys_ref,
                  yp_ref, ysm_ref, sel_scr, stage, sems, *, n_prompt_tiles):
    i = pl.program_id(0)
    tm = h1_ref.shape[0]
    par = i % 2

    def fetch(s, tile, buf):
        return _chunk_copy(ys_ref, _aligned(crow_ref[tile * N_SLOTS + s]), stage.at[buf], s * CH, CH,
                           sems.at[buf])

    def fetches(tile, buf, fn):
        for s in range(N_SLOTS):
            @pl.when(s < tot_ref[tile])
            def _():
                fn(fetch(s, tile, buf))

    @pl.when(i == 0)
    def _():
        stage[...] = jnp.zeros_like(stage)
        fetches(0, 0, lambda cp: cp.start())

    @pl.when(i + 1 < pl.num_programs(0))
    def _():
        fetches(i + 1, 1 - par, lambda cp: cp.start())

    tok = tok_ref[...]
    key1, key2 = tok[:, 0:1], tok[:, 1:2]
    gate1, gate2 = tok[:, 2:3], tok[:, 3:4]
    lane = lax.broadcasted_iota(jnp.int32, (1, LANES), 1)
    per_block = LANES // CH
    for blk in range(N_SLOTS // per_block):
        key = jnp.zeros((1, LANES), F32)
        for n in range(per_block):
            s = blk * per_block + n
            first = ce_ref[i * N_SLOTS + s] * KEY_MUL + cj_ref[i * N_SLOTS + s] * CH
            key = jnp.where(lane >= n * CH, (lane - n * CH + first).astype(F32), key)
        sel_scr[:, blk * LANES:(blk + 1) * LANES] = jnp.where(
            key1 == key, gate1, jnp.where(key2 == key, gate2, 0.0)).astype(BF16)

    fetches(i, par, lambda cp: cp.wait())
    ys_tile = stage[par]
    halves = []
    for r0 in range(0, tm, tm // 2):
        rows = slice(r0, r0 + tm // 2)
        halves.append(_rms(h1_ref[rows, :] + _dot(sel_scr[rows, :], ys_tile), gfin_ref[...]))
    y = jnp.concatenate(halves, axis=0)

    @pl.when(i < n_prompt_tiles)
    def _():
        yp_ref[...] = y

    @pl.when(i >= n_prompt_tiles)
    def _():
        ysm_ref[...] = y


def _combine(ys, tok, h1, gfin, plan, t_prompt):
    t = h1.shape[0]
    tm = TM_MOE
    n_prompt_tiles = t_prompt // tm
    p_map, s_map = _split_rows(n_prompt_tiles)
    scalars = (plan["ce"], plan["cj"], plan["crow"], plan["tot"])
    grid_spec = pltpu.PrefetchScalarGridSpec(
        num_scalar_prefetch=len(scalars),
        grid=(t // tm,),
        in_specs=[pl.BlockSpec((tm, LANES), lambda i, *_: (i, 0)),
                  pl.BlockSpec((tm, D_MODEL), lambda i, *_: (i, 0)),
                  pl.BlockSpec((1, D_MODEL), lambda i, *_: (0, 0)),
                  pl.BlockSpec(memory_space=pl.ANY)],
        out_specs=[pl.BlockSpec((tm, D_MODEL), lambda i, *_: p_map(i)),
                   pl.BlockSpec((tm, D_MODEL), lambda i, *_: s_map(i))],
        scratch_shapes=[pltpu.VMEM((tm, N_SLOTS * CH), BF16),
                        pltpu.VMEM((2, N_SLOTS * CH, D_MODEL), BF16),
                        pltpu.SemaphoreType.DMA((2,))],
    )
    return pl.pallas_call(
        functools.partial(_combine_body, n_prompt_tiles=n_prompt_tiles),
        grid_spec=grid_spec,
        out_shape=[jax.ShapeDtypeStruct((t_prompt, D_MODEL), F32),
                   jax.ShapeDtypeStruct((t - t_prompt, D_MODEL), F32)],
        compiler_params=_cparams(("arbitrary",)),
        name="moe_combine",
    )(*scalars, tok, h1, gfin, ys)


def _moe_plan(cnt, n_rows):
    i32 = jnp.int32
    seg = (cnt + SEG_ALIGN - 1) // SEG_ALIGN * SEG_ALIGN
    g_len = jnp.sum(seg, axis=0)
    g_pad = (g_len + TMG - 1) // TMG * TMG
    g_span = g_pad + TMG
    g_end = jnp.cumsum(g_span)
    g_start = g_end - g_span
    seg_start = g_start[None, :] + jnp.cumsum(seg, axis=0) - seg
    n_chunks = (cnt + CH - 1) // CH
    c_end = jnp.cumsum(n_chunks, axis=1)
    tot = c_end[:, -1]
    slot = jnp.arange(N_SLOTS, dtype=i32)[None, :]
    live = slot < tot[:, None]
    ce = jnp.minimum(jnp.sum(slot[:, :, None] >= c_end[:, None, :], axis=-1), N_EXPERTS - 1)
    cj = slot - jnp.take_along_axis(c_end - n_chunks, ce, axis=1)
    cj = jnp.where(live, cj, CJ_NONE)
    crow = jnp.where(live, jnp.take_along_axis(seg_start, ce, axis=1) + CH * cj, 0)
    tile_row = jnp.arange(n_rows // TMG, dtype=i32) * TMG
    tile_expert = jnp.minimum(jnp.sum(tile_row[:, None] >= g_end[None, :], axis=1), N_EXPERTS - 1)
    tile_live = tile_row < (g_start + g_pad)[tile_expert]
    flat = lambda x: x.reshape(-1).astype(i32)
    return dict(ce=flat(ce), cj=flat(cj), crow=flat(crow), tot=flat(tot),
                zstart=flat(g_start + g_len), zrem=flat(g_pad - g_len),
                tail=jnp.stack([g_end[-1], (n_rows - g_end[-1]) // TMG]).astype(i32),
                tile_expert=flat(tile_expert), tile_live=flat(tile_live))


def _moe_routed(h1, hn, tok, rank_t, cnt, wg, wu, wd, gfin, t_prompt):
    plan = _moe_plan(cnt[:, 0, :N_EXPERTS], _sorted_rows(h1.shape[0]))
    xs = _dispatch(hn, rank_t, plan)
    ys = _ffn_grouped(xs, plan["tile_expert"], plan["tile_live"], wg, wu, wd)
    return _combine(ys, tok, h1, gfin, plan, t_prompt)


def _mix_consts(w_spatial, b_spatial, dec_seq):
    tril = jnp.tril(jnp.ones((CHUNK, CHUNK), F32))
    w_p = w_spatial * tril
    reps = CHUNK // dec_seq
    w_s = jnp.stack([jnp.kron(jnp.eye(reps, dtype=F32), w_p[g, :dec_seq, :dec_seq]) for g in range(G_A)])
    wmix = jnp.stack([w_p, w_s]).astype(BF16)
    b_p = jnp.repeat(b_spatial.T, CH_A, axis=1)
    b_s = jnp.tile(b_p[:dec_seq], (reps, 1))
    return wmix, jnp.stack([b_p, b_s])


def kernel(x_prompt, x_sample, state_hgrn, norm_mix_g, w_in, ln_v_g, ln_v_b, w_spatial, b_spatial,
           lower_bounds, hgrn_norm_g, w_branch_a, w_branch_b, w_out, norm_ffn_g, dense_w_gate,
           dense_w_up, dense_w_down, router_w, moe_w_gate, moe_w_up, moe_w_down, final_norm_g):
    n_seq, seq_len, _ = x_prompt.shape
    dec_batch, dec_seq, _ = x_sample.shape
    t_prompt = n_seq * seq_len
    t_sample = dec_batch * dec_seq
    t = t_prompt + t_sample

    lb_cum = jnp.cumsum(jax.nn.softmax(lower_bounds.astype(F32), axis=0), axis=0)
    lb_all = lb_cum - lb_cum[0:1]
    log_lb, log1m_lb, one_m_lb = jnp.log(lb_all), jnp.log1p(-lb_all), 1.0 - lb_all

    row = lambda p: p.reshape(1, -1)
    h_parts = (x_prompt.reshape(t_prompt, D_MODEL), x_sample.reshape(t_sample, D_MODEL))
    state_in = state_hgrn.astype(F32)
    states_p, states_s, v_rows = [], None, []
    for l in range(DEPTH):
        wmix, bmix = _mix_consts(w_spatial[l], b_spatial[l], dec_seq)
        a, v_s, q, lf, k, iv, gs, sa, sb = _in_proj(
            h_parts, row(norm_mix_g[l]), w_in[l].astype(BF16), row(ln_v_g[l]), row(ln_v_b[l]),
            row(log_lb[l]), row(log1m_lb[l]), row(one_m_lb[l]), wmix, bmix, t, t_prompt)
        gn = row(hgrn_norm_g[l])
        o_p, s_p = _hgrn_prompt(q, lf, k, iv, gn, n_seq, seq_len)
        o_s, states_s = _hgrn_sample(q, lf, k, iv, state_in, l, states_s, gn, t_prompt, dec_batch, dec_seq)
        mix_w = (w_branch_a[l].astype(BF16), w_branch_b[l].astype(BF16), w_out[l].astype(BF16),
                 row(norm_ffn_g[l]))
        i = l // 2
        if l % 2 == 0:
            ffn_w = (dense_w_gate[i].astype(BF16), dense_w_up[i].astype(BF16), dense_w_down[i].astype(BF16))
            moe_f32 = (moe_w_gate[i].reshape(-1, D_FF), moe_w_up[i].reshape(-1, D_FF),
                       moe_w_down[i].reshape(-1, D_MODEL))
            h_next, *moe_bf16 = _post_mix(h_parts, a, o_p, o_s, gs, sa, sb, *mix_w, ffn_w, t_prompt,
                                          "dense", moe_f32)
            h_parts = (h_next,)
        else:
            rw = jnp.pad(router_w[i], ((0, 0), (0, LANES - N_EXPERTS)))
            rw_hi = rw.astype(BF16)
            rw_lo = (rw - rw_hi.astype(F32)).astype(BF16)
            h1, hn, tok, rank_t, cnt = _post_mix(h_parts, a, o_p, o_s, gs, sa, sb, *mix_w,
                                                 (rw_hi, rw_lo), t_prompt, "router")
            h_parts = _moe_routed(h1, hn, tok, rank_t, cnt, moe_bf16[0].reshape(moe_w_gate[i].shape),
                                  moe_bf16[1].reshape(moe_w_up[i].shape),
                                  moe_bf16[2].reshape(moe_w_down[i].shape), row(final_norm_g), t_prompt)
        states_p.append(s_p)
        v_rows.append(v_s.reshape(dec_batch, dec_seq, D_A))

    y_prompt = h_parts[0].reshape(n_seq, seq_len, D_MODEL)
    y_sample = h_parts[1].reshape(dec_batch, dec_seq, D_MODEL)
    return (y_prompt, y_sample, jnp.stack(states_p).astype(x_prompt.dtype),
            states_s.astype(state_hgrn.dtype), jnp.stack(v_rows))
```
